```python
import jax, jax.numpy as jnp
from jax import lax
import numpy as np

D_MODEL = 1024
BATCH = 4
SEQ = 4096
DEPTH = 2
DEC_BATCH = 8
DEC_SEQ = 4096
PAST_LEN = 128

HEAD_DIM = 64
N_HEADS = D_MODEL // HEAD_DIM
GRID_W = 64
A_HEADS = (3 * N_HEADS) // 8
A_KV = 2
A_WINDOW = 128
A_BLOCK = 128
B_HEADS = N_HEADS // 4
B_MAX_ROWS = 8
B_COLS = 16
C_HEADS = N_HEADS - A_HEADS - B_HEADS
C_KV = 2
C_BLOCK = 128
ROPE_THETA = 10000.0
N_GROUPS = 4
EXPERTS_PER_GROUP = 4
N_EXPERTS = N_GROUPS * EXPERTS_PER_GROUP
TOP_K = 2
D_EXPERT = D_MODEL // 2
EPS = 1e-6
NEG = -1e30

A_WIDTH = A_HEADS * HEAD_DIM
B_WIDTH = B_HEADS * HEAD_DIM
C_WIDTH = C_HEADS * HEAD_DIM
IN_SIZES = [A_WIDTH, A_KV * HEAD_DIM, A_KV * HEAD_DIM,
            B_WIDTH, B_WIDTH, B_WIDTH,
            C_WIDTH, C_KV * HEAD_DIM, C_KV * HEAD_DIM]
IN_COLS = sum(IN_SIZES)
IN_SPLITS = [sum(IN_SIZES[:i + 1]) for i in range(len(IN_SIZES) - 1)]

kernel_name = 'hybrid_parallel_head_encoder'


def rms_norm(x, g):
    xf = x.astype(jnp.float32)
    y = xf * lax.rsqrt(jnp.mean(xf * xf, axis=-1, keepdims=True) + EPS)
    return (y * g.astype(jnp.float32)).astype(x.dtype)


def window_attention(q, k, v, sink):
    bsz, t, h, dh = q.shape
    g = h // A_KV
    nb = t // A_BLOCK
    pad = ((0, 0), (A_BLOCK, A_BLOCK), (0, 0), (0, 0))
    kp = jnp.pad(k, pad).reshape(bsz, nb + 2, A_BLOCK, A_KV, dh)
    vp = jnp.pad(v, pad).reshape(bsz, nb + 2, A_BLOCK, A_KV, dh)
    kb = jnp.concatenate([kp[:, :-2], kp[:, 1:-1], kp[:, 2:]], axis=2)
    vb = jnp.concatenate([vp[:, :-2], vp[:, 1:-1], vp[:, 2:]], axis=2)
    qb = q.reshape(bsz, nb, A_BLOCK, A_KV, g, dh)
    s = jnp.einsum('bnqhgd,bnkhd->bnhgqk', qb, kb).astype(jnp.float32) * (dh ** -0.5)
    i = jnp.arange(A_BLOCK)
    j = jnp.arange(3 * A_BLOCK)
    dist = (A_BLOCK + i[:, None] - j[None, :]).astype(jnp.float32)
    kpos = (jnp.arange(nb)[:, None] - 1) * A_BLOCK + j[None, :]
    mask = (jnp.abs(dist)[None] <= A_WINDOW) & ((kpos >= 0) & (kpos < t))[:, None, :]
    slopes = jnp.asarray(np.array([2.0 ** (-8.0 * (n + 1) / A_HEADS) for n in range(A_HEADS)], np.float32))
    slopes = slopes.reshape(A_KV, g)
    s = s - slopes[:, :, None, None] * jnp.abs(dist)
    s = jnp.where(mask[None, :, None, None], s, NEG)
    sink_col = jnp.broadcast_to(sink.astype(jnp.float32).reshape(A_KV, g)[:, :, None, None],
                                s.shape[:-1] + (1,))
    p = jax.nn.softmax(jnp.concatenate([s, sink_col], axis=-1), axis=-1)[..., :-1]
    o = jnp.einsum('bnhgqk,bnkhd->bnqhgd', p.astype(v.dtype), vb)
    return o.reshape(bsz, t, h, dh)


def neighborhood_attention(q, k, v, rpb):
    bsz, t, h, dh = q.shape
    rows = t // GRID_W
    kh = min(B_MAX_ROWS, rows)
    r = jnp.arange(rows)
    r0 = jnp.clip(r - kh // 2, 0, rows - kh)
    key_rows = r0[:, None] + jnp.arange(kh)[None, :]
    kg = k.reshape(bsz, rows, GRID_W, h, dh)[:, key_rows]
    vg = v.reshape(bsz, rows, GRID_W, h, dh)[:, key_rows]
    qg = q.reshape(bsz, rows, GRID_W, h, dh)
    s = jnp.einsum('brqhd,brkwhd->brhqkw', qg, kg).astype(jnp.float32) * (dh ** -0.5)
    c = jnp.arange(GRID_W)
    c0 = jnp.clip(c - B_COLS // 2, 0, GRID_W - B_COLS)
    col_ok = (c[None, :] >= c0[:, None]) & (c[None, :] < c0[:, None] + B_COLS)
    dr = key_rows - r[:, None]
    dc = jnp.clip(c[None, :] - c[:, None], -(B_COLS - 1), B_COLS - 1)
    bias = rpb.astype(jnp.float32)[:, dr[:, :, None, None] + (B_MAX_ROWS - 1),
                                   dc[None, None] + (B_COLS - 1)]
    s = s + bias.transpose(1, 0, 3, 2, 4)[None]
    s = jnp.where(col_ok[:, None, :], s, NEG)
    p = jax.nn.softmax(s.reshape(bsz, rows, h, GRID_W, kh * GRID_W), axis=-1)
    p = p.reshape(bsz, rows, h, GRID_W, kh, GRID_W)
    o = jnp.einsum('brhqkw,brkwhd->brqhd', p.astype(v.dtype), vg)
    return o.reshape(bsz, t, h, dh)


def axial_rope(x, row, col):
    dh = x.shape[-1]
    half = dh // 2
    nf = half // 2
    inv = ROPE_THETA ** (-jnp.arange(nf, dtype=jnp.float32) / nf)
    xf = x.astype(jnp.float32)

    def rot(xp, pos):
        ang = pos.astype(jnp.float32)[:, None] * inv[None, :]
        cos = jnp.cos(ang)[None, :, None, :]
        sin = jnp.sin(ang)[None, :, None, :]
        x1, x2 = xp[..., :nf], xp[..., nf:]
        return jnp.concatenate([x1 * cos - x2 * sin, x1 * sin + x2 * cos], axis=-1)

    return jnp.concatenate([rot(xf[..., :half], row), rot(xf[..., half:], col)], axis=-1).astype(x.dtype)


def global_attention(q, k, v):
    bsz, t, h, dh = q.shape
    g = h // C_KV
    nb = t // C_BLOCK
    qb = q.reshape(bsz, nb, C_BLOCK, C_KV, g, dh).transpose(1, 0, 2, 3, 4, 5)

    def one_block(qblk):
        s = jnp.einsum('bqhgd,bkhd->bhgqk', qblk, k).astype(jnp.float32) * (dh ** -0.5)
        p = jax.nn.softmax(s, axis=-1)
        return jnp.einsum('bhgqk,bkhd->bqhgd', p.astype(v.dtype), v)

    o = lax.map(one_block, qb)
    return o.transpose(1, 0, 2, 3, 4, 5).reshape(bsz, t, h, dh)


def hierarchical_moe(h, w_rg, b_rg, w_re, b_re, w_gate, w_up, w_down):
    bsz, t, d = h.shape
    hf = h.reshape(-1, d)
    n = hf.shape[0]
    g_logits = (hf @ w_rg).astype(jnp.float32) + b_rg.astype(jnp.float32)
    g_prob = jax.nn.softmax(g_logits, axis=-1)
    grp = jnp.argmax(g_logits, axis=-1)
    p_grp = jnp.take_along_axis(g_prob, grp[:, None], axis=-1)
    e_logits = ((hf @ w_re).astype(jnp.float32) + b_re.astype(jnp.float32)).reshape(n, N_GROUPS, EXPERTS_PER_GROUP)
    e_sel = jnp.take_along_axis(e_logits, grp[:, None, None], axis=1)[:, 0]
    top_v, top_i = lax.top_k(e_sel, TOP_K)
    w = jax.nn.softmax(top_v, axis=-1) * p_grp
    expert_id = grp[:, None] * EXPERTS_PER_GROUP + top_i
    gates = jnp.sum(jax.nn.one_hot(expert_id, N_EXPERTS, dtype=jnp.float32) * w[..., None], axis=1)
    gates = gates.astype(h.dtype)
    y = jnp.zeros_like(hf)
    for e in range(N_EXPERTS):
        a = jax.nn.silu(hf @ w_gate[e]) * (hf @ w_up[e])
        y = y + gates[:, e:e + 1] * (a @ w_down[e])
    return y.reshape(bsz, t, d)


def encoder_layer(x, ln1, w_in, qk_gain, sink, rpb, out_gain, w_out, ln2,
                  w_rg, b_rg, w_re, b_re, w_gate, w_up, w_down):
    bsz, t, _ = x.shape
    h = rms_norm(x, ln1)
    proj = jnp.einsum('btd,dc->btc', h, w_in)
    qa, ka, va, qb, kb, vb, qc, kc, vc = jnp.split(proj, IN_SPLITS, axis=-1)

    def heads(z):
        return z.reshape(bsz, t, -1, HEAD_DIM)

    pos = jnp.arange(t)
    row = pos // GRID_W
    col = pos % GRID_W
    o_a = window_attention(rms_norm(heads(qa), qk_gain[0, 0]), rms_norm(heads(ka), qk_gain[0, 1]),
                           heads(va), sink)
    o_b = neighborhood_attention(rms_norm(heads(qb), qk_gain[1, 0]), rms_norm(heads(kb), qk_gain[1, 1]),
                                 heads(vb), rpb)
    o_c = global_attention(axial_rope(rms_norm(heads(qc), qk_gain[2, 0]), row, col),
                           axial_rope(rms_norm(heads(kc), qk_gain[2, 1]), row, col),
                           heads(vc))
    o = jnp.concatenate([
        rms_norm(o_a.reshape(bsz, t, A_WIDTH), out_gain[:A_WIDTH]),
        rms_norm(o_b.reshape(bsz, t, B_WIDTH), out_gain[A_WIDTH:A_WIDTH + B_WIDTH]),
        rms_norm(o_c.reshape(bsz, t, C_WIDTH), out_gain[A_WIDTH + B_WIDTH:]),
    ], axis=-1)
    x = x + jnp.einsum('btc,cd->btd', o, w_out)
    x = x + hierarchical_moe(rms_norm(x, ln2), w_rg, b_rg, w_re, b_re, w_gate, w_up, w_down)
    return x


def run_trunk(x, ln1, w_in, qk_gain, sink, rpb, out_gain, w_out, ln2,
              w_rg, b_rg, w_re, b_re, w_gate, w_up, w_down):
    for l in range(DEPTH):
        x = encoder_layer(x, ln1[l], w_in[l], qk_gain[l], sink[l], rpb[l], out_gain[l], w_out[l], ln2[l],
                          w_rg[l], b_rg[l], w_re[l], b_re[l], w_gate[l], w_up[l], w_down[l])
    return x


def setup_inputs(seed: int = 0) -> dict:
    key = jax.random.key(seed)
    ks = jax.random.split(key, 20)
    f32 = jnp.float32
    nrm = lambda k, shape, scale: jax.random.normal(k, shape, f32) * scale
    return {
        'x_prompt': nrm(ks[0], (BATCH, SEQ, D_MODEL), 1.0),
        'x_sample': nrm(ks[1], (DEC_BATCH, DEC_SEQ, D_MODEL), 1.0),
        'ln1': 1.0 + nrm(ks[2], (DEPTH, D_MODEL), 0.05),
        'w_in': nrm(ks[3], (DEPTH, D_MODEL, IN_COLS), D_MODEL ** -0.5),
        'qk_gain': 1.0 + nrm(ks[4], (DEPTH, 3, 2, HEAD_DIM), 0.05),
        'sink': nrm(ks[5], (DEPTH, A_HEADS), 0.5),
        'rpb': nrm(ks[6], (DEPTH, B_HEADS, 2 * B_MAX_ROWS - 1, 2 * B_COLS - 1), 0.1),
        'out_gain': 1.0 + nrm(ks[7], (DEPTH, D_MODEL), 0.05),
        'w_out': nrm(ks[8], (DEPTH, D_MODEL, D_MODEL), D_MODEL ** -0.5),
        'ln2': 1.0 + nrm(ks[9], (DEPTH, D_MODEL), 0.05),
        'w_router_group': nrm(ks[10], (DEPTH, D_MODEL, N_GROUPS), D_MODEL ** -0.5),
        'b_router_group': nrm(ks[11], (DEPTH, N_GROUPS), 0.01),
        'w_router_expert': nrm(ks[12], (DEPTH, D_MODEL, N_EXPERTS), D_MODEL ** -0.5),
        'b_router_expert': nrm(ks[13], (DEPTH, N_EXPERTS), 0.01),
        'w_gate': nrm(ks[14], (DEPTH, N_EXPERTS, D_MODEL, D_EXPERT), D_MODEL ** -0.5),
        'w_up': nrm(ks[15], (DEPTH, N_EXPERTS, D_MODEL, D_EXPERT), D_MODEL ** -0.5),
        'w_down': nrm(ks[16], (DEPTH, N_EXPERTS, D_EXPERT, D_MODEL), D_EXPERT ** -0.5),
    }


def reference(x_prompt, x_sample, ln1, w_in, qk_gain, sink, rpb, out_gain, w_out, ln2,
              w_router_group, b_router_group, w_router_expert, b_router_expert, w_gate, w_up, w_down):
    y_prompt = run_trunk(x_prompt, ln1, w_in, qk_gain, sink, rpb, out_gain, w_out, ln2,
                         w_router_group, b_router_group, w_router_expert, b_router_expert, w_gate, w_up, w_down)
    y_sample = run_trunk(x_sample, ln1, w_in, qk_gain, sink, rpb, out_gain, w_out, ln2,
                         w_router_group, b_router_group, w_router_expert, b_router_expert, w_gate, w_up, w_down)
    return (y_prompt, y_sample)
```

```python
import functools
import math

import numpy as np
import jax
import jax.numpy as jnp
from jax import lax
from jax.experimental import pallas as pl
from jax.experimental.pallas import tpu as pltpu

D_MODEL = 1024
SEQ = 4096
HEAD_DIM = 64
GRID_W = 64
GRID_ROWS = SEQ // GRID_W
A_HEADS, A_KV, A_WINDOW = 6, 2, 128
B_HEADS, B_ROWS, B_COLS = 4, 8, 16
C_HEADS, C_KV = 6, 2
ROPE_THETA = 10000.0
N_GROUPS, EXPERTS_PER_GROUP = 4, 4
N_EXPERTS = N_GROUPS * EXPERTS_PER_GROUP
D_EXPERT = D_MODEL // 2
EPS = 1e-6
NEG = -1e30
LOG2E = math.log2(math.e)

A_WIDTH = A_HEADS * HEAD_DIM
B_WIDTH = B_HEADS * HEAD_DIM
C_WIDTH = C_HEADS * HEAD_DIM
KV_WIDTH = A_KV * HEAD_DIM

LANES = 128
MXU_DIM = 256
VMEM_LIMIT = 56 * 1024 * 1024

PAIR_ORDER = (0, 3, 1, 4, 2, 5)

TM_PROJ = 512
TQ_A = 256
KW_A = 512
TQ_B = 256
KW_B = 768
TQ_C = 256
KC_C = 512
TM_MOE = 1024

BF16 = jnp.bfloat16
F32 = jnp.float32

_ROPE_SWAP = np.concatenate([np.arange(16, 32), np.arange(0, 16), np.arange(48, 64), np.arange(32, 48)])


def _cparams(sem):
    return pltpu.CompilerParams(dimension_semantics=sem, vmem_limit_bytes=VMEM_LIMIT)


N_NORMED = 1536
N_GAINED = 2048
N_PROJ = 2560


def _in_proj_kernel(x_ref, w_ref, ln_ref, gain_ref, cos_ref, sin_ref, ones_ref,
                    qa_ref, ka_ref, va_ref, qb_ref, kb_ref, vb_ref, qc_ref, kc_ref, vc_ref):
    x = x_ref[...]
    ms = jnp.mean(x * x, axis=-1, keepdims=True)
    h = (x * lax.rsqrt(ms + EPS) * ln_ref[...]).astype(BF16)
    p = jnp.dot(h, w_ref[...], preferred_element_type=F32)
    g = gain_ref[...]
    ones_blk = ones_ref[...]

    def chunk(c):
        return p[:, MXU_DIM * c:MXU_DIM * (c + 1)]

    def inv_rms(pc):
        ss = jnp.dot((pc * pc).astype(BF16), ones_blk, preferred_element_type=F32)
        return lax.rsqrt(ss * (1.0 / HEAD_DIM) + EPS)

    r = [inv_rms(chunk(c)) for c in range(N_NORMED // MXU_DIM)]
    pn = [chunk(c) * r[c] * g[:, MXU_DIM * c:MXU_DIM * (c + 1)] for c in range(N_NORMED // MXU_DIM)]
    qa_ref[:, 0:256] = pn[0].astype(BF16)
    qa_ref[:, 256:384] = pn[1][:, :128].astype(BF16)
    ka_ref[...] = pn[1][:, 128:].astype(BF16)
    qb_ref[...] = pn[2].astype(BF16)
    kb_ref[...] = pn[3].astype(BF16)
    cos = cos_ref[...]
    sin = sin_ref[...]
    cos2 = jnp.concatenate([cos, cos], axis=1)
    sin2 = jnp.concatenate([sin, sin], axis=1)
    sw4 = chunk(6) * r[4] * g[:, 1536:1792]
    sw5 = chunk(7) * r[5] * g[:, 1792:2048]
    c4 = pn[4] * cos2 + sw4 * sin2
    c5 = pn[5] * cos2 + sw5 * sin2
    qc_ref[:, 0:256] = c4.astype(BF16)
    qc_ref[:, 256:384] = c5[:, :128].astype(BF16)
    kc_ref[...] = c5[:, 128:].astype(BF16)
    va_ref[...] = p[:, 2048:2176].astype(BF16)
    vb_ref[...] = p[:, 2176:2432].astype(BF16)
    vc_ref[:, 0:128] = p[:, 2432:2560].astype(BF16)
    vc_ref[:, 128:256] = jnp.ones((x.shape[0], 128), BF16)


def _in_proj(x2d, w_ext, ln, gain, cos_t, sin_t, ones_blk):
    n = x2d.shape[0]
    tm = TM_PROJ
    tiles_per_seq = SEQ // tm
    row = lambda i: (i, 0)
    fixed = lambda i: (0, 0)
    pos = lambda i: (i % tiles_per_seq, 0)
    widths = (A_WIDTH, KV_WIDTH, KV_WIDTH, B_WIDTH, B_WIDTH, B_WIDTH, C_WIDTH, KV_WIDTH, 2 * KV_WIDTH)
    return pl.pallas_call(
        _in_proj_kernel,
        grid=(n // tm,),
        in_specs=[
            pl.BlockSpec((tm, D_MODEL), row),
            pl.BlockSpec((D_MODEL, N_PROJ), fixed),
            pl.BlockSpec((1, D_MODEL), fixed),
            pl.BlockSpec((1, N_GAINED), fixed),
            pl.BlockSpec((tm, LANES), pos),
            pl.BlockSpec((tm, LANES), pos),
            pl.BlockSpec((MXU_DIM, MXU_DIM), fixed),
        ],
        out_specs=[pl.BlockSpec((tm, w), row) for w in widths],
        out_shape=[jax.ShapeDtypeStruct((n, w), BF16) for w in widths],
        compiler_params=_cparams(("parallel",)),
        name="in_proj",
    )(x2d, w_ext, ln, gain, cos_t, sin_t, ones_blk)


def _half_masks():
    lane = lax.broadcasted_iota(jnp.int32, (1, LANES), 1)
    lo = lane < HEAD_DIM
    return lo, lo.astype(BF16), (~lo).astype(BF16)


def _win_attn_kernel(sink_ref, q_ref, k_ref, v_ref, bias_ref, o_ref):
    j = pl.program_id(1)
    start = pl.multiple_of(jnp.clip(j * TQ_A - A_WINDOW, 0, SEQ - KW_A), 128)
    kw = k_ref[0, pl.ds(start, KW_A), :]
    vw = v_ref[0, pl.ds(start, KW_A), :]
    lo, m_lo, m_hi = _half_masks()
    for p in range(A_HEADS // 2):
        qblk = q_ref[0, :, LANES * p:LANES * (p + 1)]
        outs = []
        for half in range(2):
            hidx = 2 * p + half
            qm = qblk * (m_lo if half == 0 else m_hi)
            s = lax.dot_general(qm, kw, (((1,), (1,)), ((), ())), preferred_element_type=F32)
            s = s + bias_ref[0, hidx]
            sk = sink_ref[hidx]
            m = jnp.maximum(jnp.max(s, axis=-1, keepdims=True), sk)
            e = jnp.exp2(s - m)
            l = jnp.sum(e, axis=-1, keepdims=True) + jnp.exp2(sk - m)
            o2 = jnp.dot(e.astype(BF16), vw, preferred_element_type=F32)
            outs.append(o2 / l)
        o_ref[0, :, LANES * p:LANES * (p + 1)] = jnp.where(lo, outs[0], outs[1]).astype(BF16)


def _win_attn(qa, ka, va, bias, sink2):
    b = qa.shape[0]
    nq = SEQ // TQ_A
    variant = lambda bi, j: (jnp.where(j == 0, 0, jnp.where(j == nq - 1, 2, 1)), 0, 0, 0)
    return pl.pallas_call(
        _win_attn_kernel,
        grid=(b, nq),
        in_specs=[
            pl.BlockSpec(memory_space=pltpu.SMEM),
            pl.BlockSpec((1, TQ_A, A_WIDTH), lambda bi, j: (bi, j, 0)),
            pl.BlockSpec((1, SEQ, KV_WIDTH), lambda bi, j: (bi, 0, 0)),
            pl.BlockSpec((1, SEQ, KV_WIDTH), lambda bi, j: (bi, 0, 0)),
            pl.BlockSpec((1, A_HEADS, TQ_A, KW_A), variant),
        ],
        out_specs=pl.BlockSpec((1, TQ_A, A_WIDTH), lambda bi, j: (bi, j, 0)),
        out_shape=jax.ShapeDtypeStruct((b, SEQ, A_WIDTH), BF16),
        compiler_params=_cparams(("parallel", "arbitrary")),
        name="win_attn",
    )(sink2, qa, ka, va, bias)


def _win_bias_table():
    slopes = np.array([2.0 ** (-8.0 * (n + 1) / A_HEADS) for n in range(A_HEADS)], np.float32)[list(PAIR_ORDER)]
    i = np.arange(TQ_A)[:, None]
    jj = np.arange(KW_A)[None, :]
    tabs = []
    for off in (0, A_WINDOW, KW_A - TQ_A):
        dist = np.abs(off + i - jj).astype(np.float32)
        tab = np.where(dist[None] <= A_WINDOW, -slopes[:, None, None] * dist[None] * LOG2E, NEG)
        tabs.append(tab)
    return jnp.asarray(np.stack(tabs).astype(np.float32))


def _nbr_attn_kernel(q_ref, k_ref, v_ref, bias_ref, o_ref):
    j = pl.program_id(1)
    rows_per_tile = TQ_B // GRID_W
    krow0 = jnp.clip(j * rows_per_tile - B_ROWS // 2, 0, GRID_ROWS - KW_B // GRID_W)
    start = pl.multiple_of(krow0 * GRID_W, 256)
    lo, m_lo, m_hi = _half_masks()
    for p in range(B_HEADS // 2):
        qblk = q_ref[0, :, LANES * p:LANES * (p + 1)]
        kw = k_ref[0, pl.ds(start, KW_B), LANES * p:LANES * (p + 1)]
        vw = v_ref[0, pl.ds(start, KW_B), LANES * p:LANES * (p + 1)]
        outs = []
        for half in range(2):
            hidx = 2 * p + half
            qm = qblk * (m_lo if half == 0 else m_hi)
            s = lax.dot_general(qm, kw, (((1,), (1,)), ((), ())), preferred_element_type=F32)
            s = s + bias_ref[0, hidx]
            m = jnp.max(s, axis=-1, keepdims=True)
            e = jnp.exp2(s - m)
            l = jnp.sum(e, axis=-1, keepdims=True)
            o2 = jnp.dot(e.astype(BF16), vw, preferred_element_type=F32)
            outs.append(o2 / l)
        o_ref[0, :, LANES * p:LANES * (p + 1)] = jnp.where(lo, outs[0], outs[1]).astype(BF16)


def _nbr_attn(qb, kb, vb, bias):
    b = qb.shape[0]
    nq = SEQ // TQ_B
    variant = lambda bi, j: (jnp.where(j == 0, 0, jnp.where(j == nq - 1, 2, 1)), 0, 0, 0)
    return pl.pallas_call(
        _nbr_attn_kernel,
        grid=(b, nq),
        in_specs=[
            pl.BlockSpec((1, TQ_B, B_WIDTH), lambda bi, j: (bi, j, 0)),
            pl.BlockSpec((1, SEQ, B_WIDTH), lambda bi, j: (bi, 0, 0)),
            pl.BlockSpec((1, SEQ, B_WIDTH), lambda bi, j: (bi, 0, 0)),
            pl.BlockSpec((1, B_HEADS, TQ_B, KW_B), variant),
        ],
        out_specs=pl.BlockSpec((1, TQ_B, B_WIDTH), lambda bi, j: (bi, j, 0)),
        out_shape=jax.ShapeDtypeStruct((b, SEQ, B_WIDTH), BF16),
        compiler_params=_cparams(("parallel", "arbitrary")),
        name="nbr_attn",
    )(qb, kb, vb, bias)


def _nbr_bias_table(rpb):
    rows_per_tile = TQ_B // GRID_W
    krows = KW_B // GRID_W
    tabs = []
    for r_first in (0, rows_per_tile, GRID_ROWS - rows_per_tile):
        krow0 = int(np.clip(r_first - B_ROWS // 2, 0, GRID_ROWS - krows))
        qi = np.arange(TQ_B)
        kj = np.arange(KW_B)
        qr = (r_first + qi // GRID_W)[:, None]
        qcol = (qi % GRID_W)[:, None]
        kr = (krow0 + kj // GRID_W)[None, :]
        kcol = (kj % GRID_W)[None, :]
        r0 = np.clip(qr - B_ROWS // 2, 0, GRID_ROWS - B_ROWS)
        c0 = np.clip(qcol - B_COLS // 2, 0, GRID_W - B_COLS)
        valid = (kr >= r0) & (kr < r0 + B_ROWS) & (kcol >= c0) & (kcol < c0 + B_COLS)
        ridx = np.clip(kr - qr + (B_ROWS - 1), 0, 2 * B_ROWS - 2)
        cidx = np.clip(kcol - qcol, -(B_COLS - 1), B_COLS - 1) + (B_COLS - 1)
        ridx = np.broadcast_to(ridx, valid.shape)
        cidx = np.broadcast_to(cidx, valid.shape)
        vals = rpb.astype(F32)[:, ridx, cidx] * LOG2E
        tabs.append(jnp.where(jnp.asarray(valid)[None], vals, NEG))
    return jnp.stack(tabs)


def _dense_attn_kernel(q_ref, k_ref, v_ref, o_ref, s_ref):
    lo, m_lo, m_hi = _half_masks()
    nchunk = SEQ // KC_C
    for p in range(C_HEADS // 2):
        qblk = q_ref[0, :, LANES * p:LANES * (p + 1)]
        q2 = jnp.concatenate([qblk * m_lo, qblk * m_hi], axis=0)
        m_run = None
        for c in range(nchunk):
            kc = k_ref[0, KC_C * c:KC_C * (c + 1), :]
            s = lax.dot_general(q2, kc, (((1,), (1,)), ((), ())), preferred_element_type=F32)
            s_ref[:, KC_C * c:KC_C * (c + 1)] = s
            for t in range(KC_C // LANES):
                blk = s[:, LANES * t:LANES * (t + 1)]
                m_run = blk if m_run is None else jnp.maximum(m_run, blk)
        m = jnp.max(m_run, axis=-1, keepdims=True)
        acc = None
        for c in range(nchunk):
            e = jnp.exp2(s_ref[:, KC_C * c:KC_C * (c + 1)] - m).astype(BF16)
            part = jnp.dot(e, v_ref[0, KC_C * c:KC_C * (c + 1), :], preferred_element_type=F32)
            acc = part if acc is None else acc + part
        on = acc[:, :LANES] / acc[:, LANES:]
        o_ref[0, :, LANES * p:LANES * (p + 1)] = jnp.where(lo, on[:TQ_C], on[TQ_C:]).astype(BF16)


def _dense_attn(qc, kc, vc):
    b = qc.shape[0]
    return pl.pallas_call(
        _dense_attn_kernel,
        grid=(b, SEQ // TQ_C),
        in_specs=[
            pl.BlockSpec((1, TQ_C, C_WIDTH), lambda bi, j: (bi, j, 0)),
            pl.BlockSpec((1, SEQ, KV_WIDTH), lambda bi, j: (bi, 0, 0)),
            pl.BlockSpec((1, SEQ, 2 * KV_WIDTH), lambda bi, j: (bi, 0, 0)),
        ],
        out_specs=pl.BlockSpec((1, TQ_C, C_WIDTH), lambda bi, j: (bi, j, 0)),
        out_shape=jax.ShapeDtypeStruct((b, SEQ, C_WIDTH), BF16),
        scratch_shapes=[pltpu.VMEM((2 * TQ_C, SEQ), F32)],
        compiler_params=_cparams(("parallel", "arbitrary")),
        name="dense_attn",
    )(qc, kc, vc)


def _out_proj_kernel(oa_ref, ob_ref, oc_ref, x_ref, ga_ref, gb_ref, gc_ref, wa_ref, wb_ref, wc_ref,
                     ln_ref, wr_ref, br_ref, xn_ref, h_ref, gates_ref):
    def nrm(o_ref, g_ref):
        o = o_ref[...].astype(F32)
        ms = jnp.mean(o * o, axis=-1, keepdims=True)
        return (o * lax.rsqrt(ms + EPS) * g_ref[...]).astype(BF16)

    acc = jnp.dot(nrm(oa_ref, ga_ref), wa_ref[...], preferred_element_type=F32)
    acc = acc + jnp.dot(nrm(ob_ref, gb_ref), wb_ref[...], preferred_element_type=F32)
    acc = acc + jnp.dot(nrm(oc_ref, gc_ref), wc_ref[...], preferred_element_type=F32)
    xn = x_ref[...] + acc
    xn_ref[...] = xn
    ms = jnp.mean(xn * xn, axis=-1, keepdims=True)
    h2 = xn * lax.rsqrt(ms + EPS) * ln_ref[...]
    hi = h2.astype(BF16)
    lo = (h2 - hi.astype(F32)).astype(BF16)
    h_ref[...] = hi
    wr = wr_ref[...]
    t = jnp.dot(hi, wr, preferred_element_type=F32)
    u = jnp.dot(lo, wr[:, :LANES], preferred_element_type=F32)
    logits = t[:, :LANES] + t[:, LANES:] + u + br_ref[...]

    lane = lax.broadcasted_iota(jnp.int32, logits.shape, 1).astype(F32)
    big = jnp.float32(3.0e38)
    is_g = lane < N_GROUPS
    gl = jnp.where(is_g, logits, -big)
    mg = jnp.max(gl, axis=-1, keepdims=True)
    grp = jnp.min(jnp.where(gl == mg, lane, big), axis=-1, keepdims=True)
    pg = 1.0 / jnp.sum(jnp.where(is_g, jnp.exp(gl - mg), 0.0), axis=-1, keepdims=True)
    e_lo = N_GROUPS + EXPERTS_PER_GROUP * grp
    sel = (lane >= e_lo) & (lane < e_lo + EXPERTS_PER_GROUP)
    el = jnp.where(sel, logits, -big)
    v1 = jnp.max(el, axis=-1, keepdims=True)
    i1 = jnp.min(jnp.where(el == v1, lane, big), axis=-1, keepdims=True)
    el2 = jnp.where(lane == i1, -big, el)
    v2 = jnp.max(el2, axis=-1, keepdims=True)
    i2 = jnp.min(jnp.where(el2 == v2, lane, big), axis=-1, keepdims=True)
    e21 = jnp.exp(v2 - v1)
    w1 = pg / (1.0 + e21)
    w2 = pg * e21 / (1.0 + e21)
    gates_ref[...] = jnp.where(lane == i1, w1, jnp.where(lane == i2, w2, 0.0))


def _out_proj(oa, ob, oc, x2d, ga, gb, gc, wa, wb, wc, ln2, wr, br):
    n = x2d.shape[0]
    tm = TM_PROJ
    row = lambda i: (i, 0)
    fixed = lambda i: (0, 0)
    return pl.pallas_call(
        _out_proj_kernel,
        grid=(n // tm,),
        in_specs=[
            pl.BlockSpec((tm, A_WIDTH), row),
            pl.BlockSpec((tm, B_WIDTH), row),
            pl.BlockSpec((tm, C_WIDTH), row),
            pl.BlockSpec((tm, D_MODEL), row),
            pl.BlockSpec((1, A_WIDTH), fixed),
            pl.BlockSpec((1, B_WIDTH), fixed),
            pl.BlockSpec((1, C_WIDTH), fixed),
            pl.BlockSpec((A_WIDTH, D_MODEL), fixed),
            pl.BlockSpec((B_WIDTH, D_MODEL), fixed),
            pl.BlockSpec((C_WIDTH, D_MODEL), fixed),
            pl.BlockSpec((1, D_MODEL), fixed),
            pl.BlockSpec((D_MODEL, 2 * LANES), fixed),
            pl.BlockSpec((1, LANES), fixed),
        ],
        out_specs=[
            pl.BlockSpec((tm, D_MODEL), row),
            pl.BlockSpec((tm, D_MODEL), row),
            pl.BlockSpec((tm, LANES), row),
        ],
        out_shape=[
            jax.ShapeDtypeStruct((n, D_MODEL), F32),
            jax.ShapeDtypeStruct((n, D_MODEL), BF16),
            jax.ShapeDtypeStruct((n, LANES), F32),
        ],
        compiler_params=_cparams(("parallel",)),
        name="out_proj_router",
    )(oa, ob, oc, x2d, ga, gb, gc, wa, wb, wc, ln2, wr, br)


def _moe_kernel(h_ref, gates_ref, xn_ref, wg_ref, wu_ref, wd_ref, o_ref):
    e = pl.program_id(1)

    @pl.when(e == 0)
    def _():
        o_ref[...] = xn_ref[...]

    h = h_ref[...]
    g = jnp.dot(h, wg_ref[0], preferred_element_type=F32)
    u = jnp.dot(h, wu_ref[0], preferred_element_type=F32)
    a = (g / (1.0 + jnp.exp(-g)) * u).astype(BF16)
    y = jnp.dot(a, wd_ref[0], preferred_element_type=F32)
    gates = gates_ref[...]
    lane = lax.broadcasted_iota(jnp.int32, gates.shape, 1)
    ge = jnp.sum(jnp.where(lane == e + N_GROUPS, gates, 0.0), axis=-1, keepdims=True)
    o_ref[...] += ge * y


def _moe(h, gates, xn, wg, wu, wd):
    n = h.shape[0]
    tm = TM_MOE
    row = lambda i, e: (i, 0)
    return pl.pallas_call(
        _moe_kernel,
        grid=(n // tm, N_EXPERTS),
        in_specs=[
            pl.BlockSpec((tm, D_MODEL), row),
            pl.BlockSpec((tm, LANES), row),
            pl.BlockSpec((tm, D_MODEL), row),
            pl.BlockSpec((1, D_MODEL, D_EXPERT), lambda i, e: (e, 0, 0)),
            pl.BlockSpec((1, D_MODEL, D_EXPERT), lambda i, e: (e, 0, 0)),
            pl.BlockSpec((1, D_EXPERT, D_MODEL), lambda i, e: (e, 0, 0)),
        ],
        out_specs=pl.BlockSpec((tm, D_MODEL), row),
        out_shape=jax.ShapeDtypeStruct((n, D_MODEL), F32),
        compiler_params=_cparams(("parallel", "arbitrary")),
        name="moe",
    )(h, gates, xn, wg, wu, wd)


def _head_cols(base, heads, swap=False):
    inner = _ROPE_SWAP if swap else np.arange(HEAD_DIM)
    return np.concatenate([base + HEAD_DIM * h + inner for h in heads])


def _proj_columns():
    o_qa, o_ka, o_va = 0, A_WIDTH, A_WIDTH + KV_WIDTH
    o_qb = o_va + KV_WIDTH
    o_kb, o_vb = o_qb + B_WIDTH, o_qb + 2 * B_WIDTH
    o_qc = o_vb + B_WIDTH
    o_kc, o_vc = o_qc + C_WIDTH, o_qc + C_WIDTH + KV_WIDTH
    nat2, nat4 = range(2), range(4)
    return np.concatenate([
        _head_cols(o_qa, PAIR_ORDER), _head_cols(o_ka, nat2),
        _head_cols(o_qb, nat4), _head_cols(o_kb, nat4),
        _head_cols(o_qc, PAIR_ORDER), _head_cols(o_kc, nat2),
        _head_cols(o_qc, PAIR_ORDER, swap=True), _head_cols(o_kc, nat2, swap=True),
        _head_cols(o_va, nat2), _head_cols(o_vb, nat4), _head_cols(o_vc, nat2),
    ])


def _rope_tables():
    nf = HEAD_DIM // 4
    inv = ROPE_THETA ** (-jnp.arange(nf, dtype=F32) / nf)
    pos = jnp.arange(SEQ)
    ang_r = (pos // GRID_W).astype(F32)[:, None] * inv[None, :]
    ang_c = (pos % GRID_W).astype(F32)[:, None] * inv[None, :]
    cos = jnp.concatenate([jnp.cos(ang_r)] * 2 + [jnp.cos(ang_c)] * 2, axis=-1)
    sin = jnp.concatenate([-jnp.sin(ang_r), jnp.sin(ang_r), -jnp.sin(ang_c), jnp.sin(ang_c)], axis=-1)
    return jnp.concatenate([cos, cos], axis=-1), jnp.concatenate([sin, sin], axis=-1)


def _layer_params(l, w_in, ln1, qk_gain, sink, rpb, out_gain, w_out, ln2, w_rg, b_rg, w_re, b_re):
    cols = _proj_columns()
    w_ext = w_in[l][:, cols].astype(BF16)
    qs = HEAD_DIM ** -0.5 * LOG2E
    g = qk_gain[l].astype(F32)
    sw = _ROPE_SWAP
    gain = jnp.concatenate([
        jnp.tile(g[0, 0], A_HEADS) * qs, jnp.tile(g[0, 1], A_KV),
        jnp.tile(g[1, 0], B_HEADS) * qs, jnp.tile(g[1, 1], B_HEADS),
        jnp.tile(g[2, 0], C_HEADS) * qs, jnp.tile(g[2, 1], C_KV),
        jnp.tile(g[2, 0][sw], C_HEADS) * qs, jnp.tile(g[2, 1][sw], C_KV),
    ])[None, :]
    order = list(PAIR_ORDER)
    sink2 = sink[l].astype(F32)[jnp.asarray(order)] * LOG2E
    rows_a = _head_cols(0, PAIR_ORDER)
    rows_b = A_WIDTH + np.arange(B_WIDTH)
    rows_c = _head_cols(A_WIDTH + B_WIDTH, PAIR_ORDER)
    og = out_gain[l].astype(F32)
    wo = w_out[l]
    wr = jnp.concatenate([w_rg[l], w_re[l], jnp.zeros((D_MODEL, LANES - N_GROUPS - N_EXPERTS), F32)], axis=1)
    wr_hi = wr.astype(BF16)
    wr_lo = (wr - wr_hi.astype(F32)).astype(BF16)
    br = jnp.concatenate([b_rg[l].astype(F32), b_re[l].astype(F32),
                          jnp.zeros((LANES - N_GROUPS - N_EXPERTS,), F32)])[None, :]
    return dict(
        w_ext=w_ext, ln1=ln1[l][None, :], gain=gain, sink2=sink2, nbr_bias=_nbr_bias_table(rpb[l]),
        ga=og[rows_a][None, :], gb=og[rows_b][None, :], gc=og[rows_c][None, :],
        wa=wo[rows_a].astype(BF16), wb=wo[rows_b].astype(BF16), wc=wo[rows_c].astype(BF16),
        ln2=ln2[l][None, :], wr=jnp.concatenate([wr_hi, wr_lo], axis=1), br=br,
    )


def _block_ones():
    idx = np.arange(MXU_DIM) // HEAD_DIM
    return jnp.asarray((idx[:, None] == idx[None, :]).astype(np.float32)).astype(BF16)


def _trunk(x, params, shared, moe_w):
    b = x.shape[0]
    x2d = x.reshape(b * SEQ, D_MODEL)
    for l, lp in enumerate(params):
        qa, ka, va, qb, kb, vb, qc, kc, vc = _in_proj(
            x2d, lp["w_ext"], lp["ln1"], lp["gain"], shared["cos"], shared["sin"], shared["ones"])
        seq = lambda z: z.reshape(b, SEQ, z.shape[-1])
        oa = _win_attn(seq(qa), seq(ka), seq(va), shared["win_bias"], lp["sink2"])
        ob = _nbr_attn(seq(qb), seq(kb), seq(vb), lp["nbr_bias"])
        oc = _dense_attn(seq(qc), seq(kc), seq(vc))
        flat = lambda z: z.reshape(b * SEQ, z.shape[-1])
        xn, h2, gates = _out_proj(flat(oa), flat(ob), flat(oc), x2d, lp["ga"], lp["gb"], lp["gc"],
                                  lp["wa"], lp["wb"], lp["wc"], lp["ln2"], lp["wr"], lp["br"])
        wg, wu, wd = moe_w[l]
        x2d = _moe(h2, gates, xn, wg, wu, wd)
    return x2d.reshape(b, SEQ, D_MODEL)


def kernel(x_prompt, x_sample, ln1, w_in, qk_gain, sink, rpb, out_gain, w_out, ln2, w_router_group,
           b_router_group, w_router_expert, b_router_expert, w_gate, w_up, w_down):
    depth = w_in.shape[0]
    params = [_layer_params(l, w_in, ln1, qk_gain, sink, rpb, out_gain, w_out, ln2, w_router_group,
                            b_router_group, w_router_expert, b_router_expert) for l in range(depth)]
    cos_t, sin_t = _rope_tables()
    shared = dict(cos=cos_t, sin=sin_t, ones=_block_ones(), win_bias=_win_bias_table())
    moe_w = [(w_gate[l].astype(BF16), w_up[l].astype(BF16), w_down[l].astype(BF16)) for l in range(depth)]
    y_prompt = _trunk(x_prompt, params, shared, moe_w)
    y_sample = _trunk(x_sample, params, shared, moe_w)
    return (y_prompt, y_sample)
```

```python
import functools
import math

import numpy as np
import jax
import jax.numpy as jnp
from jax import lax
from jax.experimental import pallas as pl
from jax.experimental.pallas import tpu as pltpu

D_MODEL = 1024
SEQ = 4096
HEAD_DIM = 64
GRID_W = 64
GRID_ROWS = SEQ // GRID_W
A_HEADS, A_KV, A_WINDOW = 6, 2, 128
B_HEADS, B_ROWS, B_COLS = 4, 8, 16
C_HEADS, C_KV = 6, 2
ROPE_THETA = 10000.0
N_GROUPS, EXPERTS_PER_GROUP = 4, 4
N_EXPERTS = N_GROUPS * EXPERTS_PER_GROUP
D_EXPERT = D_MODEL // 2
EPS = 1e-6
NEG = -1e30
LOG2E = math.log2(math.e)

A_WIDTH = A_HEADS * HEAD_DIM
B_WIDTH = B_HEADS * HEAD_DIM
C_WIDTH = C_HEADS * HEAD_DIM
KV_WIDTH = A_KV * HEAD_DIM

LANES = 128
MXU_DIM = 256
VMEM_LIMIT = 56 * 1024 * 1024

PAIR_ORDER = (0, 3, 1, 4, 2, 5)

TM_PROJ = 512
TQ_A = 256
KW_A = 512
TQ_B = 256
KW_B = 768
TQ_C = 256
KC_C = 512
TM_MOE = 256
TM_COMB = 512

PAIR_A = (0, 0, 0, 1, 1, 2)
PAIR_B = (1, 2, 3, 2, 3, 3)
PAIRS_PER_GROUP = len(PAIR_A)
N_CLASSES = N_GROUPS * PAIRS_PER_GROUP
CLASS_LANE = N_GROUPS + N_EXPERTS
HROW = D_MODEL // 2 + LANES

BF16 = jnp.bfloat16
F32 = jnp.float32

_ROPE_SWAP = np.concatenate([np.arange(16, 32), np.arange(0, 16), np.arange(48, 64), np.arange(32, 48)])


def _cparams(sem):
    return pltpu.CompilerParams(dimension_semantics=sem, vmem_limit_bytes=VMEM_LIMIT)


N_NORMED = 1536
N_GAINED = 2048
N_PROJ = 2560


def _in_proj_kernel(x_ref, w_ref, ln_ref, gain_ref, cos_ref, sin_ref, ones_ref,
                    qa_ref, ka_ref, va_ref, qb_ref, kb_ref, vb_ref, qc_ref, kc_ref, vc_ref):
    x = x_ref[...]
    ms = jnp.mean(x * x, axis=-1, keepdims=True)
    h = (x * lax.rsqrt(ms + EPS) * ln_ref[...]).astype(BF16)
    p = jnp.dot(h, w_ref[...], preferred_element_type=F32)
    g = gain_ref[...]
    ones_blk = ones_ref[...]

    def chunk(c):
        return p[:, MXU_DIM * c:MXU_DIM * (c + 1)]

    def inv_rms(pc):
        ss = jnp.dot((pc * pc).astype(BF16), ones_blk, preferred_element_type=F32)
        return lax.rsqrt(ss * (1.0 / HEAD_DIM) + EPS)

    r = [inv_rms(chunk(c)) for c in range(N_NORMED // MXU_DIM)]
    pn = [chunk(c) * r[c] * g[:, MXU_DIM * c:MXU_DIM * (c + 1)] for c in range(N_NORMED // MXU_DIM)]
    qa_ref[:, 0:256] = pn[0].astype(BF16)
    qa_ref[:, 256:384] = pn[1][:, :128].astype(BF16)
    ka_ref[...] = pn[1][:, 128:].astype(BF16)
    qb_ref[...] = pn[2].astype(BF16)
    kb_ref[...] = pn[3].astype(BF16)
    cos = cos_ref[...]
    sin = sin_ref[...]
    cos2 = jnp.concatenate([cos, cos], axis=1)
    sin2 = jnp.concatenate([sin, sin], axis=1)
    sw4 = chunk(6) * r[4] * g[:, 1536:1792]
    sw5 = chunk(7) * r[5] * g[:, 1792:2048]
    c4 = pn[4] * cos2 + sw4 * sin2
    c5 = pn[5] * cos2 + sw5 * sin2
    qc_ref[:, 0:256] = c4.astype(BF16)
    qc_ref[:, 256:384] = c5[:, :128].astype(BF16)
    kc_ref[...] = c5[:, 128:].astype(BF16)
    va_ref[...] = p[:, 2048:2176].astype(BF16)
    vb_ref[...] = p[:, 2176:2432].astype(BF16)
    vc_ref[:, 0:128] = p[:, 2432:2560].astype(BF16)
    vc_ref[:, 128:256] = jnp.ones((x.shape[0], 128), BF16)


def _in_proj(x2d, w_ext, ln, gain, cos_t, sin_t, ones_blk):
    n = x2d.shape[0]
    tm = TM_PROJ
    tiles_per_seq = SEQ // tm
    row = lambda i: (i, 0)
    fixed = lambda i: (0, 0)
    pos = lambda i: (i % tiles_per_seq, 0)
    widths = (A_WIDTH, KV_WIDTH, KV_WIDTH, B_WIDTH, B_WIDTH, B_WIDTH, C_WIDTH, KV_WIDTH, 2 * KV_WIDTH)
    return pl.pallas_call(
        _in_proj_kernel,
        grid=(n // tm,),
        in_specs=[
            pl.BlockSpec((tm, D_MODEL), row),
            pl.BlockSpec((D_MODEL, N_PROJ), fixed),
            pl.BlockSpec((1, D_MODEL), fixed),
            pl.BlockSpec((1, N_GAINED), fixed),
            pl.BlockSpec((tm, LANES), pos),
            pl.BlockSpec((tm, LANES), pos),
            pl.BlockSpec((MXU_DIM, MXU_DIM), fixed),
        ],
        out_specs=[pl.BlockSpec((tm, w), row) for w in widths],
        out_shape=[jax.ShapeDtypeStruct((n, w), BF16) for w in widths],
        compiler_params=_cparams(("parallel",)),
        name="in_proj",
    )(x2d, w_ext, ln, gain, cos_t, sin_t, ones_blk)


def _half_masks():
    lane = lax.broadcasted_iota(jnp.int32, (1, LANES), 1)
    lo = lane < HEAD_DIM
    return lo, lo.astype(BF16), (~lo).astype(BF16)


def _win_attn_kernel(sink_ref, q_ref, k_ref, v_ref, bias_ref, o_ref):
    j = pl.program_id(1)
    start = pl.multiple_of(jnp.clip(j * TQ_A - A_WINDOW, 0, SEQ - KW_A), 128)
    kw = k_ref[0, pl.ds(start, KW_A), :]
    vw = v_ref[0, pl.ds(start, KW_A), :]
    lo, m_lo, m_hi = _half_masks()
    for p in range(A_HEADS // 2):
        qblk = q_ref[0, :, LANES * p:LANES * (p + 1)]
        outs = []
        for half in range(2):
            hidx = 2 * p + half
            qm = qblk * (m_lo if half == 0 else m_hi)
            s = lax.dot_general(qm, kw, (((1,), (1,)), ((), ())), preferred_element_type=F32)
            s = s + bias_ref[0, hidx]
            sk = sink_ref[hidx]
            m = jnp.maximum(jnp.max(s, axis=-1, keepdims=True), sk)
            e = jnp.exp2(s - m)
            l = jnp.sum(e, axis=-1, keepdims=True) + jnp.exp2(sk - m)
            o2 = jnp.dot(e.astype(BF16), vw, preferred_element_type=F32)
            outs.append(o2 / l)
        o_ref[0, :, LANES * p:LANES * (p + 1)] = jnp.where(lo, outs[0], outs[1]).astype(BF16)


def _win_attn(qa, ka, va, bias, sink2):
    b = qa.shape[0]
    nq = SEQ // TQ_A
    variant = lambda bi, j: (jnp.where(j == 0, 0, jnp.where(j == nq - 1, 2, 1)), 0, 0, 0)
    return pl.pallas_call(
        _win_attn_kernel,
        grid=(b, nq),
        in_specs=[
            pl.BlockSpec(memory_space=pltpu.SMEM),
            pl.BlockSpec((1, TQ_A, A_WIDTH), lambda bi, j: (bi, j, 0)),
            pl.BlockSpec((1, SEQ, KV_WIDTH), lambda bi, j: (bi, 0, 0)),
            pl.BlockSpec((1, SEQ, KV_WIDTH), lambda bi, j: (bi, 0, 0)),
            pl.BlockSpec((1, A_HEADS, TQ_A, KW_A), variant),
        ],
        out_specs=pl.BlockSpec((1, TQ_A, A_WIDTH), lambda bi, j: (bi, j, 0)),
        out_shape=jax.ShapeDtypeStruct((b, SEQ, A_WIDTH), BF16),
        compiler_params=_cparams(("parallel", "arbitrary")),
        name="win_attn",
    )(sink2, qa, ka, va, bias)


def _win_bias_table():
    slopes = np.array([2.0 ** (-8.0 * (n + 1) / A_HEADS) for n in range(A_HEADS)], np.float32)[list(PAIR_ORDER)]
    i = np.arange(TQ_A)[:, None]
    jj = np.arange(KW_A)[None, :]
    tabs = []
    for off in (0, A_WINDOW, KW_A - TQ_A):
        dist = np.abs(off + i - jj).astype(np.float32)
        tab = np.where(dist[None] <= A_WINDOW, -slopes[:, None, None] * dist[None] * LOG2E, NEG)
        tabs.append(tab)
    return jnp.asarray(np.stack(tabs).astype(np.float32))


def _nbr_attn_kernel(q_ref, k_ref, v_ref, bias_ref, o_ref):
    j = pl.program_id(1)
    rows_per_tile = TQ_B // GRID_W
    krow0 = jnp.clip(j * rows_per_tile - B_ROWS // 2, 0, GRID_ROWS - KW_B // GRID_W)
    start = pl.multiple_of(krow0 * GRID_W, 256)
    lo, m_lo, m_hi = _half_masks()
    for p in range(B_HEADS // 2):
        qblk = q_ref[0, :, LANES * p:LANES * (p + 1)]
        kw = k_ref[0, pl.ds(start, KW_B), LANES * p:LANES * (p + 1)]
        vw = v_ref[0, pl.ds(start, KW_B), LANES * p:LANES * (p + 1)]
        outs = []
        for half in range(2):
            hidx = 2 * p + half
            qm = qblk * (m_lo if half == 0 else m_hi)
            s = lax.dot_general(qm, kw, (((1,), (1,)), ((), ())), preferred_element_type=F32)
            s = s + bias_ref[0, hidx]
            m = jnp.max(s, axis=-1, keepdims=True)
            e = jnp.exp2(s - m)
            l = jnp.sum(e, axis=-1, keepdims=True)
            o2 = jnp.dot(e.astype(BF16), vw, preferred_element_type=F32)
            outs.append(o2 / l)
        o_ref[0, :, LANES * p:LANES * (p + 1)] = jnp.where(lo, outs[0], outs[1]).astype(BF16)


def _nbr_attn(qb, kb, vb, bias):
    b = qb.shape[0]
    nq = SEQ // TQ_B
    variant = lambda bi, j: (jnp.where(j == 0, 0, jnp.where(j == nq - 1, 2, 1)), 0, 0, 0)
    return pl.pallas_call(
        _nbr_attn_kernel,
        grid=(b, nq),
        in_specs=[
            pl.BlockSpec((1, TQ_B, B_WIDTH), lambda bi, j: (bi, j, 0)),
            pl.BlockSpec((1, SEQ, B_WIDTH), lambda bi, j: (bi, 0, 0)),
            pl.BlockSpec((1, SEQ, B_WIDTH), lambda bi, j: (bi, 0, 0)),
            pl.BlockSpec((1, B_HEADS, TQ_B, KW_B), variant),
        ],
        out_specs=pl.BlockSpec((1, TQ_B, B_WIDTH), lambda bi, j: (bi, j, 0)),
        out_shape=jax.ShapeDtypeStruct((b, SEQ, B_WIDTH), BF16),
        compiler_params=_cparams(("parallel", "arbitrary")),
        name="nbr_attn",
    )(qb, kb, vb, bias)


def _nbr_bias_table(rpb):
    rows_per_tile = TQ_B // GRID_W
    krows = KW_B // GRID_W
    r = rpb.astype(F32) * LOG2E
    edge = GRID_W - B_COLS
    ext = jnp.concatenate([jnp.repeat(r[..., :1], edge, axis=-1), r, jnp.repeat(r[..., -1:], edge, axis=-1)], axis=-1)
    col = jnp.stack([ext[..., GRID_W - 1 - q:2 * GRID_W - 1 - q] for q in range(GRID_W)], axis=2)
    tabs = []
    for r_first in (0, rows_per_tile, GRID_ROWS - rows_per_tile):
        krow0 = int(np.clip(r_first - B_ROWS // 2, 0, GRID_ROWS - krows))
        slabs = []
        for ql in range(rows_per_tile):
            per_k = [col[:, int(np.clip(krow0 + kl - (r_first + ql) + B_ROWS - 1, 0, 2 * B_ROWS - 2))]
                     for kl in range(krows)]
            slabs.append(jnp.stack(per_k, axis=2))
        tab = jnp.stack(slabs, axis=1).reshape(B_HEADS, TQ_B, KW_B)
        qi = np.arange(TQ_B)
        kj = np.arange(KW_B)
        qr = (r_first + qi // GRID_W)[:, None]
        qcol = (qi % GRID_W)[:, None]
        kr = (krow0 + kj // GRID_W)[None, :]
        kcol = (kj % GRID_W)[None, :]
        r0 = np.clip(qr - B_ROWS // 2, 0, GRID_ROWS - B_ROWS)
        c0 = np.clip(qcol - B_COLS // 2, 0, GRID_W - B_COLS)
        valid = (kr >= r0) & (kr < r0 + B_ROWS) & (kcol >= c0) & (kcol < c0 + B_COLS)
        tabs.append(jnp.where(jnp.asarray(valid)[None], tab, NEG))
    return jnp.stack(tabs)


def _dense_attn_kernel(q_ref, k_ref, v_ref, o_ref, s_ref):
    lo, m_lo, m_hi = _half_masks()
    nchunk = SEQ // KC_C
    for p in range(C_HEADS // 2):
        qblk = q_ref[0, :, LANES * p:LANES * (p + 1)]
        q2 = jnp.concatenate([qblk * m_lo, qblk * m_hi], axis=0)
        m_run = None
        for c in range(nchunk):
            kc = k_ref[0, KC_C * c:KC_C * (c + 1), :]
            s = lax.dot_general(q2, kc, (((1,), (1,)), ((), ())), preferred_element_type=F32)
            s_ref[:, KC_C * c:KC_C * (c + 1)] = s
            for t in range(KC_C // LANES):
                blk = s[:, LANES * t:LANES * (t + 1)]
                m_run = blk if m_run is None else jnp.maximum(m_run, blk)
        m = jnp.max(m_run, axis=-1, keepdims=True)
        acc = None
        for c in range(nchunk):
            e = jnp.exp2(s_ref[:, KC_C * c:KC_C * (c + 1)] - m).astype(BF16)
            part = jnp.dot(e, v_ref[0, KC_C * c:KC_C * (c + 1), :], preferred_element_type=F32)
            acc = part if acc is None else acc + part
        on = acc[:, :LANES] / acc[:, LANES:]
        o_ref[0, :, LANES * p:LANES * (p + 1)] = jnp.where(lo, on[:TQ_C], on[TQ_C:]).astype(BF16)


def _dense_attn(qc, kc, vc):
    b = qc.shape[0]
    return pl.pallas_call(
        _dense_attn_kernel,
        grid=(b, SEQ // TQ_C),
        in_specs=[
            pl.BlockSpec((1, TQ_C, C_WIDTH), lambda bi, j: (bi, j, 0)),
            pl.BlockSpec((1, SEQ, KV_WIDTH), lambda bi, j: (bi, 0, 0)),
            pl.BlockSpec((1, SEQ, 2 * KV_WIDTH), lambda bi, j: (bi, 0, 0)),
        ],
        out_specs=pl.BlockSpec((1, TQ_C, C_WIDTH), lambda bi, j: (bi, j, 0)),
        out_shape=jax.ShapeDtypeStruct((b, SEQ, C_WIDTH), BF16),
        scratch_shapes=[pltpu.VMEM((2 * TQ_C, SEQ), F32)],
        compiler_params=_cparams(("parallel", "arbitrary")),
        name="dense_attn",
    )(qc, kc, vc)


def _out_proj_kernel(oa_ref, ob_ref, oc_ref, x_ref, ga_ref, gb_ref, gc_ref, wa_ref, wb_ref, wc_ref,
                     ln_ref, wr_ref, br_ref, xn_ref, h_ref):
    def nrm(o_ref, g_ref):
        o = o_ref[...].astype(F32)
        ms = jnp.mean(o * o, axis=-1, keepdims=True)
        return (o * lax.rsqrt(ms + EPS) * g_ref[...]).astype(BF16)

    acc = jnp.dot(nrm(oa_ref, ga_ref), wa_ref[...], preferred_element_type=F32)
    acc = acc + jnp.dot(nrm(ob_ref, gb_ref), wb_ref[...], preferred_element_type=F32)
    acc = acc + jnp.dot(nrm(oc_ref, gc_ref), wc_ref[...], preferred_element_type=F32)
    xn = x_ref[...] + acc
    xn_ref[...] = xn
    ms = jnp.mean(xn * xn, axis=-1, keepdims=True)
    h2 = xn * lax.rsqrt(ms + EPS) * ln_ref[...]
    hi = h2.astype(BF16)
    lo = (h2 - hi.astype(F32)).astype(BF16)
    half = D_MODEL // 2
    bits_lo = lax.bitcast_convert_type(hi[:, :half].astype(F32), jnp.uint32)
    bits_hi = lax.bitcast_convert_type(hi[:, half:].astype(F32), jnp.uint32)
    h_ref[:, :half] = (bits_hi & jnp.uint32(0xFFFF0000)) | (bits_lo >> 16)
    wr = wr_ref[...]
    t = jnp.dot(hi, wr, preferred_element_type=F32)
    u = jnp.dot(lo, wr[:, :LANES], preferred_element_type=F32)
    logits = t[:, :LANES] + t[:, LANES:] + u + br_ref[...]

    lane = lax.broadcasted_iota(jnp.int32, logits.shape, 1).astype(F32)
    big = jnp.float32(3.0e38)
    is_g = lane < N_GROUPS
    gl = jnp.where(is_g, logits, -big)
    mg = jnp.max(gl, axis=-1, keepdims=True)
    grp = jnp.min(jnp.where(gl == mg, lane, big), axis=-1, keepdims=True)
    pg = 1.0 / jnp.sum(jnp.where(is_g, jnp.exp(gl - mg), 0.0), axis=-1, keepdims=True)
    e_lo = N_GROUPS + EXPERTS_PER_GROUP * grp
    sel = (lane >= e_lo) & (lane < e_lo + EXPERTS_PER_GROUP)
    el = jnp.where(sel, logits, -big)
    v1 = jnp.max(el, axis=-1, keepdims=True)
    i1 = jnp.min(jnp.where(el == v1, lane, big), axis=-1, keepdims=True)
    el2 = jnp.where(lane == i1, -big, el)
    v2 = jnp.max(el2, axis=-1, keepdims=True)
    i2 = jnp.min(jnp.where(el2 == v2, lane, big), axis=-1, keepdims=True)
    e21 = jnp.exp(v2 - v1)
    w1 = pg / (1.0 + e21)
    w2 = pg * e21 / (1.0 + e21)
    a = jnp.minimum(i1, i2) - e_lo
    b = jnp.maximum(i1, i2) - e_lo
    cls = grp * PAIRS_PER_GROUP + (a * (7.0 - a) * 0.5 + (b - a - 1.0))
    gates = jnp.where(lane == i1, w1, jnp.where(lane == i2, w2, jnp.where(lane == CLASS_LANE, cls, 0.0)))
    h_ref[:, half:] = lax.bitcast_convert_type(gates, jnp.uint32)


def _out_proj(oa, ob, oc, x2d, ga, gb, gc, wa, wb, wc, ln2, wr, br):
    n = x2d.shape[0]
    tm = TM_PROJ
    row = lambda i: (i, 0)
    fixed = lambda i: (0, 0)
    return pl.pallas_call(
        _out_proj_kernel,
        grid=(n // tm,),
        in_specs=[
            pl.BlockSpec((tm, A_WIDTH), row),
            pl.BlockSpec((tm, B_WIDTH), row),
            pl.BlockSpec((tm, C_WIDTH), row),
            pl.BlockSpec((tm, D_MODEL), row),
            pl.BlockSpec((1, A_WIDTH), fixed),
            pl.BlockSpec((1, B_WIDTH), fixed),
            pl.BlockSpec((1, C_WIDTH), fixed),
            pl.BlockSpec((A_WIDTH, D_MODEL), fixed),
            pl.BlockSpec((B_WIDTH, D_MODEL), fixed),
            pl.BlockSpec((C_WIDTH, D_MODEL), fixed),
            pl.BlockSpec((1, D_MODEL), fixed),
            pl.BlockSpec((D_MODEL, 2 * LANES), fixed),
            pl.BlockSpec((1, LANES), fixed),
        ],
        out_specs=[
            pl.BlockSpec((tm, D_MODEL), row),
            pl.BlockSpec((tm, HROW), row),
        ],
        out_shape=[
            jax.ShapeDtypeStruct((n, D_MODEL), F32),
            jax.ShapeDtypeStruct((n, HROW), jnp.uint32),
        ],
        compiler_params=_cparams(("parallel",)),
        name="out_proj_router",
    )(oa, ob, oc, x2d, ga, gb, gc, wa, wb, wc, ln2, wr, br)


def _row_copy(src_hbm, buf, sem, slot, src_row, dst_row):
    return pltpu.make_async_copy(src_hbm.at[pl.ds(src_row, 1), :], buf.at[slot, pl.ds(dst_row, 1), :], sem.at[slot])


def _issue_row_gather(idx_ref, src_hbm, buf, sem, slot, rows):
    for r in range(rows):
        _row_copy(src_hbm, buf, sem, slot, idx_ref[0, 0, r], r).start()


def _wait_row_gather(src_hbm, buf, sem, slot, rows):
    pltpu.make_async_copy(src_hbm.at[pl.ds(0, rows), :], buf.at[slot], sem.at[slot]).wait()


def _moe_kernel(ea_ref, eb_ref, nvalid_ref, idx0_ref, idxn_ref, h_hbm,
                wga_ref, wua_ref, wda_ref, wgb_ref, wub_ref, wdb_ref, y_ref, buf, sem):
    i = pl.program_id(0)
    nvalid = nvalid_ref[0]
    slot = i % 2

    @pl.when(i == 0)
    def _():
        _issue_row_gather(idx0_ref, h_hbm, buf, sem, 0, TM_MOE)

    @pl.when(i < nvalid)
    def _():
        _issue_row_gather(idxn_ref, h_hbm, buf, sem, 1 - slot, TM_MOE)
        _wait_row_gather(h_hbm, buf, sem, slot, TM_MOE)
        rows = buf[slot]
        half = D_MODEL // 2
        feat = rows[:, :half]
        x_lo = lax.bitcast_convert_type(feat << 16, F32).astype(BF16)
        x_hi = lax.bitcast_convert_type(feat & jnp.uint32(0xFFFF0000), F32).astype(BF16)
        x = jnp.concatenate([x_lo, x_hi], axis=1)
        gates = lax.bitcast_convert_type(rows[:, half:], F32)
        lane = lax.broadcasted_iota(jnp.int32, gates.shape, 1)

        def expert(wg_ref, wu_ref, wd_ref, e):
            g = jnp.dot(x, wg_ref[0], preferred_element_type=F32)
            u = jnp.dot(x, wu_ref[0], preferred_element_type=F32)
            act = (g / (1.0 + jnp.exp(-g)) * u).astype(BF16)
            y = jnp.dot(act, wd_ref[0], preferred_element_type=F32)
            ge = jnp.sum(jnp.where(lane == e + N_GROUPS, gates, 0.0), axis=-1, keepdims=True)
            return ge * y

        y_ref[...] = expert(wga_ref, wua_ref, wda_ref, ea_ref[i]) + expert(wgb_ref, wub_ref, wdb_ref, eb_ref[i])

        @pl.when(i == nvalid - 1)
        def _():
            _wait_row_gather(h_hbm, buf, sem, 1 - slot, TM_MOE)

    @pl.when(i >= nvalid)
    def _():
        y_ref[...] = jnp.zeros(y_ref.shape, F32)


def _moe_experts(hrow, ea, eb, nvalid, src_idx, wg, wu, wd):
    tm = TM_MOE
    nt = ea.shape[0]
    smem_blk = lambda f: pl.BlockSpec((1, 1, tm), f, memory_space=pltpu.SMEM)
    w_in_spec = lambda sel: pl.BlockSpec((1, D_MODEL, D_EXPERT), sel)
    w_out_spec = lambda sel: pl.BlockSpec((1, D_EXPERT, D_MODEL), sel)
    sel_a = lambda i, ea, eb, nv: (ea[i], 0, 0)
    sel_b = lambda i, ea, eb, nv: (eb[i], 0, 0)
    grid_spec = pltpu.PrefetchScalarGridSpec(
        num_scalar_prefetch=3,
        grid=(nt,),
        in_specs=[
            smem_blk(lambda i, ea, eb, nv: (0, 0, 0)),
            smem_blk(lambda i, ea, eb, nv: (i + 1, 0, 0)),
            pl.BlockSpec(memory_space=pl.ANY),
            w_in_spec(sel_a), w_in_spec(sel_a), w_out_spec(sel_a),
            w_in_spec(sel_b), w_in_spec(sel_b), w_out_spec(sel_b),
        ],
        out_specs=pl.BlockSpec((tm, D_MODEL), lambda i, ea, eb, nv: (i, 0)),
        scratch_shapes=[pltpu.VMEM((2, tm, HROW), jnp.uint32), pltpu.SemaphoreType.DMA((2,))],
    )
    return pl.pallas_call(
        _moe_kernel,
        grid_spec=grid_spec,
        out_shape=jax.ShapeDtypeStruct((nt * tm, D_MODEL), F32),
        compiler_params=_cparams(("arbitrary",)),
        name="moe_experts",
    )(ea, eb, nvalid, src_idx, src_idx, hrow, wg, wu, wd, wg, wu, wd)


def _combine_kernel(pos0_ref, posn_ref, xn_ref, y_hbm, o_ref, buf, sem):
    i = pl.program_id(0)
    slot = i % 2

    @pl.when(i == 0)
    def _():
        _issue_row_gather(pos0_ref, y_hbm, buf, sem, 0, TM_COMB)

    _issue_row_gather(posn_ref, y_hbm, buf, sem, 1 - slot, TM_COMB)
    _wait_row_gather(y_hbm, buf, sem, slot, TM_COMB)
    o_ref[...] = xn_ref[...] + buf[slot]

    @pl.when(i == pl.num_programs(0) - 1)
    def _():
        _wait_row_gather(y_hbm, buf, sem, 1 - slot, TM_COMB)


def _combine(pos_tiles, xn, y_sorted):
    n = xn.shape[0]
    tm = TM_COMB
    smem_blk = lambda f: pl.BlockSpec((1, 1, tm), f, memory_space=pltpu.SMEM)
    return pl.pallas_call(
        _combine_kernel,
        grid=(n // tm,),
        in_specs=[
            smem_blk(lambda i: (0, 0, 0)),
            smem_blk(lambda i: (i + 1, 0, 0)),
            pl.BlockSpec((tm, D_MODEL), lambda i: (i, 0)),
            pl.BlockSpec(memory_space=pl.ANY),
        ],
        out_specs=pl.BlockSpec((tm, D_MODEL), lambda i: (i, 0)),
        out_shape=jax.ShapeDtypeStruct((n, D_MODEL), F32),
        scratch_shapes=[pltpu.VMEM((2, tm, D_MODEL), F32), pltpu.SemaphoreType.DMA((2,))],
        compiler_params=_cparams(("arbitrary",)),
        name="moe_combine",
    )(pos_tiles, pos_tiles, xn, y_sorted)


def _moe_plan(cls, n):
    tm = TM_MOE
    nt = n // tm + N_CLASSES
    onehot = (cls[:, None] == jnp.arange(N_CLASSES, dtype=jnp.int32)[None, :]).astype(jnp.int32)
    counts = jnp.sum(onehot, axis=0)
    rank = jnp.sum(jnp.cumsum(onehot, axis=0) * onehot, axis=1) - 1
    tiles = (counts + tm - 1) // tm
    tile_end = jnp.cumsum(tiles)
    tile_start = tile_end - tiles
    pos = jnp.sum(onehot * tile_start[None, :], axis=1) * tm + rank
    src_idx = jnp.zeros(((nt + 1) * tm,), jnp.int32).at[pos].set(jnp.arange(n, dtype=jnp.int32))
    nvalid = tile_end[-1]
    tile_id = jnp.minimum(jnp.arange(nt, dtype=jnp.int32), nvalid - 1)
    tcls = jnp.sum((tile_id[:, None] >= tile_end[None, :]).astype(jnp.int32), axis=1)
    grp, pair = tcls // PAIRS_PER_GROUP, tcls % PAIRS_PER_GROUP
    pair_a = jnp.asarray(PAIR_A, jnp.int32)
    pair_b = jnp.asarray(PAIR_B, jnp.int32)
    ea = grp * EXPERTS_PER_GROUP + jnp.sum((pair[:, None] == jnp.arange(PAIRS_PER_GROUP)[None]) * pair_a[None], axis=1)
    eb = grp * EXPERTS_PER_GROUP + jnp.sum((pair[:, None] == jnp.arange(PAIRS_PER_GROUP)[None]) * pair_b[None], axis=1)
    pos_tiles = jnp.concatenate([pos, jnp.zeros((TM_COMB,), jnp.int32)]).reshape(n // TM_COMB + 1, 1, TM_COMB)
    return (ea.astype(jnp.int32), eb.astype(jnp.int32), nvalid.reshape(1).astype(jnp.int32),
            src_idx.reshape(nt + 1, 1, tm), pos_tiles)


def _moe(hrow, xn, wg, wu, wd):
    n = xn.shape[0]
    cls = lax.bitcast_convert_type(hrow[:, D_MODEL // 2 + CLASS_LANE], F32).astype(jnp.int32)
    ea, eb, nvalid, src_idx, pos_tiles = _moe_plan(cls, n)
    y_sorted = _moe_experts(hrow, ea, eb, nvalid, src_idx, wg, wu, wd)
    return _combine(pos_tiles, xn, y_sorted)


def _head_cols(base, heads, swap=False):
    inner = _ROPE_SWAP if swap else np.arange(HEAD_DIM)
    return np.concatenate([base + HEAD_DIM * h + inner for h in heads])


def _proj_columns():
    o_qa, o_ka, o_va = 0, A_WIDTH, A_WIDTH + KV_WIDTH
    o_qb = o_va + KV_WIDTH
    o_kb, o_vb = o_qb + B_WIDTH, o_qb + 2 * B_WIDTH
    o_qc = o_vb + B_WIDTH
    o_kc, o_vc = o_qc + C_WIDTH, o_qc + C_WIDTH + KV_WIDTH
    nat2, nat4 = range(2), range(4)
    return np.concatenate([
        _head_cols(o_qa, PAIR_ORDER), _head_cols(o_ka, nat2),
        _head_cols(o_qb, nat4), _head_cols(o_kb, nat4),
        _head_cols(o_qc, PAIR_ORDER), _head_cols(o_kc, nat2),
        _head_cols(o_qc, PAIR_ORDER, swap=True), _head_cols(o_kc, nat2, swap=True),
        _head_cols(o_va, nat2), _head_cols(o_vb, nat4), _head_cols(o_vc, nat2),
    ])


def _rope_tables():
    nf = HEAD_DIM // 4
    inv = ROPE_THETA ** (-jnp.arange(nf, dtype=F32) / nf)
    pos = jnp.arange(SEQ)
    ang_r = (pos // GRID_W).astype(F32)[:, None] * inv[None, :]
    ang_c = (pos % GRID_W).astype(F32)[:, None] * inv[None, :]
    cos = jnp.concatenate([jnp.cos(ang_r)] * 2 + [jnp.cos(ang_c)] * 2, axis=-1)
    sin = jnp.concatenate([-jnp.sin(ang_r), jnp.sin(ang_r), -jnp.sin(ang_c), jnp.sin(ang_c)], axis=-1)
    return jnp.concatenate([cos, cos], axis=-1), jnp.concatenate([sin, sin], axis=-1)


def _layer_params(l, w_in, ln1, qk_gain, sink, rpb, out_gain, w_out, ln2, w_rg, b_rg, w_re, b_re):
    cols = _proj_columns()
    w_ext = w_in[l][:, cols].astype(BF16)
    qs = HEAD_DIM ** -0.5 * LOG2E
    g = qk_gain[l].astype(F32)
    sw = _ROPE_SWAP
    gain = jnp.concatenate([
        jnp.tile(g[0, 0], A_HEADS) * qs, jnp.tile(g[0, 1], A_KV),
        jnp.tile(g[1, 0], B_HEADS) * qs, jnp.tile(g[1, 1], B_HEADS),
        jnp.tile(g[2, 0], C_HEADS) * qs, jnp.tile(g[2, 1], C_KV),
        jnp.tile(g[2, 0][sw], C_HEADS) * qs, jnp.tile(g[2, 1][sw], C_KV),
    ])[None, :]
    order = list(PAIR_ORDER)
    sink2 = sink[l].astype(F32)[jnp.asarray(order)] * LOG2E
    rows_a = _head_cols(0, PAIR_ORDER)
    rows_b = A_WIDTH + np.arange(B_WIDTH)
    rows_c = _head_cols(A_WIDTH + B_WIDTH, PAIR_ORDER)
    og = out_gain[l].astype(F32)
    wo = w_out[l]
    wr = jnp.concatenate([w_rg[l], w_re[l], jnp.zeros((D_MODEL, LANES - N_GROUPS - N_EXPERTS), F32)], axis=1)
    wr_hi = wr.astype(BF16)
    wr_lo = (wr - wr_hi.astype(F32)).astype(BF16)
    br = jnp.concatenate([b_rg[l].astype(F32), b_re[l].astype(F32),
                          jnp.zeros((LANES - N_GROUPS - N_EXPERTS,), F32)])[None, :]
    return dict(
        w_ext=w_ext, ln1=ln1[l][None, :], gain=gain, sink2=sink2, nbr_bias=_nbr_bias_table(rpb[l]),
        ga=og[rows_a][None, :], gb=og[rows_b][None, :], gc=og[rows_c][None, :],
        wa=wo[rows_a].astype(BF16), wb=wo[rows_b].astype(BF16), wc=wo[rows_c].astype(BF16),
        ln2=ln2[l][None, :], wr=jnp.concatenate([wr_hi, wr_lo], axis=1), br=br,
    )


def _block_ones():
    idx = np.arange(MXU_DIM) // HEAD_DIM
    return jnp.asarray((idx[:, None] == idx[None, :]).astype(np.float32)).astype(BF16)


def _trunk(x, params, shared, moe_w):
    b = x.shape[0]
    x2d = x.reshape(b * SEQ, D_MODEL)
    for l, lp in enumerate(params):
        qa, ka, va, qb, kb, vb, qc, kc, vc = _in_proj(
            x2d, lp["w_ext"], lp["ln1"], lp["gain"], shared["cos"], shared["sin"], shared["ones"])
        seq = lambda z: z.reshape(b, SEQ, z.shape[-1])
        oa = _win_attn(seq(qa), seq(ka), seq(va), shared["win_bias"], lp["sink2"])
        ob = _nbr_attn(seq(qb), seq(kb), seq(vb), lp["nbr_bias"])
        oc = _dense_attn(seq(qc), seq(kc), seq(vc))
        flat = lambda z: z.reshape(b * SEQ, z.shape[-1])
        xn, hrow = _out_proj(flat(oa), flat(ob), flat(oc), x2d, lp["ga"], lp["gb"], lp["gc"],
                             lp["wa"], lp["wb"], lp["wc"], lp["ln2"], lp["wr"], lp["br"])
        wg, wu, wd = moe_w[l]
        x2d = _moe(hrow, xn, wg, wu, wd)
    return x2d.reshape(b, SEQ, D_MODEL)


def kernel(x_prompt, x_sample, ln1, w_in, qk_gain, sink, rpb, out_gain, w_out, ln2, w_router_group,
           b_router_group, w_router_expert, b_router_expert, w_gate, w_up, w_down):
    depth = w_in.shape[0]
    params = [_layer_params(l, w_in, ln1, qk_gain, sink, rpb, out_gain, w_out, ln2, w_router_group,
                            b_router_group, w_router_expert, b_router_expert) for l in range(depth)]
    cos_t, sin_t = _rope_tables()
    shared = dict(cos=cos_t, sin=sin_t, ones=_block_ones(), win_bias=_win_bias_table())
    moe_w = [(w_gate[l].astype(BF16), w_up[l].astype(BF16), w_down[l].astype(BF16)) for l in range(depth)]
    y_prompt = _trunk(x_prompt, params, shared, moe_w)
    y_sample = _trunk(x_sample, params, shared, moe_w)
    return (y_prompt, y_sample)
```

```python
import functools
import math

import numpy as np
import jax
import jax.numpy as jnp
from jax import lax
from jax.experimental import pallas as pl
from jax.experimental.pallas import tpu as pltpu

D_MODEL = 1024
SEQ = 4096
HEAD_DIM = 64
GRID_W = 64
GRID_ROWS = SEQ // GRID_W
A_HEADS, A_KV, A_WINDOW = 6, 2, 128
B_HEADS, B_ROWS, B_COLS = 4, 8, 16
C_HEADS, C_KV = 6, 2
ROPE_THETA = 10000.0
N_GROUPS, EXPERTS_PER_GROUP = 4, 4
N_EXPERTS = N_GROUPS * EXPERTS_PER_GROUP
D_EXPERT = D_MODEL // 2
EPS = 1e-6
NEG = -1e30
LOG2E = math.log2(math.e)

A_WIDTH = A_HEADS * HEAD_DIM
B_WIDTH = B_HEADS * HEAD_DIM
C_WIDTH = C_HEADS * HEAD_DIM
KV_WIDTH = A_KV * HEAD_DIM

LANES = 128
MXU_DIM = 256
VMEM_LIMIT = 56 * 1024 * 1024

PAIR_ORDER = (0, 3, 1, 4, 2, 5)

TM_PROJ = 512
TQ_A = 256
KW_A = 512
TQ_B = 256
KW_B = 768
TQ_C = 256
KC_C = 512
TM_MOE = 256
TM_COMB = 512

PAIR_A = (0, 0, 0, 1, 1, 2)
PAIR_B = (1, 2, 3, 2, 3, 3)
PAIRS_PER_GROUP = len(PAIR_A)
N_CLASSES = N_GROUPS * PAIRS_PER_GROUP
CLASS_LANE = N_GROUPS + N_EXPERTS
SUBLANES = 8
GATE_ROW = D_MODEL // 2 // LANES
GATHER_SLOTS = 3

BF16 = jnp.bfloat16
F32 = jnp.float32

_ROPE_SWAP = np.concatenate([np.arange(16, 32), np.arange(0, 16), np.arange(48, 64), np.arange(32, 48)])


def _cparams(sem):
    return pltpu.CompilerParams(dimension_semantics=sem, vmem_limit_bytes=VMEM_LIMIT)


N_NORMED = 1536
N_GAINED = 2048
N_PROJ = 2560


def _in_proj_kernel(x_ref, w_ref, ln_ref, gain_ref, cos_ref, sin_ref, ones_ref,
                    qa_ref, ka_ref, va_ref, qb_ref, kb_ref, vb_ref, qc_ref, kc_ref, vc_ref):
    x = x_ref[...]
    ms = jnp.mean(x * x, axis=-1, keepdims=True)
    h = (x * lax.rsqrt(ms + EPS) * ln_ref[...]).astype(BF16)
    p = jnp.dot(h, w_ref[...], preferred_element_type=F32)
    g = gain_ref[...]
    ones_blk = ones_ref[...]

    def chunk(c):
        return p[:, MXU_DIM * c:MXU_DIM * (c + 1)]

    def inv_rms(pc):
        ss = jnp.dot((pc * pc).astype(BF16), ones_blk, preferred_element_type=F32)
        return lax.rsqrt(ss * (1.0 / HEAD_DIM) + EPS)

    r = [inv_rms(chunk(c)) for c in range(N_NORMED // MXU_DIM)]
    pn = [chunk(c) * r[c] * g[:, MXU_DIM * c:MXU_DIM * (c + 1)] for c in range(N_NORMED // MXU_DIM)]
    qa_ref[:, 0:256] = pn[0].astype(BF16)
    qa_ref[:, 256:384] = pn[1][:, :128].astype(BF16)
    ka_ref[...] = pn[1][:, 128:].astype(BF16)
    qb_ref[...] = pn[2].astype(BF16)
    kb_ref[...] = pn[3].astype(BF16)
    cos = cos_ref[...]
    sin = sin_ref[...]
    cos2 = jnp.concatenate([cos, cos], axis=1)
    sin2 = jnp.concatenate([sin, sin], axis=1)
    sw4 = chunk(6) * r[4] * g[:, 1536:1792]
    sw5 = chunk(7) * r[5] * g[:, 1792:2048]
    c4 = pn[4] * cos2 + sw4 * sin2
    c5 = pn[5] * cos2 + sw5 * sin2
    qc_ref[:, 0:256] = c4.astype(BF16)
    qc_ref[:, 256:384] = c5[:, :128].astype(BF16)
    kc_ref[...] = c5[:, 128:].astype(BF16)
    va_ref[...] = p[:, 2048:2176].astype(BF16)
    vb_ref[...] = p[:, 2176:2432].astype(BF16)
    vc_ref[:, 0:128] = p[:, 2432:2560].astype(BF16)
    vc_ref[:, 128:256] = jnp.ones((x.shape[0], 128), BF16)


def _in_proj(x2d, w_ext, ln, gain, cos_t, sin_t, ones_blk):
    n = x2d.shape[0]
    tm = TM_PROJ
    tiles_per_seq = SEQ // tm
    row = lambda i: (i, 0)
    fixed = lambda i: (0, 0)
    pos = lambda i: (i % tiles_per_seq, 0)
    widths = (A_WIDTH, KV_WIDTH, KV_WIDTH, B_WIDTH, B_WIDTH, B_WIDTH, C_WIDTH, KV_WIDTH, 2 * KV_WIDTH)
    return pl.pallas_call(
        _in_proj_kernel,
        grid=(n // tm,),
        in_specs=[
            pl.BlockSpec((tm, D_MODEL), row),
            pl.BlockSpec((D_MODEL, N_PROJ), fixed),
            pl.BlockSpec((1, D_MODEL), fixed),
            pl.BlockSpec((1, N_GAINED), fixed),
            pl.BlockSpec((tm, LANES), pos),
            pl.BlockSpec((tm, LANES), pos),
            pl.BlockSpec((MXU_DIM, MXU_DIM), fixed),
        ],
        out_specs=[pl.BlockSpec((tm, w), row) for w in widths],
        out_shape=[jax.ShapeDtypeStruct((n, w), BF16) for w in widths],
        compiler_params=_cparams(("parallel",)),
        name="in_proj",
    )(x2d, w_ext, ln, gain, cos_t, sin_t, ones_blk)


def _half_masks():
    lane = lax.broadcasted_iota(jnp.int32, (1, LANES), 1)
    lo = lane < HEAD_DIM
    return lo, lo.astype(BF16), (~lo).astype(BF16)


def _win_attn_kernel(sink_ref, q_ref, k_ref, v_ref, bias_ref, o_ref):
    j = pl.program_id(1)
    start = pl.multiple_of(jnp.clip(j * TQ_A - A_WINDOW, 0, SEQ - KW_A), 128)
    kw = k_ref[0, pl.ds(start, KW_A), :]
    vw = v_ref[0, pl.ds(start, KW_A), :]
    lo, m_lo, m_hi = _half_masks()
    for p in range(A_HEADS // 2):
        qblk = q_ref[0, :, LANES * p:LANES * (p + 1)]
        outs = []
        for half in range(2):
            hidx = 2 * p + half
            qm = qblk * (m_lo if half == 0 else m_hi)
            s = lax.dot_general(qm, kw, (((1,), (1,)), ((), ())), preferred_element_type=F32)
            s = s + bias_ref[0, hidx]
            sk = sink_ref[hidx]
            m = jnp.maximum(jnp.max(s, axis=-1, keepdims=True), sk)
            e = jnp.exp2(s - m)
            l = jnp.sum(e, axis=-1, keepdims=True) + jnp.exp2(sk - m)
            o2 = jnp.dot(e.astype(BF16), vw, preferred_element_type=F32)
            outs.append(o2 / l)
        o_ref[0, :, LANES * p:LANES * (p + 1)] = jnp.where(lo, outs[0], outs[1]).astype(BF16)


def _win_attn(qa, ka, va, bias, sink2):
    b = qa.shape[0]
    nq = SEQ // TQ_A
    variant = lambda bi, j: (jnp.where(j == 0, 0, jnp.where(j == nq - 1, 2, 1)), 0, 0, 0)
    return pl.pallas_call(
        _win_attn_kernel,
        grid=(b, nq),
        in_specs=[
            pl.BlockSpec(memory_space=pltpu.SMEM),
            pl.BlockSpec((1, TQ_A, A_WIDTH), lambda bi, j: (bi, j, 0)),
            pl.BlockSpec((1, SEQ, KV_WIDTH), lambda bi, j: (bi, 0, 0)),
            pl.BlockSpec((1, SEQ, KV_WIDTH), lambda bi, j: (bi, 0, 0)),
            pl.BlockSpec((1, A_HEADS, TQ_A, KW_A), variant),
        ],
        out_specs=pl.BlockSpec((1, TQ_A, A_WIDTH), lambda bi, j: (bi, j, 0)),
        out_shape=jax.ShapeDtypeStruct((b, SEQ, A_WIDTH), BF16),
        compiler_params=_cparams(("parallel", "arbitrary")),
        name="win_attn",
    )(sink2, qa, ka, va, bias)


def _win_bias_table():
    slopes = np.array([2.0 ** (-8.0 * (n + 1) / A_HEADS) for n in range(A_HEADS)], np.float32)[list(PAIR_ORDER)]
    i = np.arange(TQ_A)[:, None]
    jj = np.arange(KW_A)[None, :]
    tabs = []
    for off in (0, A_WINDOW, KW_A - TQ_A):
        dist = np.abs(off + i - jj).astype(np.float32)
        tab = np.where(dist[None] <= A_WINDOW, -slopes[:, None, None] * dist[None] * LOG2E, NEG)
        tabs.append(tab)
    return jnp.asarray(np.stack(tabs).astype(np.float32))


def _nbr_attn_kernel(q_ref, k_ref, v_ref, bias_ref, o_ref):
    j = pl.program_id(1)
    rows_per_tile = TQ_B // GRID_W
    krow0 = jnp.clip(j * rows_per_tile - B_ROWS // 2, 0, GRID_ROWS - KW_B // GRID_W)
    start = pl.multiple_of(krow0 * GRID_W, 256)
    lo, m_lo, m_hi = _half_masks()
    for p in range(B_HEADS // 2):
        qblk = q_ref[0, :, LANES * p:LANES * (p + 1)]
        kw = k_ref[0, pl.ds(start, KW_B), LANES * p:LANES * (p + 1)]
        vw = v_ref[0, pl.ds(start, KW_B), LANES * p:LANES * (p + 1)]
        outs = []
        for half in range(2):
            hidx = 2 * p + half
            qm = qblk * (m_lo if half == 0 else m_hi)
            s = lax.dot_general(qm, kw, (((1,), (1,)), ((), ())), preferred_element_type=F32)
            s = s + bias_ref[0, hidx]
            m = jnp.max(s, axis=-1, keepdims=True)
            e = jnp.exp2(s - m)
            l = jnp.sum(e, axis=-1, keepdims=True)
            o2 = jnp.dot(e.astype(BF16), vw, preferred_element_type=F32)
            outs.append(o2 / l)
        o_ref[0, :, LANES * p:LANES * (p + 1)] = jnp.where(lo, outs[0], outs[1]).astype(BF16)


def _nbr_attn(qb, kb, vb, bias):
    b = qb.shape[0]
    nq = SEQ // TQ_B
    variant = lambda bi, j: (jnp.where(j == 0, 0, jnp.where(j == nq - 1, 2, 1)), 0, 0, 0)
    return pl.pallas_call(
        _nbr_attn_kernel,
        grid=(b, nq),
        in_specs=[
            pl.BlockSpec((1, TQ_B, B_WIDTH), lambda bi, j: (bi, j, 0)),
            pl.BlockSpec((1, SEQ, B_WIDTH), lambda bi, j: (bi, 0, 0)),
            pl.BlockSpec((1, SEQ, B_WIDTH), lambda bi, j: (bi, 0, 0)),
            pl.BlockSpec((1, B_HEADS, TQ_B, KW_B), variant),
        ],
        out_specs=pl.BlockSpec((1, TQ_B, B_WIDTH), lambda bi, j: (bi, j, 0)),
        out_shape=jax.ShapeDtypeStruct((b, SEQ, B_WIDTH), BF16),
        compiler_params=_cparams(("parallel", "arbitrary")),
        name="nbr_attn",
    )(qb, kb, vb, bias)


def _nbr_bias_table(rpb):
    rows_per_tile = TQ_B // GRID_W
    krows = KW_B // GRID_W
    r = rpb.astype(F32) * LOG2E
    edge = GRID_W - B_COLS
    ext = jnp.concatenate([jnp.repeat(r[..., :1], edge, axis=-1), r, jnp.repeat(r[..., -1:], edge, axis=-1)], axis=-1)
    col = jnp.stack([ext[..., GRID_W - 1 - q:2 * GRID_W - 1 - q] for q in range(GRID_W)], axis=2)
    tabs = []
    for r_first in (0, rows_per_tile, GRID_ROWS - rows_per_tile):
        krow0 = int(np.clip(r_first - B_ROWS // 2, 0, GRID_ROWS - krows))
        slabs = []
        for ql in range(rows_per_tile):
            per_k = [col[:, int(np.clip(krow0 + kl - (r_first + ql) + B_ROWS - 1, 0, 2 * B_ROWS - 2))]
                     for kl in range(krows)]
            slabs.append(jnp.stack(per_k, axis=2))
        tab = jnp.stack(slabs, axis=1).reshape(B_HEADS, TQ_B, KW_B)
        qi = np.arange(TQ_B)
        kj = np.arange(KW_B)
        qr = (r_first + qi // GRID_W)[:, None]
        qcol = (qi % GRID_W)[:, None]
        kr = (krow0 + kj // GRID_W)[None, :]
        kcol = (kj % GRID_W)[None, :]
        r0 = np.clip(qr - B_ROWS // 2, 0, GRID_ROWS - B_ROWS)
        c0 = np.clip(qcol - B_COLS // 2, 0, GRID_W - B_COLS)
        valid = (kr >= r0) & (kr < r0 + B_ROWS) & (kcol >= c0) & (kcol < c0 + B_COLS)
        tabs.append(jnp.where(jnp.asarray(valid)[None], tab, NEG))
    return jnp.stack(tabs)


def _dense_attn_kernel(q_ref, k_ref, v_ref, o_ref, s_ref):
    lo, m_lo, m_hi = _half_masks()
    nchunk = SEQ // KC_C
    npairs = C_HEADS // 2

    def scores(p):
        qblk = q_ref[0, :, LANES * p:LANES * (p + 1)]
        q2 = jnp.concatenate([qblk * m_lo, qblk * m_hi], axis=0)
        m_run = None
        for c in range(nchunk):
            kc = k_ref[0, KC_C * c:KC_C * (c + 1), :]
            s = lax.dot_general(q2, kc, (((1,), (1,)), ((), ())), preferred_element_type=F32)
            s_ref[p % 2, :, KC_C * c:KC_C * (c + 1)] = s
            for t in range(KC_C // LANES):
                blk = s[:, LANES * t:LANES * (t + 1)]
                m_run = blk if m_run is None else jnp.maximum(m_run, blk)
        return jnp.max(m_run, axis=-1, keepdims=True)

    def weighted_values(p, m):
        acc = None
        for c in range(nchunk):
            e = jnp.exp2(s_ref[p % 2, :, KC_C * c:KC_C * (c + 1)] - m).astype(BF16)
            part = jnp.dot(e, v_ref[0, KC_C * c:KC_C * (c + 1), :], preferred_element_type=F32)
            acc = part if acc is None else acc + part
        on = acc[:, :LANES] / acc[:, LANES:]
        o_ref[0, :, LANES * p:LANES * (p + 1)] = jnp.where(lo, on[:TQ_C], on[TQ_C:]).astype(BF16)

    m_next = scores(0)
    for p in range(npairs):
        m_cur = m_next
        if p + 1 < npairs:
            m_next = scores(p + 1)
        weighted_values(p, m_cur)


def _dense_attn(qc, kc, vc):
    b = qc.shape[0]
    return pl.pallas_call(
        _dense_attn_kernel,
        grid=(b, SEQ // TQ_C),
        in_specs=[
            pl.BlockSpec((1, TQ_C, C_WIDTH), lambda bi, j: (bi, j, 0)),
            pl.BlockSpec((1, SEQ, KV_WIDTH), lambda bi, j: (bi, 0, 0)),
            pl.BlockSpec((1, SEQ, 2 * KV_WIDTH), lambda bi, j: (bi, 0, 0)),
        ],
        out_specs=pl.BlockSpec((1, TQ_C, C_WIDTH), lambda bi, j: (bi, j, 0)),
        out_shape=jax.ShapeDtypeStruct((b, SEQ, C_WIDTH), BF16),
        scratch_shapes=[pltpu.VMEM((2, 2 * TQ_C, SEQ), F32)],
        compiler_params=_cparams(("parallel", "arbitrary")),
        name="dense_attn",
    )(qc, kc, vc)


def _out_proj_kernel(oa_ref, ob_ref, oc_ref, x_ref, ga_ref, gb_ref, gc_ref, wa_ref, wb_ref, wc_ref,
                     ln_ref, wr_ref, br_ref, xn_ref, h_ref):
    def nrm(o_ref, g_ref):
        o = o_ref[...].astype(F32)
        ms = jnp.mean(o * o, axis=-1, keepdims=True)
        return (o * lax.rsqrt(ms + EPS) * g_ref[...]).astype(BF16)

    acc = jnp.dot(nrm(oa_ref, ga_ref), wa_ref[...], preferred_element_type=F32)
    acc = acc + jnp.dot(nrm(ob_ref, gb_ref), wb_ref[...], preferred_element_type=F32)
    acc = acc + jnp.dot(nrm(oc_ref, gc_ref), wc_ref[...], preferred_element_type=F32)
    xn = x_ref[...] + acc
    xn_ref[...] = xn
    ms = jnp.mean(xn * xn, axis=-1, keepdims=True)
    h2 = xn * lax.rsqrt(ms + EPS) * ln_ref[...]
    hi = h2.astype(BF16)
    lo = (h2 - hi.astype(F32)).astype(BF16)
    tm = xn.shape[0]
    half = D_MODEL // 2
    bits_lo = lax.bitcast_convert_type(hi[:, :half].astype(F32), jnp.uint32)
    bits_hi = lax.bitcast_convert_type(hi[:, half:].astype(F32), jnp.uint32)
    words = (bits_hi & jnp.uint32(0xFFFF0000)) | (bits_lo >> 16)
    for s in range(GATE_ROW):
        h_ref[pl.ds(s, tm, stride=SUBLANES), :] = words[:, LANES * s:LANES * (s + 1)]
    for s in range(GATE_ROW + 1, SUBLANES):
        h_ref[pl.ds(s, tm, stride=SUBLANES), :] = jnp.zeros((tm, LANES), jnp.uint32)
    wr = wr_ref[...]
    t = jnp.dot(hi, wr, preferred_element_type=F32)
    u = jnp.dot(lo, wr[:, :LANES], preferred_element_type=F32)
    logits = t[:, :LANES] + t[:, LANES:] + u + br_ref[...]

    lane = lax.broadcasted_iota(jnp.int32, logits.shape, 1).astype(F32)
    big = jnp.float32(3.0e38)
    is_g = lane < N_GROUPS
    gl = jnp.where(is_g, logits, -big)
    mg = jnp.max(gl, axis=-1, keepdims=True)
    grp = jnp.min(jnp.where(gl == mg, lane, big), axis=-1, keepdims=True)
    pg = 1.0 / jnp.sum(jnp.where(is_g, jnp.exp(gl - mg), 0.0), axis=-1, keepdims=True)
    e_lo = N_GROUPS + EXPERTS_PER_GROUP * grp
    sel = (lane >= e_lo) & (lane < e_lo + EXPERTS_PER_GROUP)
    el = jnp.where(sel, logits, -big)
    v1 = jnp.max(el, axis=-1, keepdims=True)
    i1 = jnp.min(jnp.where(el == v1, lane, big), axis=-1, keepdims=True)
    el2 = jnp.where(lane == i1, -big, el)
    v2 = jnp.max(el2, axis=-1, keepdims=True)
    i2 = jnp.min(jnp.where(el2 == v2, lane, big), axis=-1, keepdims=True)
    e21 = jnp.exp(v2 - v1)
    w1 = pg / (1.0 + e21)
    w2 = pg * e21 / (1.0 + e21)
    a = jnp.minimum(i1, i2) - e_lo
    b = jnp.maximum(i1, i2) - e_lo
    cls = grp * PAIRS_PER_GROUP + (a * (7.0 - a) * 0.5 + (b - a - 1.0))
    gates = jnp.where(lane == i1, w1, jnp.where(lane == i2, w2, jnp.where(lane == CLASS_LANE, cls, 0.0)))
    h_ref[pl.ds(GATE_ROW, tm, stride=SUBLANES), :] = lax.bitcast_convert_type(gates, jnp.uint32)


def _out_proj(oa, ob, oc, x2d, ga, gb, gc, wa, wb, wc, ln2, wr, br):
    n = x2d.shape[0]
    tm = TM_PROJ
    row = lambda i: (i, 0)
    fixed = lambda i: (0, 0)
    return pl.pallas_call(
        _out_proj_kernel,
        grid=(n // tm,),
        in_specs=[
            pl.BlockSpec((tm, A_WIDTH), row),
            pl.BlockSpec((tm, B_WIDTH), row),
            pl.BlockSpec((tm, C_WIDTH), row),
            pl.BlockSpec((tm, D_MODEL), row),
            pl.BlockSpec((1, A_WIDTH), fixed),
            pl.BlockSpec((1, B_WIDTH), fixed),
            pl.BlockSpec((1, C_WIDTH), fixed),
            pl.BlockSpec((A_WIDTH, D_MODEL), fixed),
            pl.BlockSpec((B_WIDTH, D_MODEL), fixed),
            pl.BlockSpec((C_WIDTH, D_MODEL), fixed),
            pl.BlockSpec((1, D_MODEL), fixed),
            pl.BlockSpec((D_MODEL, 2 * LANES), fixed),
            pl.BlockSpec((1, LANES), fixed),
        ],
        out_specs=[
            pl.BlockSpec((tm, D_MODEL), row),
            pl.BlockSpec((tm * SUBLANES, LANES), row),
        ],
        out_shape=[
            jax.ShapeDtypeStruct((n, D_MODEL), F32),
            jax.ShapeDtypeStruct((n * SUBLANES, LANES), jnp.uint32),
        ],
        compiler_params=_cparams(("parallel",)),
        name="out_proj_router",
    )(oa, ob, oc, x2d, ga, gb, gc, wa, wb, wc, ln2, wr, br)


def _tile_copy(src_hbm, buf, sem, slot, src_row8, dst_tok):
    src = src_hbm.at[pl.ds(pl.multiple_of(src_row8, SUBLANES), SUBLANES), :]
    return pltpu.make_async_copy(src, buf[slot].at[pl.ds(SUBLANES * dst_tok, SUBLANES), :], sem.at[slot])


def _issue_gather(idx_ref, src_hbm, buf, sem, slot, toks):
    for r in range(toks):
        _tile_copy(src_hbm, buf, sem, slot, idx_ref[0, 0, r], r).start()


def _wait_gather(src_hbm, buf, sem, slot, toks):
    pltpu.make_async_copy(src_hbm.at[pl.ds(0, SUBLANES * toks), :], buf[slot], sem.at[slot]).wait()


def _tile_row(ref, s, toks):
    return ref[pl.ds(s, toks, stride=SUBLANES), :]


def _moe_kernel(ea_ref, eb_ref, nvalid_ref, idx0_ref, idx1_ref, idxn_ref, h_hbm,
                wga_ref, wua_ref, wda_ref, wgb_ref, wub_ref, wdb_ref, y_ref, buf0, buf1, buf2, sem):
    buf = (buf0, buf1, buf2)
    tm = TM_MOE
    i = pl.program_id(0)
    nvalid = nvalid_ref[0]

    @pl.when(i == 0)
    def _():
        _issue_gather(idx0_ref, h_hbm, buf, sem, 0, tm)
        _issue_gather(idx1_ref, h_hbm, buf, sem, 1, tm)

    def tile_body(slot):
        cur = buf[slot]
        _wait_gather(h_hbm, buf, sem, slot, tm)
        _issue_gather(idxn_ref, h_hbm, buf, sem, (slot + 2) % GATHER_SLOTS, tm)
        parts_lo, parts_hi = [], []
        for s in range(D_MODEL // 2 // LANES):
            w = _tile_row(cur, s, tm)
            parts_lo.append(lax.bitcast_convert_type(w << 16, F32).astype(BF16))
            parts_hi.append(lax.bitcast_convert_type(w & jnp.uint32(0xFFFF0000), F32).astype(BF16))
        x = jnp.concatenate(parts_lo + parts_hi, axis=1)
        gates = lax.bitcast_convert_type(_tile_row(cur, GATE_ROW, tm), F32)
        lane = lax.broadcasted_iota(jnp.int32, gates.shape, 1)

        def expert(wg_ref, wu_ref, wd_ref, e):
            g = jnp.dot(x, wg_ref[0], preferred_element_type=F32)
            u = jnp.dot(x, wu_ref[0], preferred_element_type=F32)
            act = (g / (1.0 + jnp.exp(-g)) * u).astype(BF16)
            y = jnp.dot(act, wd_ref[0], preferred_element_type=F32)
            ge = jnp.sum(jnp.where(lane == e + N_GROUPS, gates, 0.0), axis=-1, keepdims=True)
            return ge * y

        y = expert(wga_ref, wua_ref, wda_ref, ea_ref[i]) + expert(wgb_ref, wub_ref, wdb_ref, eb_ref[i])
        for s in range(SUBLANES):
            y_ref[pl.ds(s, tm, stride=SUBLANES), :] = y[:, LANES * s:LANES * (s + 1)]

        @pl.when(i == nvalid - 1)
        def _():
            _wait_gather(h_hbm, buf, sem, (slot + 1) % GATHER_SLOTS, tm)
            _wait_gather(h_hbm, buf, sem, (slot + 2) % GATHER_SLOTS, tm)

    for slot in range(GATHER_SLOTS):
        @pl.when((i < nvalid) & (i % GATHER_SLOTS == slot))
        def _(slot=slot):
            tile_body(slot)

    @pl.when(i >= nvalid)
    def _():
        y_ref[...] = jnp.zeros(y_ref.shape, F32)


def _moe_experts(hrow, ea, eb, nvalid, src_idx, wg, wu, wd):
    tm = TM_MOE
    nt = ea.shape[0]
    smem_blk = lambda f: pl.BlockSpec((1, 1, tm), f, memory_space=pltpu.SMEM)
    w_in_spec = lambda sel: pl.BlockSpec((1, D_MODEL, D_EXPERT), sel)
    w_out_spec = lambda sel: pl.BlockSpec((1, D_EXPERT, D_MODEL), sel)
    sel_a = lambda i, ea, eb, nv: (ea[i], 0, 0)
    sel_b = lambda i, ea, eb, nv: (eb[i], 0, 0)
    grid_spec = pltpu.PrefetchScalarGridSpec(
        num_scalar_prefetch=3,
        grid=(nt,),
        in_specs=[
            smem_blk(lambda i, ea, eb, nv: (0, 0, 0)),
            smem_blk(lambda i, ea, eb, nv: (1, 0, 0)),
            smem_blk(lambda i, ea, eb, nv: (i + 2, 0, 0)),
            pl.BlockSpec(memory_space=pl.ANY),
            w_in_spec(sel_a), w_in_spec(sel_a), w_out_spec(sel_a),
            w_in_spec(sel_b), w_in_spec(sel_b), w_out_spec(sel_b),
        ],
        out_specs=pl.BlockSpec((tm * SUBLANES, LANES), lambda i, ea, eb, nv: (i, 0)),
        scratch_shapes=[pltpu.VMEM((tm * SUBLANES, LANES), jnp.uint32) for _ in range(GATHER_SLOTS)]
        + [pltpu.SemaphoreType.DMA((GATHER_SLOTS,))],
    )
    return pl.pallas_call(
        _moe_kernel,
        grid_spec=grid_spec,
        out_shape=jax.ShapeDtypeStruct((nt * tm * SUBLANES, LANES), F32),
        compiler_params=_cparams(("arbitrary",)),
        name="moe_experts",
    )(ea, eb, nvalid, src_idx, src_idx, src_idx, hrow, wg, wu, wd, wg, wu, wd)


def _combine_kernel(pos0_ref, pos1_ref, posn_ref, xn_ref, y_hbm, o_ref, buf0, buf1, buf2, sem):
    buf = (buf0, buf1, buf2)
    tm = TM_COMB
    i = pl.program_id(0)

    @pl.when(i == 0)
    def _():
        _issue_gather(pos0_ref, y_hbm, buf, sem, 0, tm)
        _issue_gather(pos1_ref, y_hbm, buf, sem, 1, tm)

    for slot in range(GATHER_SLOTS):
        @pl.when(i % GATHER_SLOTS == slot)
        def _(slot=slot):
            _wait_gather(y_hbm, buf, sem, slot, tm)
            _issue_gather(posn_ref, y_hbm, buf, sem, (slot + 2) % GATHER_SLOTS, tm)
            for s in range(SUBLANES):
                cols = slice(LANES * s, LANES * (s + 1))
                o_ref[:, cols] = xn_ref[:, cols] + _tile_row(buf[slot], s, tm)

            @pl.when(i == pl.num_programs(0) - 1)
            def _():
                _wait_gather(y_hbm, buf, sem, (slot + 1) % GATHER_SLOTS, tm)
                _wait_gather(y_hbm, buf, sem, (slot + 2) % GATHER_SLOTS, tm)


def _combine(pos_tiles, xn, y_sorted):
    n = xn.shape[0]
    tm = TM_COMB
    smem_blk = lambda f: pl.BlockSpec((1, 1, tm), f, memory_space=pltpu.SMEM)
    return pl.pallas_call(
        _combine_kernel,
        grid=(n // tm,),
        in_specs=[
            smem_blk(lambda i: (0, 0, 0)),
            smem_blk(lambda i: (1, 0, 0)),
            smem_blk(lambda i: (i + 2, 0, 0)),
            pl.BlockSpec((tm, D_MODEL), lambda i: (i, 0)),
            pl.BlockSpec(memory_space=pl.ANY),
        ],
        out_specs=pl.BlockSpec((tm, D_MODEL), lambda i: (i, 0)),
        out_shape=jax.ShapeDtypeStruct((n, D_MODEL), F32),
        scratch_shapes=[pltpu.VMEM((tm * SUBLANES, LANES), F32) for _ in range(GATHER_SLOTS)]
        + [pltpu.SemaphoreType.DMA((GATHER_SLOTS,))],
        compiler_params=_cparams(("arbitrary",)),
        name="moe_combine",
    )(pos_tiles, pos_tiles, pos_tiles, xn, y_sorted)


def _moe_plan(cls, n):
    tm = TM_MOE
    nt = n // tm + N_CLASSES
    onehot = (cls[:, None] == jnp.arange(N_CLASSES, dtype=jnp.int32)[None, :]).astype(jnp.int32)
    counts = jnp.sum(onehot, axis=0)
    rank = jnp.sum(jnp.cumsum(onehot, axis=0) * onehot, axis=1) - 1
    tiles = (counts + tm - 1) // tm
    tile_end = jnp.cumsum(tiles)
    tile_start = tile_end - tiles
    pos = jnp.sum(onehot * tile_start[None, :], axis=1) * tm + rank
    pad = GATHER_SLOTS - 1
    src_idx = jnp.zeros(((nt + pad) * tm,), jnp.int32).at[pos].set(
        SUBLANES * jnp.arange(n, dtype=jnp.int32), unique_indices=True)
    nvalid = tile_end[-1]
    tile_id = jnp.minimum(jnp.arange(nt, dtype=jnp.int32), nvalid - 1)
    tcls = jnp.sum((tile_id[:, None] >= tile_end[None, :]).astype(jnp.int32), axis=1)
    grp, pair = tcls // PAIRS_PER_GROUP, tcls % PAIRS_PER_GROUP
    pair_a = jnp.asarray(PAIR_A, jnp.int32)
    pair_b = jnp.asarray(PAIR_B, jnp.int32)
    ea = grp * EXPERTS_PER_GROUP + jnp.sum((pair[:, None] == jnp.arange(PAIRS_PER_GROUP)[None]) * pair_a[None], axis=1)
    eb = grp * EXPERTS_PER_GROUP + jnp.sum((pair[:, None] == jnp.arange(PAIRS_PER_GROUP)[None]) * pair_b[None], axis=1)
    pos_tiles = jnp.concatenate([SUBLANES * pos, jnp.zeros((pad * TM_COMB,), jnp.int32)])
    pos_tiles = pos_tiles.reshape(n // TM_COMB + pad, 1, TM_COMB)
    return (ea.astype(jnp.int32), eb.astype(jnp.int32), nvalid.reshape(1).astype(jnp.int32),
            src_idx.reshape(nt + pad, 1, tm), pos_tiles)


def _moe(hrow, xn, wg, wu, wd):
    n = xn.shape[0]
    gate_rows = hrow.reshape(n, SUBLANES, LANES)[:, GATE_ROW, CLASS_LANE]
    cls = lax.bitcast_convert_type(gate_rows, F32).astype(jnp.int32)
    ea, eb, nvalid, src_idx, pos_tiles = _moe_plan(cls, n)
    y_sorted = _moe_experts(hrow, ea, eb, nvalid, src_idx, wg, wu, wd)
    return _combine(pos_tiles, xn, y_sorted)


def _head_cols(base, heads, swap=False):
    inner = _ROPE_SWAP if swap else np.arange(HEAD_DIM)
    return np.concatenate([base + HEAD_DIM * h + inner for h in heads])


def _proj_columns():
    o_qa, o_ka, o_va = 0, A_WIDTH, A_WIDTH + KV_WIDTH
    o_qb = o_va + KV_WIDTH
    o_kb, o_vb = o_qb + B_WIDTH, o_qb + 2 * B_WIDTH
    o_qc = o_vb + B_WIDTH
    o_kc, o_vc = o_qc + C_WIDTH, o_qc + C_WIDTH + KV_WIDTH
    nat2, nat4 = range(2), range(4)
    return np.concatenate([
        _head_cols(o_qa, PAIR_ORDER), _head_cols(o_ka, nat2),
        _head_cols(o_qb, nat4), _head_cols(o_kb, nat4),
        _head_cols(o_qc, PAIR_ORDER), _head_cols(o_kc, nat2),
        _head_cols(o_qc, PAIR_ORDER, swap=True), _head_cols(o_kc, nat2, swap=True),
        _head_cols(o_va, nat2), _head_cols(o_vb, nat4), _head_cols(o_vc, nat2),
    ])


def _rope_tables():
    nf = HEAD_DIM // 4
    inv = ROPE_THETA ** (-jnp.arange(nf, dtype=F32) / nf)
    pos = jnp.arange(SEQ)
    ang_r = (pos // GRID_W).astype(F32)[:, None] * inv[None, :]
    ang_c = (pos % GRID_W).astype(F32)[:, None] * inv[None, :]
    cos = jnp.concatenate([jnp.cos(ang_r)] * 2 + [jnp.cos(ang_c)] * 2, axis=-1)
    sin = jnp.concatenate([-jnp.sin(ang_r), jnp.sin(ang_r), -jnp.sin(ang_c), jnp.sin(ang_c)], axis=-1)
    return jnp.concatenate([cos, cos], axis=-1), jnp.concatenate([sin, sin], axis=-1)


def _layer_params(l, w_in, ln1, qk_gain, sink, rpb, out_gain, w_out, ln2, w_rg, b_rg, w_re, b_re):
    cols = _proj_columns()
    w_ext = w_in[l][:, cols].astype(BF16)
    qs = HEAD_DIM ** -0.5 * LOG2E
    g = qk_gain[l].astype(F32)
    sw = _ROPE_SWAP
    gain = jnp.concatenate([
        jnp.tile(g[0, 0], A_HEADS) * qs, jnp.tile(g[0, 1], A_KV),
        jnp.tile(g[1, 0], B_HEADS) * qs, jnp.tile(g[1, 1], B_HEADS),
        jnp.tile(g[2, 0], C_HEADS) * qs, jnp.tile(g[2, 1], C_KV),
        jnp.tile(g[2, 0][sw], C_HEADS) * qs, jnp.tile(g[2, 1][sw], C_KV),
    ])[None, :]
    order = list(PAIR_ORDER)
    sink2 = sink[l].astype(F32)[jnp.asarray(order)] * LOG2E
    rows_a = _head_cols(0, PAIR_ORDER)
    rows_b = A_WIDTH + np.arange(B_WIDTH)
    rows_c = _head_cols(A_WIDTH + B_WIDTH, PAIR_ORDER)
    og = out_gain[l].astype(F32)
    wo = w_out[l]
    wr = jnp.concatenate([w_rg[l], w_re[l], jnp.zeros((D_MODEL, LANES - N_GROUPS - N_EXPERTS), F32)], axis=1)
    wr_hi = wr.astype(BF16)
    wr_lo = (wr - wr_hi.astype(F32)).astype(BF16)
    br = jnp.concatenate([b_rg[l].astype(F32), b_re[l].astype(F32),
                          jnp.zeros((LANES - N_GROUPS - N_EXPERTS,), F32)])[None, :]
    return dict(
        w_ext=w_ext, ln1=ln1[l][None, :], gain=gain, sink2=sink2, nbr_bias=_nbr_bias_table(rpb[l]),
        ga=og[rows_a][None, :], gb=og[rows_b][None, :], gc=og[rows_c][None, :],
        wa=wo[rows_a].astype(BF16), wb=wo[rows_b].astype(BF16), wc=wo[rows_c].astype(BF16),
        ln2=ln2[l][None, :], wr=jnp.concatenate([wr_hi, wr_lo], axis=1), br=br,
    )


def _block_ones():
    idx = np.arange(MXU_DIM) // HEAD_DIM
    return jnp.asarray((idx[:, None] == idx[None, :]).astype(np.float32)).astype(BF16)


def _trunk(x, params, shared, moe_w):
    b = x.shape[0]
    x2d = x.reshape(b * SEQ, D_MODEL)
    for l, lp in enumerate(params):
        qa, ka, va, qb, kb, vb, qc, kc, vc = _in_proj(
            x2d, lp["w_ext"], lp["ln1"], lp["gain"], shared["cos"], shared["sin"], shared["ones"])
        seq = lambda z: z.reshape(b, SEQ, z.shape[-1])
        oa = _win_attn(seq(qa), seq(ka), seq(va), shared["win_bias"], lp["sink2"])
        ob = _nbr_attn(seq(qb), seq(kb), seq(vb), lp["nbr_bias"])
        oc = _dense_attn(seq(qc), seq(kc), seq(vc))
        flat = lambda z: z.reshape(b * SEQ, z.shape[-1])
        xn, hrow = _out_proj(flat(oa), flat(ob), flat(oc), x2d, lp["ga"], lp["gb"], lp["gc"],
                             lp["wa"], lp["wb"], lp["wc"], lp["ln2"], lp["wr"], lp["br"])
        wg, wu, wd = moe_w[l]
        x2d = _moe(hrow, xn, wg, wu, wd)
    return x2d.reshape(b, SEQ, D_MODEL)


def kernel(x_prompt, x_sample, ln1, w_in, qk_gain, sink, rpb, out_gain, w_out, ln2, w_router_group,
           b_router_group, w_router_expert, b_router_expert, w_gate, w_up, w_down):
    depth = w_in.shape[0]
    params = [_layer_params(l, w_in, ln1, qk_gain, sink, rpb, out_gain, w_out, ln2, w_router_group,
                            b_router_group, w_router_expert, b_router_expert) for l in range(depth)]
    cos_t, sin_t = _rope_tables()
    shared = dict(cos=cos_t, sin=sin_t, ones=_block_ones(), win_bias=_win_bias_table())
    moe_w = [(w_gate[l].astype(BF16), w_up[l].astype(BF16), w_down[l].astype(BF16)) for l in range(depth)]
    y_prompt = _trunk(x_prompt, params, shared, moe_w)
    y_sample = _trunk(x_sample, params, shared, moe_w)
    return (y_prompt, y_sample)
```

```python
import functools
import math

import numpy as np
import jax
import jax.numpy as jnp
from jax import lax
from jax.experimental import pallas as pl
from jax.experimental.pallas import tpu as pltpu

D_MODEL = 1024
SEQ = 4096
HEAD_DIM = 64
GRID_W = 64
GRID_ROWS = SEQ // GRID_W
A_HEADS, A_KV, A_WINDOW = 6, 2, 128
B_HEADS, B_ROWS, B_COLS = 4, 8, 16
C_HEADS, C_KV = 6, 2
ROPE_THETA = 10000.0
N_GROUPS, EXPERTS_PER_GROUP = 4, 4
N_EXPERTS = N_GROUPS * EXPERTS_PER_GROUP
D_EXPERT = D_MODEL // 2
EPS = 1e-6
NEG = -1e30
LOG2E = math.log2(math.e)

A_WIDTH = A_HEADS * HEAD_DIM
B_WIDTH = B_HEADS * HEAD_DIM
C_WIDTH = C_HEADS * HEAD_DIM
KV_WIDTH = A_KV * HEAD_DIM

LANES = 128
MXU_DIM = 256
VMEM_LIMIT = 56 * 1024 * 1024

PAIR_ORDER = (0, 3, 1, 4, 2, 5)

TM_PROJ = 512
TQ_A = 256
KW_A = 512
TQ_B = 256
KW_B = 768
TQ_C = 256
KC_C = 512
TM_MOE = 256
TM_COMB = 512

PAIR_A = (0, 0, 0, 1, 1, 2)
PAIR_B = (1, 2, 3, 2, 3, 3)
PAIRS_PER_GROUP = len(PAIR_A)
N_CLASSES = N_GROUPS * PAIRS_PER_GROUP
CLASS_LANE = N_GROUPS + N_EXPERTS
SUBLANES = 8
GATE_ROW = D_MODEL // 2 // LANES
GATHER_SLOTS = 3

BF16 = jnp.bfloat16
F32 = jnp.float32

_ROPE_SWAP = np.concatenate([np.arange(16, 32), np.arange(0, 16), np.arange(48, 64), np.arange(32, 48)])


def _cparams(sem):
    return pltpu.CompilerParams(dimension_semantics=sem, vmem_limit_bytes=VMEM_LIMIT)


N_NORMED = 1536
N_GAINED = 2048
N_PROJ = 2560


def _in_proj_kernel(x_ref, w_ref, ln_ref, gain_ref, cos_ref, sin_ref, ones_ref,
                    qa_ref, ka_ref, va_ref, qb_ref, kb_ref, vb_ref, qc_ref, kc_ref, vc_ref):
    x = x_ref[...]
    ms = jnp.mean(x * x, axis=-1, keepdims=True)
    h = (x * lax.rsqrt(ms + EPS) * ln_ref[...]).astype(BF16)
    p = jnp.dot(h, w_ref[...], preferred_element_type=F32)
    g = gain_ref[...]
    ones_blk = ones_ref[...]

    def chunk(c):
        return p[:, MXU_DIM * c:MXU_DIM * (c + 1)]

    def inv_rms(pc):
        ss = jnp.dot((pc * pc).astype(BF16), ones_blk, preferred_element_type=F32)
        return lax.rsqrt(ss * (1.0 / HEAD_DIM) + EPS)

    r = [inv_rms(chunk(c)) for c in range(N_NORMED // MXU_DIM)]
    pn = [chunk(c) * r[c] * g[:, MXU_DIM * c:MXU_DIM * (c + 1)] for c in range(N_NORMED // MXU_DIM)]
    qa_ref[:, 0:256] = pn[0].astype(BF16)
    qa_ref[:, 256:384] = pn[1][:, :128].astype(BF16)
    ka_ref[...] = pn[1][:, 128:].astype(BF16)
    qb_ref[...] = pn[2].astype(BF16)
    kb_ref[...] = pn[3].astype(BF16)
    cos = cos_ref[...]
    sin = sin_ref[...]
    cos2 = jnp.concatenate([cos, cos], axis=1)
    sin2 = jnp.concatenate([sin, sin], axis=1)
    sw4 = chunk(6) * r[4] * g[:, 1536:1792]
    sw5 = chunk(7) * r[5] * g[:, 1792:2048]
    c4 = pn[4] * cos2 + sw4 * sin2
    c5 = pn[5] * cos2 + sw5 * sin2
    qc_ref[:, 0:256] = c4.astype(BF16)
    qc_ref[:, 256:384] = c5[:, :128].astype(BF16)
    kc_ref[...] = c5[:, 128:].astype(BF16)
    va_ref[...] = p[:, 2048:2176].astype(BF16)
    vb_ref[...] = p[:, 2176:2432].astype(BF16)
    vc_ref[:, 0:128] = p[:, 2432:2560].astype(BF16)
    vc_ref[:, 128:256] = jnp.ones((x.shape[0], 128), BF16)


def _in_proj(x2d, w_ext, ln, gain, cos_t, sin_t, ones_blk):
    n = x2d.shape[0]
    tm = TM_PROJ
    tiles_per_seq = SEQ // tm
    row = lambda i: (i, 0)
    fixed = lambda i: (0, 0)
    pos = lambda i: (i % tiles_per_seq, 0)
    widths = (A_WIDTH, KV_WIDTH, KV_WIDTH, B_WIDTH, B_WIDTH, B_WIDTH, C_WIDTH, KV_WIDTH, 2 * KV_WIDTH)
    return pl.pallas_call(
        _in_proj_kernel,
        grid=(n // tm,),
        in_specs=[
            pl.BlockSpec((tm, D_MODEL), row),
            pl.BlockSpec((D_MODEL, N_PROJ), fixed),
            pl.BlockSpec((1, D_MODEL), fixed),
            pl.BlockSpec((1, N_GAINED), fixed),
            pl.BlockSpec((tm, LANES), pos),
            pl.BlockSpec((tm, LANES), pos),
            pl.BlockSpec((MXU_DIM, MXU_DIM), fixed),
        ],
        out_specs=[pl.BlockSpec((tm, w), row) for w in widths],
        out_shape=[jax.ShapeDtypeStruct((n, w), BF16) for w in widths],
        compiler_params=_cparams(("parallel",)),
        name="in_proj",
    )(x2d, w_ext, ln, gain, cos_t, sin_t, ones_blk)


def _half_masks():
    lane = lax.broadcasted_iota(jnp.int32, (1, LANES), 1)
    lo = lane < HEAD_DIM
    return lo, lo.astype(BF16), (~lo).astype(BF16)


def _win_attn_kernel(sink_ref, q_ref, k_ref, v_ref, bias_ref, o_ref):
    j = pl.program_id(1)
    start = pl.multiple_of(jnp.clip(j * TQ_A - A_WINDOW, 0, SEQ - KW_A), 128)
    kw = k_ref[0, pl.ds(start, KW_A), :]
    vw = v_ref[0, pl.ds(start, KW_A), :]
    lo, m_lo, m_hi = _half_masks()
    for p in range(A_HEADS // 2):
        qblk = q_ref[0, :, LANES * p:LANES * (p + 1)]
        outs = []
        for half in range(2):
            hidx = 2 * p + half
            qm = qblk * (m_lo if half == 0 else m_hi)
            s = lax.dot_general(qm, kw, (((1,), (1,)), ((), ())), preferred_element_type=F32)
            s = s + bias_ref[0, hidx]
            sk = sink_ref[hidx]
            m = jnp.maximum(jnp.max(s, axis=-1, keepdims=True), sk)
            e = jnp.exp2(s - m)
            l = jnp.sum(e, axis=-1, keepdims=True) + jnp.exp2(sk - m)
            o2 = jnp.dot(e.astype(BF16), vw, preferred_element_type=F32)
            outs.append(o2 / l)
        o_ref[0, :, LANES * p:LANES * (p + 1)] = jnp.where(lo, outs[0], outs[1]).astype(BF16)


def _win_attn(qa, ka, va, bias, sink2):
    b = qa.shape[0]
    nq = SEQ // TQ_A
    variant = lambda bi, j: (jnp.where(j == 0, 0, jnp.where(j == nq - 1, 2, 1)), 0, 0, 0)
    return pl.pallas_call(
        _win_attn_kernel,
        grid=(b, nq),
        in_specs=[
            pl.BlockSpec(memory_space=pltpu.SMEM),
            pl.BlockSpec((1, TQ_A, A_WIDTH), lambda bi, j: (bi, j, 0)),
            pl.BlockSpec((1, SEQ, KV_WIDTH), lambda bi, j: (bi, 0, 0)),
            pl.BlockSpec((1, SEQ, KV_WIDTH), lambda bi, j: (bi, 0, 0)),
            pl.BlockSpec((1, A_HEADS, TQ_A, KW_A), variant),
        ],
        out_specs=pl.BlockSpec((1, TQ_A, A_WIDTH), lambda bi, j: (bi, j, 0)),
        out_shape=jax.ShapeDtypeStruct((b, SEQ, A_WIDTH), BF16),
        compiler_params=_cparams(("parallel", "arbitrary")),
        name="win_attn",
    )(sink2, qa, ka, va, bias)


def _win_bias_table():
    slopes = np.array([2.0 ** (-8.0 * (n + 1) / A_HEADS) for n in range(A_HEADS)], np.float32)[list(PAIR_ORDER)]
    i = np.arange(TQ_A)[:, None]
    jj = np.arange(KW_A)[None, :]
    tabs = []
    for off in (0, A_WINDOW, KW_A - TQ_A):
        dist = np.abs(off + i - jj).astype(np.float32)
        tab = np.where(dist[None] <= A_WINDOW, -slopes[:, None, None] * dist[None] * LOG2E, NEG)
        tabs.append(tab)
    return jnp.asarray(np.stack(tabs).astype(np.float32))


def _nbr_attn_kernel(q_ref, k_ref, v_ref, bias_ref, o_ref):
    j = pl.program_id(1)
    rows_per_tile = TQ_B // GRID_W
    krow0 = jnp.clip(j * rows_per_tile - B_ROWS // 2, 0, GRID_ROWS - KW_B // GRID_W)
    start = pl.multiple_of(krow0 * GRID_W, 256)
    lo, m_lo, m_hi = _half_masks()
    for p in range(B_HEADS // 2):
        qblk = q_ref[0, :, LANES * p:LANES * (p + 1)]
        kw = k_ref[0, pl.ds(start, KW_B), LANES * p:LANES * (p + 1)]
        vw = v_ref[0, pl.ds(start, KW_B), LANES * p:LANES * (p + 1)]
        outs = []
        for half in range(2):
            hidx = 2 * p + half
            qm = qblk * (m_lo if half == 0 else m_hi)
            s = lax.dot_general(qm, kw, (((1,), (1,)), ((), ())), preferred_element_type=F32)
            s = s + bias_ref[0, hidx]
            m = jnp.max(s, axis=-1, keepdims=True)
            e = jnp.exp2(s - m)
            l = jnp.sum(e, axis=-1, keepdims=True)
            o2 = jnp.dot(e.astype(BF16), vw, preferred_element_type=F32)
            outs.append(o2 / l)
        o_ref[0, :, LANES * p:LANES * (p + 1)] = jnp.where(lo, outs[0], outs[1]).astype(BF16)


def _nbr_attn(qb, kb, vb, bias):
    b = qb.shape[0]
    nq = SEQ // TQ_B
    variant = lambda bi, j: (jnp.where(j == 0, 0, jnp.where(j == nq - 1, 2, 1)), 0, 0, 0)
    return pl.pallas_call(
        _nbr_attn_kernel,
        grid=(b, nq),
        in_specs=[
            pl.BlockSpec((1, TQ_B, B_WIDTH), lambda bi, j: (bi, j, 0)),
            pl.BlockSpec((1, SEQ, B_WIDTH), lambda bi, j: (bi, 0, 0)),
            pl.BlockSpec((1, SEQ, B_WIDTH), lambda bi, j: (bi, 0, 0)),
            pl.BlockSpec((1, B_HEADS, TQ_B, KW_B), variant),
        ],
        out_specs=pl.BlockSpec((1, TQ_B, B_WIDTH), lambda bi, j: (bi, j, 0)),
        out_shape=jax.ShapeDtypeStruct((b, SEQ, B_WIDTH), BF16),
        compiler_params=_cparams(("parallel", "arbitrary")),
        name="nbr_attn",
    )(qb, kb, vb, bias)


def _nbr_bias_table(rpb):
    rows_per_tile = TQ_B // GRID_W
    krows = KW_B // GRID_W
    r = rpb.astype(F32) * LOG2E
    edge = GRID_W - B_COLS
    ext = jnp.concatenate([jnp.repeat(r[..., :1], edge, axis=-1), r, jnp.repeat(r[..., -1:], edge, axis=-1)], axis=-1)
    col = jnp.stack([ext[..., GRID_W - 1 - q:2 * GRID_W - 1 - q] for q in range(GRID_W)], axis=2)
    tabs = []
    for r_first in (0, rows_per_tile, GRID_ROWS - rows_per_tile):
        krow0 = int(np.clip(r_first - B_ROWS // 2, 0, GRID_ROWS - krows))
        slabs = []
        for ql in range(rows_per_tile):
            per_k = [col[:, int(np.clip(krow0 + kl - (r_first + ql) + B_ROWS - 1, 0, 2 * B_ROWS - 2))]
                     for kl in range(krows)]
            slabs.append(jnp.stack(per_k, axis=2))
        tab = jnp.stack(slabs, axis=1).reshape(B_HEADS, TQ_B, KW_B)
        qi = np.arange(TQ_B)
        kj = np.arange(KW_B)
        qr = (r_first + qi // GRID_W)[:, None]
        qcol = (qi % GRID_W)[:, None]
        kr = (krow0 + kj // GRID_W)[None, :]
        kcol = (kj % GRID_W)[None, :]
        r0 = np.clip(qr - B_ROWS // 2, 0, GRID_ROWS - B_ROWS)
        c0 = np.clip(qcol - B_COLS // 2, 0, GRID_W - B_COLS)
        valid = (kr >= r0) & (kr < r0 + B_ROWS) & (kcol >= c0) & (kcol < c0 + B_COLS)
        tabs.append(jnp.where(jnp.asarray(valid)[None], tab, NEG))
    return jnp.stack(tabs)


def _dense_attn_kernel(q_ref, k_ref, v_ref, o_ref, s_ref):
    lo, m_lo, m_hi = _half_masks()
    nchunk = SEQ // KC_C
    npairs = C_HEADS // 2

    def scores(p):
        qblk = q_ref[0, :, LANES * p:LANES * (p + 1)]
        q2 = jnp.concatenate([qblk * m_lo, qblk * m_hi], axis=0)
        m_run = None
        for c in range(nchunk):
            kc = k_ref[0, KC_C * c:KC_C * (c + 1), :]
            s = lax.dot_general(q2, kc, (((1,), (1,)), ((), ())), preferred_element_type=F32)
            s_ref[p % 2, :, KC_C * c:KC_C * (c + 1)] = s
            for t in range(KC_C // LANES):
                blk = s[:, LANES * t:LANES * (t + 1)]
                m_run = blk if m_run is None else jnp.maximum(m_run, blk)
        return jnp.max(m_run, axis=-1, keepdims=True)

    def weighted_values(p, m):
        acc = None
        for c in range(nchunk):
            e = jnp.exp2(s_ref[p % 2, :, KC_C * c:KC_C * (c + 1)] - m).astype(BF16)
            part = jnp.dot(e, v_ref[0, KC_C * c:KC_C * (c + 1), :], preferred_element_type=F32)
            acc = part if acc is None else acc + part
        on = acc[:, :LANES] / acc[:, LANES:]
        o_ref[0, :, LANES * p:LANES * (p + 1)] = jnp.where(lo, on[:TQ_C], on[TQ_C:]).astype(BF16)

    m_next = scores(0)
    for p in range(npairs):
        m_cur = m_next
        if p + 1 < npairs:
            m_next = scores(p + 1)
        weighted_values(p, m_cur)


def _dense_attn(qc, kc, vc):
    b = qc.shape[0]
    return pl.pallas_call(
        _dense_attn_kernel,
        grid=(b, SEQ // TQ_C),
        in_specs=[
            pl.BlockSpec((1, TQ_C, C_WIDTH), lambda bi, j: (bi, j, 0)),
            pl.BlockSpec((1, SEQ, KV_WIDTH), lambda bi, j: (bi, 0, 0)),
            pl.BlockSpec((1, SEQ, 2 * KV_WIDTH), lambda bi, j: (bi, 0, 0)),
        ],
        out_specs=pl.BlockSpec((1, TQ_C, C_WIDTH), lambda bi, j: (bi, j, 0)),
        out_shape=jax.ShapeDtypeStruct((b, SEQ, C_WIDTH), BF16),
        scratch_shapes=[pltpu.VMEM((2, 2 * TQ_C, SEQ), F32)],
        compiler_params=_cparams(("parallel", "arbitrary")),
        name="dense_attn",
    )(qc, kc, vc)


def _out_proj_kernel(oa_ref, ob_ref, oc_ref, x_ref, ga_ref, gb_ref, gc_ref, wa_ref, wb_ref, wc_ref,
                     ln_ref, wr_ref, br_ref, xn_ref, h_ref):
    def nrm(o_ref, g_ref):
        o = o_ref[...].astype(F32)
        ms = jnp.mean(o * o, axis=-1, keepdims=True)
        return (o * lax.rsqrt(ms + EPS) * g_ref[...]).astype(BF16)

    acc = jnp.dot(nrm(oa_ref, ga_ref), wa_ref[...], preferred_element_type=F32)
    acc = acc + jnp.dot(nrm(ob_ref, gb_ref), wb_ref[...], preferred_element_type=F32)
    acc = acc + jnp.dot(nrm(oc_ref, gc_ref), wc_ref[...], preferred_element_type=F32)
    xn = x_ref[...] + acc
    xn_ref[...] = xn
    ms = jnp.mean(xn * xn, axis=-1, keepdims=True)
    h2 = xn * lax.rsqrt(ms + EPS) * ln_ref[...]
    hi = h2.astype(BF16)
    lo = (h2 - hi.astype(F32)).astype(BF16)
    tm = xn.shape[0]
    half = D_MODEL // 2
    bits_lo = lax.bitcast_convert_type(hi[:, :half].astype(F32), jnp.uint32)
    bits_hi = lax.bitcast_convert_type(hi[:, half:].astype(F32), jnp.uint32)
    words = (bits_hi & jnp.uint32(0xFFFF0000)) | (bits_lo >> 16)
    for s in range(GATE_ROW):
        h_ref[pl.ds(s, tm, stride=SUBLANES), :] = words[:, LANES * s:LANES * (s + 1)]
    for s in range(GATE_ROW + 1, SUBLANES):
        h_ref[pl.ds(s, tm, stride=SUBLANES), :] = jnp.zeros((tm, LANES), jnp.uint32)
    wr = wr_ref[...]
    t = jnp.dot(hi, wr, preferred_element_type=F32)
    u = jnp.dot(lo, wr[:, :LANES], preferred_element_type=F32)
    logits = t[:, :LANES] + t[:, LANES:] + u + br_ref[...]

    lane = lax.broadcasted_iota(jnp.int32, logits.shape, 1).astype(F32)
    big = jnp.float32(3.0e38)
    is_g = lane < N_GROUPS
    gl = jnp.where(is_g, logits, -big)
    mg = jnp.max(gl, axis=-1, keepdims=True)
    grp = jnp.min(jnp.where(gl == mg, lane, big), axis=-1, keepdims=True)
    pg = 1.0 / jnp.sum(jnp.where(is_g, jnp.exp(gl - mg), 0.0), axis=-1, keepdims=True)
    e_lo = N_GROUPS + EXPERTS_PER_GROUP * grp
    sel = (lane >= e_lo) & (lane < e_lo + EXPERTS_PER_GROUP)
    el = jnp.where(sel, logits, -big)
    v1 = jnp.max(el, axis=-1, keepdims=True)
    i1 = jnp.min(jnp.where(el == v1, lane, big), axis=-1, keepdims=True)
    el2 = jnp.where(lane == i1, -big, el)
    v2 = jnp.max(el2, axis=-1, keepdims=True)
    i2 = jnp.min(jnp.where(el2 == v2, lane, big), axis=-1, keepdims=True)
    e21 = jnp.exp(v2 - v1)
    w1 = pg / (1.0 + e21)
    w2 = pg * e21 / (1.0 + e21)
    a = jnp.minimum(i1, i2) - e_lo
    b = jnp.maximum(i1, i2) - e_lo
    cls = grp * PAIRS_PER_GROUP + (a * (7.0 - a) * 0.5 + (b - a - 1.0))
    gates = jnp.where(lane == i1, w1, jnp.where(lane == i2, w2, jnp.where(lane == CLASS_LANE, cls, 0.0)))
    h_ref[pl.ds(GATE_ROW, tm, stride=SUBLANES), :] = lax.bitcast_convert_type(gates, jnp.uint32)


def _out_proj(oa, ob, oc, x2d, ga, gb, gc, wa, wb, wc, ln2, wr, br):
    n = x2d.shape[0]
    tm = TM_PROJ
    row = lambda i: (i, 0)
    fixed = lambda i: (0, 0)
    return pl.pallas_call(
        _out_proj_kernel,
        grid=(n // tm,),
        in_specs=[
            pl.BlockSpec((tm, A_WIDTH), row),
            pl.BlockSpec((tm, B_WIDTH), row),
            pl.BlockSpec((tm, C_WIDTH), row),
            pl.BlockSpec((tm, D_MODEL), row),
            pl.BlockSpec((1, A_WIDTH), fixed),
            pl.BlockSpec((1, B_WIDTH), fixed),
            pl.BlockSpec((1, C_WIDTH), fixed),
            pl.BlockSpec((A_WIDTH, D_MODEL), fixed),
            pl.BlockSpec((B_WIDTH, D_MODEL), fixed),
            pl.BlockSpec((C_WIDTH, D_MODEL), fixed),
            pl.BlockSpec((1, D_MODEL), fixed),
            pl.BlockSpec((D_MODEL, 2 * LANES), fixed),
            pl.BlockSpec((1, LANES), fixed),
        ],
        out_specs=[
            pl.BlockSpec((tm, D_MODEL), row),
            pl.BlockSpec((tm * SUBLANES, LANES), row),
        ],
        out_shape=[
            jax.ShapeDtypeStruct((n, D_MODEL), F32),
            jax.ShapeDtypeStruct((n * SUBLANES, LANES), jnp.uint32),
        ],
        compiler_params=_cparams(("parallel",)),
        name="out_proj_router",
    )(oa, ob, oc, x2d, ga, gb, gc, wa, wb, wc, ln2, wr, br)


def _tile_copy(src_hbm, buf, sem, slot, src_row8, dst_tok):
    src = src_hbm.at[pl.ds(pl.multiple_of(src_row8, SUBLANES), SUBLANES), :]
    return pltpu.make_async_copy(src, buf[slot].at[pl.ds(SUBLANES * dst_tok, SUBLANES), :], sem.at[slot])


def _issue_gather(idx_ref, src_hbm, buf, sem, slot, toks):
    for r in range(toks):
        _tile_copy(src_hbm, buf, sem, slot, idx_ref[0, 0, r], r).start()


def _wait_gather(src_hbm, buf, sem, slot, toks):
    pltpu.make_async_copy(src_hbm.at[pl.ds(0, SUBLANES * toks), :], buf[slot], sem.at[slot]).wait()


def _tile_row(ref, s, toks):
    return ref[pl.ds(s, toks, stride=SUBLANES), :]


def _moe_kernel(ea_ref, eb_ref, nvalid_ref, idx0_ref, idx1_ref, idxn_ref, h_hbm,
                wga_ref, wua_ref, wda_ref, wgb_ref, wub_ref, wdb_ref, y_ref,
                buf0, buf1, buf2, sem, x_ref, g_ref, act_ref, ge_ref, yacc_ref):
    buf = (buf0, buf1, buf2)
    tm = TM_MOE
    i = pl.program_id(0)
    nvalid = nvalid_ref[0]

    @pl.when(i == 0)
    def _():
        _issue_gather(idx0_ref, h_hbm, buf, sem, 0, tm)
        _issue_gather(idx1_ref, h_hbm, buf, sem, 1, tm)

    def tile_body(slot):
        cur = buf[slot]
        nxt = (slot + 2) % GATHER_SLOTS
        n_batches = 6
        per_batch = -(-tm // n_batches)

        def prefetch_batch(k):
            for r in range(k * per_batch, min((k + 1) * per_batch, tm)):
                _tile_copy(h_hbm, buf, sem, nxt, idxn_ref[0, 0, r], r).start()

        def stage(k):
            return pl.when(nvalid > -(k + 1))

        _wait_gather(h_hbm, buf, sem, slot, tm)
        parts_lo, parts_hi = [], []
        for s in range(D_MODEL // 2 // LANES):
            w = _tile_row(cur, s, tm)
            parts_lo.append(lax.bitcast_convert_type(w << 16, F32).astype(BF16))
            parts_hi.append(lax.bitcast_convert_type(w & jnp.uint32(0xFFFF0000), F32).astype(BF16))
        x_ref[...] = jnp.concatenate(parts_lo + parts_hi, axis=1)
        gates = lax.bitcast_convert_type(_tile_row(cur, GATE_ROW, tm), F32)
        lane = lax.broadcasted_iota(jnp.int32, gates.shape, 1)
        for which, e in enumerate((ea_ref[i], eb_ref[i])):
            ge = jnp.sum(jnp.where(lane == e + N_GROUPS, gates, 0.0), axis=-1, keepdims=True)
            ge_ref[which] = jnp.broadcast_to(ge, (tm, LANES))

        def gate_proj(k, wg_ref):
            @stage(k)
            def _():
                g_ref[...] = jnp.dot(x_ref[...], wg_ref[0], preferred_element_type=F32)
                prefetch_batch(k)

        def up_proj_act(k, wu_ref):
            @stage(k)
            def _():
                u = jnp.dot(x_ref[...], wu_ref[0], preferred_element_type=F32)
                g = g_ref[...]
                act_ref[...] = (g / (1.0 + jnp.exp(-g)) * u).astype(BF16)
                prefetch_batch(k)

        def down_proj(k, wd_ref, which):
            @stage(k)
            def _():
                y = jnp.dot(act_ref[...], wd_ref[0], preferred_element_type=F32)
                y = y * jnp.tile(ge_ref[which], (1, D_MODEL // LANES))
                if which == 0:
                    yacc_ref[...] = y
                else:
                    y = y + yacc_ref[...]
                    for s in range(SUBLANES):
                        y_ref[pl.ds(s, tm, stride=SUBLANES), :] = y[:, LANES * s:LANES * (s + 1)]
                prefetch_batch(k)

        gate_proj(0, wga_ref)
        up_proj_act(1, wua_ref)
        down_proj(2, wda_ref, 0)
        gate_proj(3, wgb_ref)
        up_proj_act(4, wub_ref)
        down_proj(5, wdb_ref, 1)

        @pl.when(i == nvalid - 1)
        def _():
            _wait_gather(h_hbm, buf, sem, (slot + 1) % GATHER_SLOTS, tm)
            _wait_gather(h_hbm, buf, sem, (slot + 2) % GATHER_SLOTS, tm)

    for slot in range(GATHER_SLOTS):
        @pl.when((i < nvalid) & (i % GATHER_SLOTS == slot))
        def _(slot=slot):
            tile_body(slot)

    @pl.when(i >= nvalid)
    def _():
        y_ref[...] = jnp.zeros(y_ref.shape, F32)


def _moe_experts(hrow, ea, eb, nvalid, src_idx, wg, wu, wd):
    tm = TM_MOE
    nt = ea.shape[0]
    smem_blk = lambda f: pl.BlockSpec((1, 1, tm), f, memory_space=pltpu.SMEM)
    w_in_spec = lambda sel: pl.BlockSpec((1, D_MODEL, D_EXPERT), sel)
    w_out_spec = lambda sel: pl.BlockSpec((1, D_EXPERT, D_MODEL), sel)
    sel_a = lambda i, ea, eb, nv: (ea[i], 0, 0)
    sel_b = lambda i, ea, eb, nv: (eb[i], 0, 0)
    grid_spec = pltpu.PrefetchScalarGridSpec(
        num_scalar_prefetch=3,
        grid=(nt,),
        in_specs=[
            smem_blk(lambda i, ea, eb, nv: (0, 0, 0)),
            smem_blk(lambda i, ea, eb, nv: (1, 0, 0)),
            smem_blk(lambda i, ea, eb, nv: (i + 2, 0, 0)),
            pl.BlockSpec(memory_space=pl.ANY),
            w_in_spec(sel_a), w_in_spec(sel_a), w_out_spec(sel_a),
            w_in_spec(sel_b), w_in_spec(sel_b), w_out_spec(sel_b),
        ],
        out_specs=pl.BlockSpec((tm * SUBLANES, LANES), lambda i, ea, eb, nv: (i, 0)),
        scratch_shapes=[pltpu.VMEM((tm * SUBLANES, LANES), jnp.uint32) for _ in range(GATHER_SLOTS)]
        + [pltpu.SemaphoreType.DMA((GATHER_SLOTS,)),
           pltpu.VMEM((tm, D_MODEL), BF16),
           pltpu.VMEM((tm, D_EXPERT), F32),
           pltpu.VMEM((tm, D_EXPERT), BF16),
           pltpu.VMEM((2, tm, LANES), F32),
           pltpu.VMEM((tm, D_MODEL), F32)],
    )
    return pl.pallas_call(
        _moe_kernel,
        grid_spec=grid_spec,
        out_shape=jax.ShapeDtypeStruct((nt * tm * SUBLANES, LANES), F32),
        compiler_params=_cparams(("arbitrary",)),
        name="moe_experts",
    )(ea, eb, nvalid, src_idx, src_idx, src_idx, hrow, wg, wu, wd, wg, wu, wd)


def _combine_kernel(pos0_ref, pos1_ref, posn_ref, xn_ref, y_hbm, o_ref, buf0, buf1, buf2, sem):
    buf = (buf0, buf1, buf2)
    tm = TM_COMB
    i = pl.program_id(0)

    @pl.when(i == 0)
    def _():
        _issue_gather(pos0_ref, y_hbm, buf, sem, 0, tm)
        _issue_gather(pos1_ref, y_hbm, buf, sem, 1, tm)

    for slot in range(GATHER_SLOTS):
        @pl.when(i % GATHER_SLOTS == slot)
        def _(slot=slot):
            _wait_gather(y_hbm, buf, sem, slot, tm)
            _issue_gather(posn_ref, y_hbm, buf, sem, (slot + 2) % GATHER_SLOTS, tm)
            for s in range(SUBLANES):
                cols = slice(LANES * s, LANES * (s + 1))
                o_ref[:, cols] = xn_ref[:, cols] + _tile_row(buf[slot], s, tm)

            @pl.when(i == pl.num_programs(0) - 1)
            def _():
                _wait_gather(y_hbm, buf, sem, (slot + 1) % GATHER_SLOTS, tm)
                _wait_gather(y_hbm, buf, sem, (slot + 2) % GATHER_SLOTS, tm)


def _combine(pos_tiles, xn, y_sorted):
    n = xn.shape[0]
    tm = TM_COMB
    smem_blk = lambda f: pl.BlockSpec((1, 1, tm), f, memory_space=pltpu.SMEM)
    return pl.pallas_call(
        _combine_kernel,
        grid=(n // tm,),
        in_specs=[
            smem_blk(lambda i: (0, 0, 0)),
            smem_blk(lambda i: (1, 0, 0)),
            smem_blk(lambda i: (i + 2, 0, 0)),
            pl.BlockSpec((tm, D_MODEL), lambda i: (i, 0)),
            pl.BlockSpec(memory_space=pl.ANY),
        ],
        out_specs=pl.BlockSpec((tm, D_MODEL), lambda i: (i, 0)),
        out_shape=jax.ShapeDtypeStruct((n, D_MODEL), F32),
        scratch_shapes=[pltpu.VMEM((tm * SUBLANES, LANES), F32) for _ in range(GATHER_SLOTS)]
        + [pltpu.SemaphoreType.DMA((GATHER_SLOTS,))],
        compiler_params=_cparams(("arbitrary",)),
        name="moe_combine",
    )(pos_tiles, pos_tiles, pos_tiles, xn, y_sorted)


def _moe_plan(cls, n):
    tm = TM_MOE
    nt = n // tm + N_CLASSES
    onehot = (cls[:, None] == jnp.arange(N_CLASSES, dtype=jnp.int32)[None, :]).astype(jnp.int32)
    counts = jnp.sum(onehot, axis=0)
    rank = jnp.sum(jnp.cumsum(onehot, axis=0) * onehot, axis=1) - 1
    tiles = (counts + tm - 1) // tm
    tile_end = jnp.cumsum(tiles)
    tile_start = tile_end - tiles
    pos = jnp.sum(onehot * tile_start[None, :], axis=1) * tm + rank
    pad = GATHER_SLOTS - 1
    src_idx = jnp.zeros(((nt + pad) * tm,), jnp.int32).at[pos].set(
        SUBLANES * jnp.arange(n, dtype=jnp.int32), unique_indices=True)
    nvalid = tile_end[-1]
    tile_id = jnp.minimum(jnp.arange(nt, dtype=jnp.int32), nvalid - 1)
    tcls = jnp.sum((tile_id[:, None] >= tile_end[None, :]).astype(jnp.int32), axis=1)
    grp, pair = tcls // PAIRS_PER_GROUP, tcls % PAIRS_PER_GROUP
    pair_a = jnp.asarray(PAIR_A, jnp.int32)
    pair_b = jnp.asarray(PAIR_B, jnp.int32)
    ea = grp * EXPERTS_PER_GROUP + jnp.sum((pair[:, None] == jnp.arange(PAIRS_PER_GROUP)[None]) * pair_a[None], axis=1)
    eb = grp * EXPERTS_PER_GROUP + jnp.sum((pair[:, None] == jnp.arange(PAIRS_PER_GROUP)[None]) * pair_b[None], axis=1)
    pos_tiles = jnp.concatenate([SUBLANES * pos, jnp.zeros((pad * TM_COMB,), jnp.int32)])
    pos_tiles = pos_tiles.reshape(n // TM_COMB + pad, 1, TM_COMB)
    return (ea.astype(jnp.int32), eb.astype(jnp.int32), nvalid.reshape(1).astype(jnp.int32),
            src_idx.reshape(nt + pad, 1, tm), pos_tiles)


def _moe(hrow, xn, wg, wu, wd):
    n = xn.shape[0]
    gate_rows = hrow.reshape(n, SUBLANES, LANES)[:, GATE_ROW, CLASS_LANE]
    cls = lax.bitcast_convert_type(gate_rows, F32).astype(jnp.int32)
    ea, eb, nvalid, src_idx, pos_tiles = _moe_plan(cls, n)
    y_sorted = _moe_experts(hrow, ea, eb, nvalid, src_idx, wg, wu, wd)
    return _combine(pos_tiles, xn, y_sorted)


def _head_cols(base, heads, swap=False):
    inner = _ROPE_SWAP if swap else np.arange(HEAD_DIM)
    return np.concatenate([base + HEAD_DIM * h + inner for h in heads])


def _proj_columns():
    o_qa, o_ka, o_va = 0, A_WIDTH, A_WIDTH + KV_WIDTH
    o_qb = o_va + KV_WIDTH
    o_kb, o_vb = o_qb + B_WIDTH, o_qb + 2 * B_WIDTH
    o_qc = o_vb + B_WIDTH
    o_kc, o_vc = o_qc + C_WIDTH, o_qc + C_WIDTH + KV_WIDTH
    nat2, nat4 = range(2), range(4)
    return np.concatenate([
        _head_cols(o_qa, PAIR_ORDER), _head_cols(o_ka, nat2),
        _head_cols(o_qb, nat4), _head_cols(o_kb, nat4),
        _head_cols(o_qc, PAIR_ORDER), _head_cols(o_kc, nat2),
        _head_cols(o_qc, PAIR_ORDER, swap=True), _head_cols(o_kc, nat2, swap=True),
        _head_cols(o_va, nat2), _head_cols(o_vb, nat4), _head_cols(o_vc, nat2),
    ])


def _rope_tables():
    nf = HEAD_DIM // 4
    inv = ROPE_THETA ** (-jnp.arange(nf, dtype=F32) / nf)
    pos = jnp.arange(SEQ)
    ang_r = (pos // GRID_W).astype(F32)[:, None] * inv[None, :]
    ang_c = (pos % GRID_W).astype(F32)[:, None] * inv[None, :]
    cos = jnp.concatenate([jnp.cos(ang_r)] * 2 + [jnp.cos(ang_c)] * 2, axis=-1)
    sin = jnp.concatenate([-jnp.sin(ang_r), jnp.sin(ang_r), -jnp.sin(ang_c), jnp.sin(ang_c)], axis=-1)
    return jnp.concatenate([cos, cos], axis=-1), jnp.concatenate([sin, sin], axis=-1)


def _layer_params(l, w_in, ln1, qk_gain, sink, rpb, out_gain, w_out, ln2, w_rg, b_rg, w_re, b_re):
    cols = _proj_columns()
    w_ext = w_in[l][:, cols].astype(BF16)
    qs = HEAD_DIM ** -0.5 * LOG2E
    g = qk_gain[l].astype(F32)
    sw = _ROPE_SWAP
    gain = jnp.concatenate([
        jnp.tile(g[0, 0], A_HEADS) * qs, jnp.tile(g[0, 1], A_KV),
        jnp.tile(g[1, 0], B_HEADS) * qs, jnp.tile(g[1, 1], B_HEADS),
        jnp.tile(g[2, 0], C_HEADS) * qs, jnp.tile(g[2, 1], C_KV),
        jnp.tile(g[2, 0][sw], C_HEADS) * qs, jnp.tile(g[2, 1][sw], C_KV),
    ])[None, :]
    order = list(PAIR_ORDER)
    sink2 = sink[l].astype(F32)[jnp.asarray(order)] * LOG2E
    rows_a = _head_cols(0, PAIR_ORDER)
    rows_b = A_WIDTH + np.arange(B_WIDTH)
    rows_c = _head_cols(A_WIDTH + B_WIDTH, PAIR_ORDER)
    og = out_gain[l].astype(F32)
    wo = w_out[l]
    wr = jnp.concatenate([w_rg[l], w_re[l], jnp.zeros((D_MODEL, LANES - N_GROUPS - N_EXPERTS), F32)], axis=1)
    wr_hi = wr.astype(BF16)
    wr_lo = (wr - wr_hi.astype(F32)).astype(BF16)
    br = jnp.concatenate([b_rg[l].astype(F32), b_re[l].astype(F32),
                          jnp.zeros((LANES - N_GROUPS - N_EXPERTS,), F32)])[None, :]
    return dict(
        w_ext=w_ext, ln1=ln1[l][None, :], gain=gain, sink2=sink2, nbr_bias=_nbr_bias_table(rpb[l]),
        ga=og[rows_a][None, :], gb=og[rows_b][None, :], gc=og[rows_c][None, :],
        wa=wo[rows_a].astype(BF16), wb=wo[rows_b].astype(BF16), wc=wo[rows_c].astype(BF16),
        ln2=ln2[l][None, :], wr=jnp.concatenate([wr_hi, wr_lo], axis=1), br=br,
    )


def _block_ones():
    idx = np.arange(MXU_DIM) // HEAD_DIM
    return jnp.asarray((idx[:, None] == idx[None, :]).astype(np.float32)).astype(BF16)


def _trunk(x, params, shared, moe_w):
    b = x.shape[0]
    x2d = x.reshape(b * SEQ, D_MODEL)
    for l, lp in enumerate(params):
        qa, ka, va, qb, kb, vb, qc, kc, vc = _in_proj(
            x2d, lp["w_ext"], lp["ln1"], lp["gain"], shared["cos"], shared["sin"], shared["ones"])
        seq = lambda z: z.reshape(b, SEQ, z.shape[-1])
        oa = _win_attn(seq(qa), seq(ka), seq(va), shared["win_bias"], lp["sink2"])
        ob = _nbr_attn(seq(qb), seq(kb), seq(vb), lp["nbr_bias"])
        oc = _dense_attn(seq(qc), seq(kc), seq(vc))
        flat = lambda z: z.reshape(b * SEQ, z.shape[-1])
        xn, hrow = _out_proj(flat(oa), flat(ob), flat(oc), x2d, lp["ga"], lp["gb"], lp["gc"],
                             lp["wa"], lp["wb"], lp["wc"], lp["ln2"], lp["wr"], lp["br"])
        wg, wu, wd = moe_w[l]
        x2d = _moe(hrow, xn, wg, wu, wd)
    return x2d.reshape(b, SEQ, D_MODEL)


def kernel(x_prompt, x_sample, ln1, w_in, qk_gain, sink, rpb, out_gain, w_out, ln2, w_router_group,
           b_router_group, w_router_expert, b_router_expert, w_gate, w_up, w_down):
    depth = w_in.shape[0]
    params = [_layer_params(l, w_in, ln1, qk_gain, sink, rpb, out_gain, w_out, ln2, w_router_group,
                            b_router_group, w_router_expert, b_router_expert) for l in range(depth)]
    cos_t, sin_t = _rope_tables()
    shared = dict(cos=cos_t, sin=sin_t, ones=_block_ones(), win_bias=_win_bias_table())
    moe_w = [(w_gate[l].astype(BF16), w_up[l].astype(BF16), w_down[l].astype(BF16)) for l in range(depth)]
    y_prompt = _trunk(x_prompt, params, shared, moe_w)
    y_sample = _trunk(x_sample, params, shared, moe_w)
    return (y_prompt, y_sample)
```

```python
import functools
import math

import numpy as np
import jax
import jax.numpy as jnp
from jax import lax
from jax.experimental import pallas as pl
from jax.experimental.pallas import tpu as pltpu

D_MODEL = 1024
SEQ = 4096
HEAD_DIM = 64
GRID_W = 64
GRID_ROWS = SEQ // GRID_W
A_HEADS, A_KV, A_WINDOW = 6, 2, 128
B_HEADS, B_ROWS, B_COLS = 4, 8, 16
C_HEADS, C_KV = 6, 2
ROPE_THETA = 10000.0
N_GROUPS, EXPERTS_PER_GROUP = 4, 4
N_EXPERTS = N_GROUPS * EXPERTS_PER_GROUP
D_EXPERT = D_MODEL // 2
EPS = 1e-6
NEG = -1e30
LOG2E = math.log2(math.e)

A_WIDTH = A_HEADS * HEAD_DIM
B_WIDTH = B_HEADS * HEAD_DIM
C_WIDTH = C_HEADS * HEAD_DIM
KV_WIDTH = A_KV * HEAD_DIM

LANES = 128
MXU_DIM = 256
VMEM_LIMIT = 56 * 1024 * 1024

PAIR_ORDER = (0, 3, 1, 4, 2, 5)

TM_PROJ = 512
TQ_A = 256
KW_A = 512
TQ_B = 256
KW_B = 768
TQ_C = 256
KC_C = 512
TM_MOE = 256
TM_COMB = 512
TM_DISP = 512

PAIR_A = (0, 0, 0, 1, 1, 2)
PAIR_B = (1, 2, 3, 2, 3, 3)
PAIRS_PER_GROUP = len(PAIR_A)
N_CLASSES = N_GROUPS * PAIRS_PER_GROUP
CLASS_LANE = N_GROUPS + N_EXPERTS
SUBLANES = 8
GATE_ROW = D_MODEL // 2 // LANES
GATHER_SLOTS = 3

BF16 = jnp.bfloat16
F32 = jnp.float32

_ROPE_SWAP = np.concatenate([np.arange(16, 32), np.arange(0, 16), np.arange(48, 64), np.arange(32, 48)])


def _cparams(sem):
    return pltpu.CompilerParams(dimension_semantics=sem, vmem_limit_bytes=VMEM_LIMIT)


N_NORMED = 1536
N_GAINED = 2048
N_PROJ = 2560


def _in_proj_kernel(x_ref, w_ref, ln_ref, gain_ref, cos_ref, sin_ref, ones_ref,
                    qa_ref, ka_ref, va_ref, qb_ref, kb_ref, vb_ref, qc_ref, kc_ref, vc_ref):
    x = x_ref[...]
    ms = jnp.mean(x * x, axis=-1, keepdims=True)
    h = (x * lax.rsqrt(ms + EPS) * ln_ref[...]).astype(BF16)
    p = jnp.dot(h, w_ref[...], preferred_element_type=F32)
    g = gain_ref[...]
    ones_blk = ones_ref[...]

    def chunk(c):
        return p[:, MXU_DIM * c:MXU_DIM * (c + 1)]

    def inv_rms(pc):
        ss = jnp.dot((pc * pc).astype(BF16), ones_blk, preferred_element_type=F32)
        return lax.rsqrt(ss * (1.0 / HEAD_DIM) + EPS)

    r = [inv_rms(chunk(c)) for c in range(N_NORMED // MXU_DIM)]
    pn = [chunk(c) * r[c] * g[:, MXU_DIM * c:MXU_DIM * (c + 1)] for c in range(N_NORMED // MXU_DIM)]
    qa_ref[:, 0:256] = pn[0].astype(BF16)
    qa_ref[:, 256:384] = pn[1][:, :128].astype(BF16)
    ka_ref[...] = pn[1][:, 128:].astype(BF16)
    qb_ref[...] = pn[2].astype(BF16)
    kb_ref[...] = pn[3].astype(BF16)
    cos = cos_ref[...]
    sin = sin_ref[...]
    cos2 = jnp.concatenate([cos, cos], axis=1)
    sin2 = jnp.concatenate([sin, sin], axis=1)
    sw4 = chunk(6) * r[4] * g[:, 1536:1792]
    sw5 = chunk(7) * r[5] * g[:, 1792:2048]
    c4 = pn[4] * cos2 + sw4 * sin2
    c5 = pn[5] * cos2 + sw5 * sin2
    qc_ref[:, 0:256] = c4.astype(BF16)
    qc_ref[:, 256:384] = c5[:, :128].astype(BF16)
    kc_ref[...] = c5[:, 128:].astype(BF16)
    va_ref[...] = p[:, 2048:2176].astype(BF16)
    vb_ref[...] = p[:, 2176:2432].astype(BF16)
    vc_ref[:, 0:128] = p[:, 2432:2560].astype(BF16)
    vc_ref[:, 128:256] = jnp.ones((x.shape[0], 128), BF16)


def _in_proj(x2d, w_ext, ln, gain, cos_t, sin_t, ones_blk):
    n = x2d.shape[0]
    tm = TM_PROJ
    tiles_per_seq = SEQ // tm
    row = lambda i: (i, 0)
    fixed = lambda i: (0, 0)
    pos = lambda i: (i % tiles_per_seq, 0)
    widths = (A_WIDTH, KV_WIDTH, KV_WIDTH, B_WIDTH, B_WIDTH, B_WIDTH, C_WIDTH, KV_WIDTH, 2 * KV_WIDTH)
    return pl.pallas_call(
        _in_proj_kernel,
        grid=(n // tm,),
        in_specs=[
            pl.BlockSpec((tm, D_MODEL), row),
            pl.BlockSpec((D_MODEL, N_PROJ), fixed),
            pl.BlockSpec((1, D_MODEL), fixed),
            pl.BlockSpec((1, N_GAINED), fixed),
            pl.BlockSpec((tm, LANES), pos),
            pl.BlockSpec((tm, LANES), pos),
            pl.BlockSpec((MXU_DIM, MXU_DIM), fixed),
        ],
        out_specs=[pl.BlockSpec((tm, w), row) for w in widths],
        out_shape=[jax.ShapeDtypeStruct((n, w), BF16) for w in widths],
        compiler_params=_cparams(("parallel",)),
        name="in_proj",
    )(x2d, w_ext, ln, gain, cos_t, sin_t, ones_blk)


def _half_masks():
    lane = lax.broadcasted_iota(jnp.int32, (1, LANES), 1)
    lo = lane < HEAD_DIM
    return lo, lo.astype(BF16), (~lo).astype(BF16)


def _win_attn_kernel(sink_ref, q_ref, k_ref, v_ref, bias_ref, o_ref):
    j = pl.program_id(1)
    start = pl.multiple_of(jnp.clip(j * TQ_A - A_WINDOW, 0, SEQ - KW_A), 128)
    kw = k_ref[0, pl.ds(start, KW_A), :]
    vw = v_ref[0, pl.ds(start, KW_A), :]
    lo, m_lo, m_hi = _half_masks()
    for p in range(A_HEADS // 2):
        qblk = q_ref[0, :, LANES * p:LANES * (p + 1)]
        outs = []
        for half in range(2):
            hidx = 2 * p + half
            qm = qblk * (m_lo if half == 0 else m_hi)
            s = lax.dot_general(qm, kw, (((1,), (1,)), ((), ())), preferred_element_type=F32)
            s = s + bias_ref[0, hidx]
            sk = sink_ref[hidx]
            m = jnp.maximum(jnp.max(s, axis=-1, keepdims=True), sk)
            e = jnp.exp2(s - m)
            l = jnp.sum(e, axis=-1, keepdims=True) + jnp.exp2(sk - m)
            o2 = jnp.dot(e.astype(BF16), vw, preferred_element_type=F32)
            outs.append(o2 / l)
        o_ref[0, :, LANES * p:LANES * (p + 1)] = jnp.where(lo, outs[0], outs[1]).astype(BF16)


def _win_attn(qa, ka, va, bias, sink2):
    b = qa.shape[0]
    nq = SEQ // TQ_A
    variant = lambda bi, j: (jnp.where(j == 0, 0, jnp.where(j == nq - 1, 2, 1)), 0, 0, 0)
    return pl.pallas_call(
        _win_attn_kernel,
        grid=(b, nq),
        in_specs=[
            pl.BlockSpec(memory_space=pltpu.SMEM),
            pl.BlockSpec((1, TQ_A, A_WIDTH), lambda bi, j: (bi, j, 0)),
            pl.BlockSpec((1, SEQ, KV_WIDTH), lambda bi, j: (bi, 0, 0)),
            pl.BlockSpec((1, SEQ, KV_WIDTH), lambda bi, j: (bi, 0, 0)),
            pl.BlockSpec((1, A_HEADS, TQ_A, KW_A), variant),
        ],
        out_specs=pl.BlockSpec((1, TQ_A, A_WIDTH), lambda bi, j: (bi, j, 0)),
        out_shape=jax.ShapeDtypeStruct((b, SEQ, A_WIDTH), BF16),
        compiler_params=_cparams(("parallel", "arbitrary")),
        name="win_attn",
    )(sink2, qa, ka, va, bias)


def _win_bias_table():
    slopes = np.array([2.0 ** (-8.0 * (n + 1) / A_HEADS) for n in range(A_HEADS)], np.float32)[list(PAIR_ORDER)]
    i = np.arange(TQ_A)[:, None]
    jj = np.arange(KW_A)[None, :]
    tabs = []
    for off in (0, A_WINDOW, KW_A - TQ_A):
        dist = np.abs(off + i - jj).astype(np.float32)
        tab = np.where(dist[None] <= A_WINDOW, -slopes[:, None, None] * dist[None] * LOG2E, NEG)
        tabs.append(tab)
    return jnp.asarray(np.stack(tabs).astype(np.float32))


def _nbr_attn_kernel(q_ref, k_ref, v_ref, bias_ref, o_ref):
    j = pl.program_id(1)
    rows_per_tile = TQ_B // GRID_W
    krow0 = jnp.clip(j * rows_per_tile - B_ROWS // 2, 0, GRID_ROWS - KW_B // GRID_W)
    start = pl.multiple_of(krow0 * GRID_W, 256)
    lo, m_lo, m_hi = _half_masks()
    for p in range(B_HEADS // 2):
        qblk = q_ref[0, :, LANES * p:LANES * (p + 1)]
        kw = k_ref[0, pl.ds(start, KW_B), LANES * p:LANES * (p + 1)]
        vw = v_ref[0, pl.ds(start, KW_B), LANES * p:LANES * (p + 1)]
        outs = []
        for half in range(2):
            hidx = 2 * p + half
            qm = qblk * (m_lo if half == 0 else m_hi)
            s = lax.dot_general(qm, kw, (((1,), (1,)), ((), ())), preferred_element_type=F32)
            s = s + bias_ref[0, hidx]
            m = jnp.max(s, axis=-1, keepdims=True)
            e = jnp.exp2(s - m)
            l = jnp.sum(e, axis=-1, keepdims=True)
            o2 = jnp.dot(e.astype(BF16), vw, preferred_element_type=F32)
            outs.append(o2 / l)
        o_ref[0, :, LANES * p:LANES * (p + 1)] = jnp.where(lo, outs[0], outs[1]).astype(BF16)


def _nbr_attn(qb, kb, vb, bias):
    b = qb.shape[0]
    nq = SEQ // TQ_B
    variant = lambda bi, j: (jnp.where(j == 0, 0, jnp.where(j == nq - 1, 2, 1)), 0, 0, 0)
    return pl.pallas_call(
        _nbr_attn_kernel,
        grid=(b, nq),
        in_specs=[
            pl.BlockSpec((1, TQ_B, B_WIDTH), lambda bi, j: (bi, j, 0)),
            pl.BlockSpec((1, SEQ, B_WIDTH), lambda bi, j: (bi, 0, 0)),
            pl.BlockSpec((1, SEQ, B_WIDTH), lambda bi, j: (bi, 0, 0)),
            pl.BlockSpec((1, B_HEADS, TQ_B, KW_B), variant),
        ],
        out_specs=pl.BlockSpec((1, TQ_B, B_WIDTH), lambda bi, j: (bi, j, 0)),
        out_shape=jax.ShapeDtypeStruct((b, SEQ, B_WIDTH), BF16),
        compiler_params=_cparams(("parallel", "arbitrary")),
        name="nbr_attn",
    )(qb, kb, vb, bias)


def _nbr_bias_table(rpb):
    rows_per_tile = TQ_B // GRID_W
    krows = KW_B // GRID_W
    r = rpb.astype(F32) * LOG2E
    edge = GRID_W - B_COLS
    ext = jnp.concatenate([jnp.repeat(r[..., :1], edge, axis=-1), r, jnp.repeat(r[..., -1:], edge, axis=-1)], axis=-1)
    col = jnp.stack([ext[..., GRID_W - 1 - q:2 * GRID_W - 1 - q] for q in range(GRID_W)], axis=2)
    tabs = []
    for r_first in (0, rows_per_tile, GRID_ROWS - rows_per_tile):
        krow0 = int(np.clip(r_first - B_ROWS // 2, 0, GRID_ROWS - krows))
        slabs = []
        for ql in range(rows_per_tile):
            per_k = [col[:, int(np.clip(krow0 + kl - (r_first + ql) + B_ROWS - 1, 0, 2 * B_ROWS - 2))]
                     for kl in range(krows)]
            slabs.append(jnp.stack(per_k, axis=2))
        tab = jnp.stack(slabs, axis=1).reshape(B_HEADS, TQ_B, KW_B)
        qi = np.arange(TQ_B)
        kj = np.arange(KW_B)
        qr = (r_first + qi // GRID_W)[:, None]
        qcol = (qi % GRID_W)[:, None]
        kr = (krow0 + kj // GRID_W)[None, :]
        kcol = (kj % GRID_W)[None, :]
        r0 = np.clip(qr - B_ROWS // 2, 0, GRID_ROWS - B_ROWS)
        c0 = np.clip(qcol - B_COLS // 2, 0, GRID_W - B_COLS)
        valid = (kr >= r0) & (kr < r0 + B_ROWS) & (kcol >= c0) & (kcol < c0 + B_COLS)
        tabs.append(jnp.where(jnp.asarray(valid)[None], tab, NEG))
    return jnp.stack(tabs)


def _dense_attn_kernel(q_ref, k_ref, v_ref, o_ref, s_ref):
    lo, m_lo, m_hi = _half_masks()
    nchunk = SEQ // KC_C
    npairs = C_HEADS // 2

    def scores(p):
        qblk = q_ref[0, :, LANES * p:LANES * (p + 1)]
        q2 = jnp.concatenate([qblk * m_lo, qblk * m_hi], axis=0)
        m_run = None
        for c in range(nchunk):
            kc = k_ref[0, KC_C * c:KC_C * (c + 1), :]
            s = lax.dot_general(q2, kc, (((1,), (1,)), ((), ())), preferred_element_type=F32)
            s_ref[p % 2, :, KC_C * c:KC_C * (c + 1)] = s
            for t in range(KC_C // LANES):
                blk = s[:, LANES * t:LANES * (t + 1)]
                m_run = blk if m_run is None else jnp.maximum(m_run, blk)
        return jnp.max(m_run, axis=-1, keepdims=True)

    def weighted_values(p, m):
        acc = None
        for c in range(nchunk):
            e = jnp.exp2(s_ref[p % 2, :, KC_C * c:KC_C * (c + 1)] - m).astype(BF16)
            part = jnp.dot(e, v_ref[0, KC_C * c:KC_C * (c + 1), :], preferred_element_type=F32)
            acc = part if acc is None else acc + part
        on = acc[:, :LANES] / acc[:, LANES:]
        o_ref[0, :, LANES * p:LANES * (p + 1)] = jnp.where(lo, on[:TQ_C], on[TQ_C:]).astype(BF16)

    m_next = scores(0)
    for p in range(npairs):
        m_cur = m_next
        if p + 1 < npairs:
            m_next = scores(p + 1)
        weighted_values(p, m_cur)


def _dense_attn(qc, kc, vc):
    b = qc.shape[0]
    return pl.pallas_call(
        _dense_attn_kernel,
        grid=(b, SEQ // TQ_C),
        in_specs=[
            pl.BlockSpec((1, TQ_C, C_WIDTH), lambda bi, j: (bi, j, 0)),
            pl.BlockSpec((1, SEQ, KV_WIDTH), lambda bi, j: (bi, 0, 0)),
            pl.BlockSpec((1, SEQ, 2 * KV_WIDTH), lambda bi, j: (bi, 0, 0)),
        ],
        out_specs=pl.BlockSpec((1, TQ_C, C_WIDTH), lambda bi, j: (bi, j, 0)),
        out_shape=jax.ShapeDtypeStruct((b, SEQ, C_WIDTH), BF16),
        scratch_shapes=[pltpu.VMEM((2, 2 * TQ_C, SEQ), F32)],
        compiler_params=_cparams(("parallel", "arbitrary")),
        name="dense_attn",
    )(qc, kc, vc)


def _out_proj_kernel(oa_ref, ob_ref, oc_ref, x_ref, ga_ref, gb_ref, gc_ref, wa_ref, wb_ref, wc_ref,
                     ln_ref, wr_ref, br_ref, xn_ref, h_ref, cls_ref):
    def nrm(o_ref, g_ref):
        o = o_ref[...].astype(F32)
        ms = jnp.mean(o * o, axis=-1, keepdims=True)
        return (o * lax.rsqrt(ms + EPS) * g_ref[...]).astype(BF16)

    acc = jnp.dot(nrm(oa_ref, ga_ref), wa_ref[...], preferred_element_type=F32)
    acc = acc + jnp.dot(nrm(ob_ref, gb_ref), wb_ref[...], preferred_element_type=F32)
    acc = acc + jnp.dot(nrm(oc_ref, gc_ref), wc_ref[...], preferred_element_type=F32)
    xn = x_ref[...] + acc
    xn_ref[...] = xn
    ms = jnp.mean(xn * xn, axis=-1, keepdims=True)
    h2 = xn * lax.rsqrt(ms + EPS) * ln_ref[...]
    hi = h2.astype(BF16)
    lo = (h2 - hi.astype(F32)).astype(BF16)
    tm = xn.shape[0]
    half = D_MODEL // 2
    bits_lo = lax.bitcast_convert_type(hi[:, :half].astype(F32), jnp.uint32)
    bits_hi = lax.bitcast_convert_type(hi[:, half:].astype(F32), jnp.uint32)
    words = (bits_hi & jnp.uint32(0xFFFF0000)) | (bits_lo >> 16)
    for s in range(GATE_ROW):
        h_ref[pl.ds(s, tm, stride=SUBLANES), :] = words[:, LANES * s:LANES * (s + 1)]
    for s in range(GATE_ROW + 1, SUBLANES):
        h_ref[pl.ds(s, tm, stride=SUBLANES), :] = jnp.zeros((tm, LANES), jnp.uint32)
    wr = wr_ref[...]
    t = jnp.dot(hi, wr, preferred_element_type=F32)
    u = jnp.dot(lo, wr[:, :LANES], preferred_element_type=F32)
    logits = t[:, :LANES] + t[:, LANES:] + u + br_ref[...]

    lane = lax.broadcasted_iota(jnp.int32, logits.shape, 1).astype(F32)
    big = jnp.float32(3.0e38)
    is_g = lane < N_GROUPS
    gl = jnp.where(is_g, logits, -big)
    mg = jnp.max(gl, axis=-1, keepdims=True)
    grp = jnp.min(jnp.where(gl == mg, lane, big), axis=-1, keepdims=True)
    pg = 1.0 / jnp.sum(jnp.where(is_g, jnp.exp(gl - mg), 0.0), axis=-1, keepdims=True)
    e_lo = N_GROUPS + EXPERTS_PER_GROUP * grp
    sel = (lane >= e_lo) & (lane < e_lo + EXPERTS_PER_GROUP)
    el = jnp.where(sel, logits, -big)
    v1 = jnp.max(el, axis=-1, keepdims=True)
    i1 = jnp.min(jnp.where(el == v1, lane, big), axis=-1, keepdims=True)
    el2 = jnp.where(lane == i1, -big, el)
    v2 = jnp.max(el2, axis=-1, keepdims=True)
    i2 = jnp.min(jnp.where(el2 == v2, lane, big), axis=-1, keepdims=True)
    e21 = jnp.exp(v2 - v1)
    w1 = pg / (1.0 + e21)
    w2 = pg * e21 / (1.0 + e21)
    a = jnp.minimum(i1, i2) - e_lo
    b = jnp.maximum(i1, i2) - e_lo
    cls = grp * PAIRS_PER_GROUP + (a * (7.0 - a) * 0.5 + (b - a - 1.0))
    gates = jnp.where(lane == i1, w1, jnp.where(lane == i2, w2, jnp.where(lane == CLASS_LANE, cls, 0.0)))
    h_ref[pl.ds(GATE_ROW, tm, stride=SUBLANES), :] = lax.bitcast_convert_type(gates, jnp.uint32)
    cls_ref[...] = jnp.broadcast_to(cls, (tm, LANES)).astype(jnp.int32)


def _out_proj(oa, ob, oc, x2d, ga, gb, gc, wa, wb, wc, ln2, wr, br):
    n = x2d.shape[0]
    tm = TM_PROJ
    row = lambda i: (i, 0)
    fixed = lambda i: (0, 0)
    return pl.pallas_call(
        _out_proj_kernel,
        grid=(n // tm,),
        in_specs=[
            pl.BlockSpec((tm, A_WIDTH), row),
            pl.BlockSpec((tm, B_WIDTH), row),
            pl.BlockSpec((tm, C_WIDTH), row),
            pl.BlockSpec((tm, D_MODEL), row),
            pl.BlockSpec((1, A_WIDTH), fixed),
            pl.BlockSpec((1, B_WIDTH), fixed),
            pl.BlockSpec((1, C_WIDTH), fixed),
            pl.BlockSpec((A_WIDTH, D_MODEL), fixed),
            pl.BlockSpec((B_WIDTH, D_MODEL), fixed),
            pl.BlockSpec((C_WIDTH, D_MODEL), fixed),
            pl.BlockSpec((1, D_MODEL), fixed),
            pl.BlockSpec((D_MODEL, 2 * LANES), fixed),
            pl.BlockSpec((1, LANES), fixed),
        ],
        out_specs=[
            pl.BlockSpec((tm, D_MODEL), row),
            pl.BlockSpec((tm * SUBLANES, LANES), row),
            pl.BlockSpec((tm, LANES), row),
        ],
        out_shape=[
            jax.ShapeDtypeStruct((n, D_MODEL), F32),
            jax.ShapeDtypeStruct((n * SUBLANES, LANES), jnp.uint32),
            jax.ShapeDtypeStruct((n, LANES), jnp.int32),
        ],
        compiler_params=_cparams(("parallel",)),
        name="out_proj_router",
    )(oa, ob, oc, x2d, ga, gb, gc, wa, wb, wc, ln2, wr, br)


def _tile_copy(src_hbm, buf, sem, slot, src_row8, dst_tok):
    src = src_hbm.at[pl.ds(pl.multiple_of(src_row8, SUBLANES), SUBLANES), :]
    return pltpu.make_async_copy(src, buf[slot].at[pl.ds(SUBLANES * dst_tok, SUBLANES), :], sem.at[slot])


def _issue_gather(idx_ref, src_hbm, buf, sem, slot, toks):
    for r in range(toks):
        _tile_copy(src_hbm, buf, sem, slot, idx_ref[0, 0, r], r).start()


def _wait_gather(src_hbm, buf, sem, slot, toks):
    pltpu.make_async_copy(src_hbm.at[pl.ds(0, SUBLANES * toks), :], buf[slot], sem.at[slot]).wait()


def _tile_row(ref, s, toks):
    return ref[pl.ds(s, toks, stride=SUBLANES), :]


def _dispatch_kernel(ztile_ref, pos_ref, h_ref, o_hbm, zbuf, sem):
    tm = TM_DISP
    i = pl.program_id(0)

    @pl.when(i == 0)
    def _():
        zbuf[...] = jnp.zeros(zbuf.shape, zbuf.dtype)

        def zero_tile(c):
            dst = o_hbm.at[pl.ds(pl.multiple_of(ztile_ref[c], SUBLANES), TM_MOE * SUBLANES), :]
            return pltpu.make_async_copy(zbuf, dst, sem.at[1])

        for c in range(N_CLASSES):
            @pl.when(ztile_ref[c] >= 0)
            def _(c=c):
                zero_tile(c).start()
        for c in range(N_CLASSES):
            @pl.when(ztile_ref[c] >= 0)
            def _(c=c):
                zero_tile(c).wait()

    def row_copy(r):
        dst = o_hbm.at[pl.ds(pl.multiple_of(pos_ref[0, 0, r], SUBLANES), SUBLANES), :]
        return pltpu.make_async_copy(h_ref.at[pl.ds(SUBLANES * r, SUBLANES), :], dst, sem.at[0])

    for r in range(tm):
        row_copy(r).start()
    pltpu.make_async_copy(h_ref, o_hbm.at[pl.ds(0, tm * SUBLANES), :], sem.at[0]).wait()


def _dispatch(hrow, ztile, pos_tiles, nt):
    n = hrow.shape[0] // SUBLANES
    tm = TM_DISP
    grid_spec = pltpu.PrefetchScalarGridSpec(
        num_scalar_prefetch=1,
        grid=(n // tm,),
        in_specs=[
            pl.BlockSpec((1, 1, tm), lambda i, zt: (i, 0, 0), memory_space=pltpu.SMEM),
            pl.BlockSpec((tm * SUBLANES, LANES), lambda i, zt: (i, 0)),
        ],
        out_specs=pl.BlockSpec(memory_space=pl.ANY),
        scratch_shapes=[pltpu.VMEM((TM_MOE * SUBLANES, LANES), jnp.uint32), pltpu.SemaphoreType.DMA((2,))],
    )
    return pl.pallas_call(
        _dispatch_kernel,
        grid_spec=grid_spec,
        out_shape=jax.ShapeDtypeStruct((nt * TM_MOE * SUBLANES, LANES), jnp.uint32),
        compiler_params=_cparams(("arbitrary",)),
        name="moe_dispatch",
    )(ztile, pos_tiles, hrow)


def _moe_kernel(ea_ref, eb_ref, nvalid_ref, h_ref, wga_ref, wua_ref, wda_ref, wgb_ref, wub_ref, wdb_ref, y_ref):
    tm = TM_MOE
    i = pl.program_id(0)
    nvalid = nvalid_ref[0]

    @pl.when(i < nvalid)
    def _():
        parts_lo, parts_hi = [], []
        for s in range(D_MODEL // 2 // LANES):
            w = _tile_row(h_ref, s, tm)
            parts_lo.append(lax.bitcast_convert_type(w << 16, F32).astype(BF16))
            parts_hi.append(lax.bitcast_convert_type(w & jnp.uint32(0xFFFF0000), F32).astype(BF16))
        x = jnp.concatenate(parts_lo + parts_hi, axis=1)
        gates = lax.bitcast_convert_type(_tile_row(h_ref, GATE_ROW, tm), F32)
        lane = lax.broadcasted_iota(jnp.int32, gates.shape, 1)

        def expert(wg_ref, wu_ref, wd_ref, e):
            g = jnp.dot(x, wg_ref[0], preferred_element_type=F32)
            u = jnp.dot(x, wu_ref[0], preferred_element_type=F32)
            act = (g / (1.0 + jnp.exp(-g)) * u).astype(BF16)
            y = jnp.dot(act, wd_ref[0], preferred_element_type=F32)
            ge = jnp.sum(jnp.where(lane == e + N_GROUPS, gates, 0.0), axis=-1, keepdims=True)
            return ge * y

        y = expert(wga_ref, wua_ref, wda_ref, ea_ref[i]) + expert(wgb_ref, wub_ref, wdb_ref, eb_ref[i])
        for s in range(SUBLANES):
            y_ref[pl.ds(s, tm, stride=SUBLANES), :] = y[:, LANES * s:LANES * (s + 1)]

    @pl.when(i >= nvalid)
    def _():
        y_ref[...] = jnp.zeros(y_ref.shape, F32)


def _moe_experts(h_sorted, ea, eb, nvalid, wg, wu, wd):
    tm = TM_MOE
    nt = ea.shape[0]
    w_in_spec = lambda sel: pl.BlockSpec((1, D_MODEL, D_EXPERT), sel)
    w_out_spec = lambda sel: pl.BlockSpec((1, D_EXPERT, D_MODEL), sel)
    sel_a = lambda i, ea, eb, nv: (ea[i], 0, 0)
    sel_b = lambda i, ea, eb, nv: (eb[i], 0, 0)
    rows = lambda i, ea, eb, nv: (jnp.minimum(i, nv[0] - 1), 0)
    grid_spec = pltpu.PrefetchScalarGridSpec(
        num_scalar_prefetch=3,
        grid=(nt,),
        in_specs=[
            pl.BlockSpec((tm * SUBLANES, LANES), rows),
            w_in_spec(sel_a), w_in_spec(sel_a), w_out_spec(sel_a),
            w_in_spec(sel_b), w_in_spec(sel_b), w_out_spec(sel_b),
        ],
        out_specs=pl.BlockSpec((tm * SUBLANES, LANES), lambda i, ea, eb, nv: (i, 0)),
    )
    return pl.pallas_call(
        _moe_kernel,
        grid_spec=grid_spec,
        out_shape=jax.ShapeDtypeStruct((nt * tm * SUBLANES, LANES), F32),
        compiler_params=_cparams(("arbitrary",)),
        name="moe_experts",
    )(ea, eb, nvalid, h_sorted, wg, wu, wd, wg, wu, wd)


def _combine_kernel(pos0_ref, pos1_ref, posn_ref, xn_ref, y_hbm, o_ref, buf0, buf1, buf2, sem):
    buf = (buf0, buf1, buf2)
    tm = TM_COMB
    i = pl.program_id(0)

    @pl.when(i == 0)
    def _():
        _issue_gather(pos0_ref, y_hbm, buf, sem, 0, tm)
        _issue_gather(pos1_ref, y_hbm, buf, sem, 1, tm)

    for slot in range(GATHER_SLOTS):
        @pl.when(i % GATHER_SLOTS == slot)
        def _(slot=slot):
            _wait_gather(y_hbm, buf, sem, slot, tm)
            _issue_gather(posn_ref, y_hbm, buf, sem, (slot + 2) % GATHER_SLOTS, tm)
            for s in range(SUBLANES):
                cols = slice(LANES * s, LANES * (s + 1))
                o_ref[:, cols] = xn_ref[:, cols] + _tile_row(buf[slot], s, tm)

            @pl.when(i == pl.num_programs(0) - 1)
            def _():
                _wait_gather(y_hbm, buf, sem, (slot + 1) % GATHER_SLOTS, tm)
                _wait_gather(y_hbm, buf, sem, (slot + 2) % GATHER_SLOTS, tm)


def _combine(pos_tiles, xn, y_sorted):
    n = xn.shape[0]
    tm = TM_COMB
    smem_blk = lambda f: pl.BlockSpec((1, 1, tm), f, memory_space=pltpu.SMEM)
    return pl.pallas_call(
        _combine_kernel,
        grid=(n // tm,),
        in_specs=[
            smem_blk(lambda i: (0, 0, 0)),
            smem_blk(lambda i: (1, 0, 0)),
            smem_blk(lambda i: (i + 2, 0, 0)),
            pl.BlockSpec((tm, D_MODEL), lambda i: (i, 0)),
            pl.BlockSpec(memory_space=pl.ANY),
        ],
        out_specs=pl.BlockSpec((tm, D_MODEL), lambda i: (i, 0)),
        out_shape=jax.ShapeDtypeStruct((n, D_MODEL), F32),
        scratch_shapes=[pltpu.VMEM((tm * SUBLANES, LANES), F32) for _ in range(GATHER_SLOTS)]
        + [pltpu.SemaphoreType.DMA((GATHER_SLOTS,))],
        compiler_params=_cparams(("arbitrary",)),
        name="moe_combine",
    )(pos_tiles, pos_tiles, pos_tiles, xn, y_sorted)


def _moe_plan(cls, n):
    tm = TM_MOE
    nt = n // tm + N_CLASSES
    onehot = (cls[:, None] == jnp.arange(N_CLASSES, dtype=jnp.int32)[None, :]).astype(jnp.int32)
    counts = jnp.sum(onehot, axis=0)
    rank = jnp.sum(jnp.cumsum(onehot, axis=0) * onehot, axis=1) - 1
    tiles = (counts + tm - 1) // tm
    tile_end = jnp.cumsum(tiles)
    tile_start = tile_end - tiles
    pos = jnp.sum(onehot * tile_start[None, :], axis=1) * tm + rank
    pad = GATHER_SLOTS - 1
    ztile = jnp.where(tiles > 0, (tile_end - 1) * (tm * SUBLANES), -1).astype(jnp.int32)
    nvalid = tile_end[-1]
    tile_id = jnp.minimum(jnp.arange(nt, dtype=jnp.int32), nvalid - 1)
    tcls = jnp.sum((tile_id[:, None] >= tile_end[None, :]).astype(jnp.int32), axis=1)
    grp, pair = tcls // PAIRS_PER_GROUP, tcls % PAIRS_PER_GROUP
    pair_a = jnp.asarray(PAIR_A, jnp.int32)
    pair_b = jnp.asarray(PAIR_B, jnp.int32)
    ea = grp * EXPERTS_PER_GROUP + jnp.sum((pair[:, None] == jnp.arange(PAIRS_PER_GROUP)[None]) * pair_a[None], axis=1)
    eb = grp * EXPERTS_PER_GROUP + jnp.sum((pair[:, None] == jnp.arange(PAIRS_PER_GROUP)[None]) * pair_b[None], axis=1)
    pos8 = SUBLANES * pos
    pos_disp = pos8.reshape(n // TM_DISP, 1, TM_DISP)
    pos_comb = jnp.concatenate([pos8, jnp.zeros((pad * TM_COMB,), jnp.int32)]).reshape(n // TM_COMB + pad, 1, TM_COMB)
    return (ea.astype(jnp.int32), eb.astype(jnp.int32), nvalid.reshape(1).astype(jnp.int32), ztile,
            pos_disp, pos_comb)


def _moe(hrow, cls, xn, wg, wu, wd):
    n = xn.shape[0]
    ea, eb, nvalid, ztile, pos_disp, pos_comb = _moe_plan(cls[:, 0], n)
    h_sorted = _dispatch(hrow, ztile, pos_disp, ea.shape[0])
    y_sorted = _moe_experts(h_sorted, ea, eb, nvalid, wg, wu, wd)
    return _combine(pos_comb, xn, y_sorted)


def _head_cols(base, heads, swap=False):
    inner = _ROPE_SWAP if swap else np.arange(HEAD_DIM)
    return np.concatenate([base + HEAD_DIM * h + inner for h in heads])


def _proj_columns():
    o_qa, o_ka, o_va = 0, A_WIDTH, A_WIDTH + KV_WIDTH
    o_qb = o_va + KV_WIDTH
    o_kb, o_vb = o_qb + B_WIDTH, o_qb + 2 * B_WIDTH
    o_qc = o_vb + B_WIDTH
    o_kc, o_vc = o_qc + C_WIDTH, o_qc + C_WIDTH + KV_WIDTH
    nat2, nat4 = range(2), range(4)
    return np.concatenate([
        _head_cols(o_qa, PAIR_ORDER), _head_cols(o_ka, nat2),
        _head_cols(o_qb, nat4), _head_cols(o_kb, nat4),
        _head_cols(o_qc, PAIR_ORDER), _head_cols(o_kc, nat2),
        _head_cols(o_qc, PAIR_ORDER, swap=True), _head_cols(o_kc, nat2, swap=True),
        _head_cols(o_va, nat2), _head_cols(o_vb, nat4), _head_cols(o_vc, nat2),
    ])


def _rope_tables():
    nf = HEAD_DIM // 4
    inv = ROPE_THETA ** (-jnp.arange(nf, dtype=F32) / nf)
    pos = jnp.arange(SEQ)
    ang_r = (pos // GRID_W).astype(F32)[:, None] * inv[None, :]
    ang_c = (pos % GRID_W).astype(F32)[:, None] * inv[None, :]
    cos = jnp.concatenate([jnp.cos(ang_r)] * 2 + [jnp.cos(ang_c)] * 2, axis=-1)
    sin = jnp.concatenate([-jnp.sin(ang_r), jnp.sin(ang_r), -jnp.sin(ang_c), jnp.sin(ang_c)], axis=-1)
    return jnp.concatenate([cos, cos], axis=-1), jnp.concatenate([sin, sin], axis=-1)


def _layer_params(l, w_in, ln1, qk_gain, sink, rpb, out_gain, w_out, ln2, w_rg, b_rg, w_re, b_re):
    cols = _proj_columns()
    w_ext = w_in[l][:, cols].astype(BF16)
    qs = HEAD_DIM ** -0.5 * LOG2E
    g = qk_gain[l].astype(F32)
    sw = _ROPE_SWAP
    gain = jnp.concatenate([
        jnp.tile(g[0, 0], A_HEADS) * qs, jnp.tile(g[0, 1], A_KV),
        jnp.tile(g[1, 0], B_HEADS) * qs, jnp.tile(g[1, 1], B_HEADS),
        jnp.tile(g[2, 0], C_HEADS) * qs, jnp.tile(g[2, 1], C_KV),
        jnp.tile(g[2, 0][sw], C_HEADS) * qs, jnp.tile(g[2, 1][sw], C_KV),
    ])[None, :]
    order = list(PAIR_ORDER)
    sink2 = sink[l].astype(F32)[jnp.asarray(order)] * LOG2E
    rows_a = _head_cols(0, PAIR_ORDER)
    rows_b = A_WIDTH + np.arange(B_WIDTH)
    rows_c = _head_cols(A_WIDTH + B_WIDTH, PAIR_ORDER)
    og = out_gain[l].astype(F32)
    wo = w_out[l]
    wr = jnp.concatenate([w_rg[l], w_re[l], jnp.zeros((D_MODEL, LANES - N_GROUPS - N_EXPERTS), F32)], axis=1)
    wr_hi = wr.astype(BF16)
    wr_lo = (wr - wr_hi.astype(F32)).astype(BF16)
    br = jnp.concatenate([b_rg[l].astype(F32), b_re[l].astype(F32),
                          jnp.zeros((LANES - N_GROUPS - N_EXPERTS,), F32)])[None, :]
    return dict(
        w_ext=w_ext, ln1=ln1[l][None, :], gain=gain, sink2=sink2, nbr_bias=_nbr_bias_table(rpb[l]),
        ga=og[rows_a][None, :], gb=og[rows_b][None, :], gc=og[rows_c][None, :],
        wa=wo[rows_a].astype(BF16), wb=wo[rows_b].astype(BF16), wc=wo[rows_c].astype(BF16),
        ln2=ln2[l][None, :], wr=jnp.concatenate([wr_hi, wr_lo], axis=1), br=br,
    )


def _block_ones():
    idx = np.arange(MXU_DIM) // HEAD_DIM
    return jnp.asarray((idx[:, None] == idx[None, :]).astype(np.float32)).astype(BF16)


def _trunk(x, params, shared, moe_w):
    b = x.shape[0]
    x2d = x.reshape(b * SEQ, D_MODEL)
    for l, lp in enumerate(params):
        qa, ka, va, qb, kb, vb, qc, kc, vc = _in_proj(
            x2d, lp["w_ext"], lp["ln1"], lp["gain"], shared["cos"], shared["sin"], shared["ones"])
        seq = lambda z: z.reshape(b, SEQ, z.shape[-1])
        oa = _win_attn(seq(qa), seq(ka), seq(va), shared["win_bias"], lp["sink2"])
        ob = _nbr_attn(seq(qb), seq(kb), seq(vb), lp["nbr_bias"])
        oc = _dense_attn(seq(qc), seq(kc), seq(vc))
        flat = lambda z: z.reshape(b * SEQ, z.shape[-1])
        xn, hrow, cls = _out_proj(flat(oa), flat(ob), flat(oc), x2d, lp["ga"], lp["gb"], lp["gc"],
                             lp["wa"], lp["wb"], lp["wc"], lp["ln2"], lp["wr"], lp["br"])
        wg, wu, wd = moe_w[l]
        x2d = _moe(hrow, cls, xn, wg, wu, wd)
    return x2d.reshape(b, SEQ, D_MODEL)


def kernel(x_prompt, x_sample, ln1, w_in, qk_gain, sink, rpb, out_gain, w_out, ln2, w_router_group,
           b_router_group, w_router_expert, b_router_expert, w_gate, w_up, w_down):
    depth = w_in.shape[0]
    params = [_layer_params(l, w_in, ln1, qk_gain, sink, rpb, out_gain, w_out, ln2, w_router_group,
                            b_router_group, w_router_expert, b_router_expert) for l in range(depth)]
    cos_t, sin_t = _rope_tables()
    shared = dict(cos=cos_t, sin=sin_t, ones=_block_ones(), win_bias=_win_bias_table())
    moe_w = [(w_gate[l].astype(BF16), w_up[l].astype(BF16), w_down[l].astype(BF16)) for l in range(depth)]
    y_prompt = _trunk(x_prompt, params, shared, moe_w)
    y_sample = _trunk(x_sample, params, shared, moe_w)
    return (y_prompt, y_sample)
```

```python
import functools
import math

import numpy as np
import jax
import jax.numpy as jnp
from jax import lax
from jax.experimental import pallas as pl
from jax.experimental.pallas import tpu as pltpu

D_MODEL = 1024
SEQ = 4096
HEAD_DIM = 64
GRID_W = 64
GRID_ROWS = SEQ // GRID_W
A_HEADS, A_KV, A_WINDOW = 6, 2, 128
B_HEADS, B_ROWS, B_COLS = 4, 8, 16
C_HEADS, C_KV = 6, 2
ROPE_THETA = 10000.0
N_GROUPS, EXPERTS_PER_GROUP = 4, 4
N_EXPERTS = N_GROUPS * EXPERTS_PER_GROUP
D_EXPERT = D_MODEL // 2
EPS = 1e-6
NEG = -1e30
LOG2E = math.log2(math.e)

A_WIDTH = A_HEADS * HEAD_DIM
B_WIDTH = B_HEADS * HEAD_DIM
C_WIDTH = C_HEADS * HEAD_DIM
KV_WIDTH = A_KV * HEAD_DIM

LANES = 128
MXU_DIM = 256
VMEM_LIMIT = 56 * 1024 * 1024

PAIR_ORDER = (0, 3, 1, 4, 2, 5)

TM_PROJ = 512
TQ_A = 256
KW_A = 512
TQ_B = 256
KW_B = 768
TQ_C = 512
KC_C = 512
TM_MOE = 256
TM_COMB = 512
TM_DISP = 512

PAIR_A = (0, 0, 0, 1, 1, 2)
PAIR_B = (1, 2, 3, 2, 3, 3)
PAIRS_PER_GROUP = len(PAIR_A)
N_CLASSES = N_GROUPS * PAIRS_PER_GROUP
CLASS_LANE = N_GROUPS + N_EXPERTS
SUBLANES = 8
GATE_ROW = D_MODEL // 2 // LANES
GATHER_SLOTS = 3
DMA_THREADS = 2

BF16 = jnp.bfloat16
F32 = jnp.float32

_ROPE_SWAP = np.concatenate([np.arange(16, 32), np.arange(0, 16), np.arange(48, 64), np.arange(32, 48)])


def _cparams(sem):
    return pltpu.CompilerParams(dimension_semantics=sem, vmem_limit_bytes=VMEM_LIMIT)


N_NORMED = 1536
N_GAINED = 2048
N_PROJ = 2560


def _in_proj_kernel(x_ref, w_ref, ln_ref, gain_ref, cos_ref, sin_ref, ones_ref,
                    qa_ref, ka_ref, va_ref, qb_ref, kb_ref, vb_ref, qc_ref, kc_ref, vc_ref):
    x = x_ref[...]
    ms = jnp.mean(x * x, axis=-1, keepdims=True)
    h = (x * lax.rsqrt(ms + EPS) * ln_ref[...]).astype(BF16)
    p = jnp.dot(h, w_ref[...], preferred_element_type=F32)
    g = gain_ref[...]
    ones_blk = ones_ref[...]

    def chunk(c):
        return p[:, MXU_DIM * c:MXU_DIM * (c + 1)]

    def inv_rms(pc):
        ss = jnp.dot((pc * pc).astype(BF16), ones_blk, preferred_element_type=F32)
        return lax.rsqrt(ss * (1.0 / HEAD_DIM) + EPS)

    r = [inv_rms(chunk(c)) for c in range(N_NORMED // MXU_DIM)]
    pn = [chunk(c) * r[c] * g[:, MXU_DIM * c:MXU_DIM * (c + 1)] for c in range(N_NORMED // MXU_DIM)]
    qa_ref[:, 0:256] = pn[0].astype(BF16)
    qa_ref[:, 256:384] = pn[1][:, :128].astype(BF16)
    ka_ref[...] = pn[1][:, 128:].astype(BF16)
    qb_ref[...] = pn[2].astype(BF16)
    kb_ref[...] = pn[3].astype(BF16)
    cos = cos_ref[...]
    sin = sin_ref[...]
    cos2 = jnp.concatenate([cos, cos], axis=1)
    sin2 = jnp.concatenate([sin, sin], axis=1)
    sw4 = chunk(6) * r[4] * g[:, 1536:1792]
    sw5 = chunk(7) * r[5] * g[:, 1792:2048]
    c4 = pn[4] * cos2 + sw4 * sin2
    c5 = pn[5] * cos2 + sw5 * sin2
    qc_ref[:, 0:256] = c4.astype(BF16)
    qc_ref[:, 256:384] = c5[:, :128].astype(BF16)
    kc_ref[...] = c5[:, 128:].astype(BF16)
    va_ref[...] = p[:, 2048:2176].astype(BF16)
    vb_ref[...] = p[:, 2176:2432].astype(BF16)
    vc_ref[:, 0:128] = p[:, 2432:2560].astype(BF16)
    vc_ref[:, 128:256] = jnp.ones((x.shape[0], 128), BF16)


def _in_proj(x2d, w_ext, ln, gain, cos_t, sin_t, ones_blk):
    n = x2d.shape[0]
    tm = TM_PROJ
    tiles_per_seq = SEQ // tm
    row = lambda i: (i, 0)
    fixed = lambda i: (0, 0)
    pos = lambda i: (i % tiles_per_seq, 0)
    widths = (A_WIDTH, KV_WIDTH, KV_WIDTH, B_WIDTH, B_WIDTH, B_WIDTH, C_WIDTH, KV_WIDTH, 2 * KV_WIDTH)
    return pl.pallas_call(
        _in_proj_kernel,
        grid=(n // tm,),
        in_specs=[
            pl.BlockSpec((tm, D_MODEL), row),
            pl.BlockSpec((D_MODEL, N_PROJ), fixed),
            pl.BlockSpec((1, D_MODEL), fixed),
            pl.BlockSpec((1, N_GAINED), fixed),
            pl.BlockSpec((tm, LANES), pos),
            pl.BlockSpec((tm, LANES), pos),
            pl.BlockSpec((MXU_DIM, MXU_DIM), fixed),
        ],
        out_specs=[pl.BlockSpec((tm, w), row) for w in widths],
        out_shape=[jax.ShapeDtypeStruct((n, w), BF16) for w in widths],
        compiler_params=_cparams(("parallel",)),
        name="in_proj",
    )(x2d, w_ext, ln, gain, cos_t, sin_t, ones_blk)


def _half_masks():
    lane = lax.broadcasted_iota(jnp.int32, (1, LANES), 1)
    lo = lane < HEAD_DIM
    return lo, lo.astype(BF16), (~lo).astype(BF16)


def _win_attn_kernel(sink_ref, q_ref, k_ref, v_ref, bias_ref, o_ref):
    j = pl.program_id(1)
    start = pl.multiple_of(jnp.clip(j * TQ_A - A_WINDOW, 0, SEQ - KW_A), 128)
    kw = k_ref[0, pl.ds(start, KW_A), :]
    vw = v_ref[0, pl.ds(start, KW_A), :]
    lo, m_lo, m_hi = _half_masks()
    for p in range(A_HEADS // 2):
        qblk = q_ref[0, :, LANES * p:LANES * (p + 1)]
        outs = []
        for half in range(2):
            hidx = 2 * p + half
            qm = qblk * (m_lo if half == 0 else m_hi)
            s = lax.dot_general(qm, kw, (((1,), (1,)), ((), ())), preferred_element_type=F32)
            s = s + bias_ref[0, hidx]
            sk = sink_ref[hidx]
            m = jnp.maximum(jnp.max(s, axis=-1, keepdims=True), sk)
            e = jnp.exp2(s - m)
            l = jnp.sum(e, axis=-1, keepdims=True) + jnp.exp2(sk - m)
            o2 = jnp.dot(e.astype(BF16), vw, preferred_element_type=F32)
            outs.append(o2 / l)
        o_ref[0, :, LANES * p:LANES * (p + 1)] = jnp.where(lo, outs[0], outs[1]).astype(BF16)


def _win_attn(qa, ka, va, bias, sink2):
    b = qa.shape[0]
    nq = SEQ // TQ_A
    variant = lambda bi, j: (jnp.where(j == 0, 0, jnp.where(j == nq - 1, 2, 1)), 0, 0, 0)
    return pl.pallas_call(
        _win_attn_kernel,
        grid=(b, nq),
        in_specs=[
            pl.BlockSpec(memory_space=pltpu.SMEM),
            pl.BlockSpec((1, TQ_A, A_WIDTH), lambda bi, j: (bi, j, 0)),
            pl.BlockSpec((1, SEQ, KV_WIDTH), lambda bi, j: (bi, 0, 0)),
            pl.BlockSpec((1, SEQ, KV_WIDTH), lambda bi, j: (bi, 0, 0)),
            pl.BlockSpec((1, A_HEADS, TQ_A, KW_A), variant),
        ],
        out_specs=pl.BlockSpec((1, TQ_A, A_WIDTH), lambda bi, j: (bi, j, 0)),
        out_shape=jax.ShapeDtypeStruct((b, SEQ, A_WIDTH), BF16),
        compiler_params=_cparams(("parallel", "arbitrary")),
        name="win_attn",
    )(sink2, qa, ka, va, bias)


def _win_bias_table():
    slopes = np.array([2.0 ** (-8.0 * (n + 1) / A_HEADS) for n in range(A_HEADS)], np.float32)[list(PAIR_ORDER)]
    i = np.arange(TQ_A)[:, None]
    jj = np.arange(KW_A)[None, :]
    tabs = []
    for off in (0, A_WINDOW, KW_A - TQ_A):
        dist = np.abs(off + i - jj).astype(np.float32)
        tab = np.where(dist[None] <= A_WINDOW, -slopes[:, None, None] * dist[None] * LOG2E, NEG)
        tabs.append(tab)
    return jnp.asarray(np.stack(tabs).astype(np.float32))


def _nbr_attn_kernel(q_ref, k_ref, v_ref, bias_ref, o_ref):
    j = pl.program_id(1)
    rows_per_tile = TQ_B // GRID_W
    krow0 = jnp.clip(j * rows_per_tile - B_ROWS // 2, 0, GRID_ROWS - KW_B // GRID_W)
    start = pl.multiple_of(krow0 * GRID_W, 256)
    lo, m_lo, m_hi = _half_masks()
    for p in range(B_HEADS // 2):
        qblk = q_ref[0, :, LANES * p:LANES * (p + 1)]
        kw = k_ref[0, pl.ds(start, KW_B), LANES * p:LANES * (p + 1)]
        vw = v_ref[0, pl.ds(start, KW_B), LANES * p:LANES * (p + 1)]
        outs = []
        for half in range(2):
            hidx = 2 * p + half
            qm = qblk * (m_lo if half == 0 else m_hi)
            s = lax.dot_general(qm, kw, (((1,), (1,)), ((), ())), preferred_element_type=F32)
            s = s + bias_ref[0, hidx]
            m = jnp.max(s, axis=-1, keepdims=True)
            e = jnp.exp2(s - m)
            l = jnp.sum(e, axis=-1, keepdims=True)
            o2 = jnp.dot(e.astype(BF16), vw, preferred_element_type=F32)
            outs.append(o2 / l)
        o_ref[0, :, LANES * p:LANES * (p + 1)] = jnp.where(lo, outs[0], outs[1]).astype(BF16)


def _nbr_attn(qb, kb, vb, bias):
    b = qb.shape[0]
    nq = SEQ // TQ_B
    variant = lambda bi, j: (jnp.where(j == 0, 0, jnp.where(j == nq - 1, 2, 1)), 0, 0, 0)
    return pl.pallas_call(
        _nbr_attn_kernel,
        grid=(b, nq),
        in_specs=[
            pl.BlockSpec((1, TQ_B, B_WIDTH), lambda bi, j: (bi, j, 0)),
            pl.BlockSpec((1, SEQ, B_WIDTH), lambda bi, j: (bi, 0, 0)),
            pl.BlockSpec((1, SEQ, B_WIDTH), lambda bi, j: (bi, 0, 0)),
            pl.BlockSpec((1, B_HEADS, TQ_B, KW_B), variant),
        ],
        out_specs=pl.BlockSpec((1, TQ_B, B_WIDTH), lambda bi, j: (bi, j, 0)),
        out_shape=jax.ShapeDtypeStruct((b, SEQ, B_WIDTH), BF16),
        compiler_params=_cparams(("parallel", "arbitrary")),
        name="nbr_attn",
    )(qb, kb, vb, bias)


def _nbr_bias_table(rpb):
    rows_per_tile = TQ_B // GRID_W
    krows = KW_B // GRID_W
    r = rpb.astype(F32) * LOG2E
    edge = GRID_W - B_COLS
    ext = jnp.concatenate([jnp.repeat(r[..., :1], edge, axis=-1), r, jnp.repeat(r[..., -1:], edge, axis=-1)], axis=-1)
    col = jnp.stack([ext[..., GRID_W - 1 - q:2 * GRID_W - 1 - q] for q in range(GRID_W)], axis=2)
    tabs = []
    for r_first in (0, rows_per_tile, GRID_ROWS - rows_per_tile):
        krow0 = int(np.clip(r_first - B_ROWS // 2, 0, GRID_ROWS - krows))
        slabs = []
        for ql in range(rows_per_tile):
            per_k = [col[:, int(np.clip(krow0 + kl - (r_first + ql) + B_ROWS - 1, 0, 2 * B_ROWS - 2))]
                     for kl in range(krows)]
            slabs.append(jnp.stack(per_k, axis=2))
        tab = jnp.stack(slabs, axis=1).reshape(B_HEADS, TQ_B, KW_B)
        qi = np.arange(TQ_B)
        kj = np.arange(KW_B)
        qr = (r_first + qi // GRID_W)[:, None]
        qcol = (qi % GRID_W)[:, None]
        kr = (krow0 + kj // GRID_W)[None, :]
        kcol = (kj % GRID_W)[None, :]
        r0 = np.clip(qr - B_ROWS // 2, 0, GRID_ROWS - B_ROWS)
        c0 = np.clip(qcol - B_COLS // 2, 0, GRID_W - B_COLS)
        valid = (kr >= r0) & (kr < r0 + B_ROWS) & (kcol >= c0) & (kcol < c0 + B_COLS)
        tabs.append(jnp.where(jnp.asarray(valid)[None], tab, NEG))
    return jnp.stack(tabs)


def _dense_attn_kernel(q_ref, k_ref, v_ref, o_ref, s_ref):
    lo, m_lo, m_hi = _half_masks()
    nchunk = SEQ // KC_C
    npairs = C_HEADS // 2

    def scores(p):
        qblk = q_ref[0, :, LANES * p:LANES * (p + 1)]
        q2 = jnp.concatenate([qblk * m_lo, qblk * m_hi], axis=0)
        m_run = None
        for c in range(nchunk):
            kc = k_ref[0, KC_C * c:KC_C * (c + 1), :]
            s = lax.dot_general(q2, kc, (((1,), (1,)), ((), ())), preferred_element_type=F32)
            s_ref[p % 2, :, KC_C * c:KC_C * (c + 1)] = s
            for t in range(KC_C // LANES):
                blk = s[:, LANES * t:LANES * (t + 1)]
                m_run = blk if m_run is None else jnp.maximum(m_run, blk)
        return jnp.max(m_run, axis=-1, keepdims=True)

    def weighted_values(p, m):
        acc = None
        for c in range(nchunk):
            e = jnp.exp2(s_ref[p % 2, :, KC_C * c:KC_C * (c + 1)] - m).astype(BF16)
            part = jnp.dot(e, v_ref[0, KC_C * c:KC_C * (c + 1), :], preferred_element_type=F32)
            acc = part if acc is None else acc + part
        on = acc[:, :LANES] / acc[:, LANES:]
        o_ref[0, :, LANES * p:LANES * (p + 1)] = jnp.where(lo, on[:TQ_C], on[TQ_C:]).astype(BF16)

    m_next = scores(0)
    for p in range(npairs):
        m_cur = m_next
        if p + 1 < npairs:
            m_next = scores(p + 1)
        weighted_values(p, m_cur)


def _dense_attn(qc, kc, vc):
    b = qc.shape[0]
    return pl.pallas_call(
        _dense_attn_kernel,
        grid=(b, SEQ // TQ_C),
        in_specs=[
            pl.BlockSpec((1, TQ_C, C_WIDTH), lambda bi, j: (bi, j, 0)),
            pl.BlockSpec((1, SEQ, KV_WIDTH), lambda bi, j: (bi, 0, 0)),
            pl.BlockSpec((1, SEQ, 2 * KV_WIDTH), lambda bi, j: (bi, 0, 0)),
        ],
        out_specs=pl.BlockSpec((1, TQ_C, C_WIDTH), lambda bi, j: (bi, j, 0)),
        out_shape=jax.ShapeDtypeStruct((b, SEQ, C_WIDTH), BF16),
        scratch_shapes=[pltpu.VMEM((2, 2 * TQ_C, SEQ), F32)],
        compiler_params=_cparams(("parallel", "arbitrary")),
        name="dense_attn",
    )(qc, kc, vc)


def _out_proj_kernel(oa_ref, ob_ref, oc_ref, x_ref, ga_ref, gb_ref, gc_ref, wa_ref, wb_ref, wc_ref,
                     ln_ref, wr_ref, br_ref, xn_ref, h_ref, cls_ref):
    def nrm(o_ref, g_ref):
        o = o_ref[...].astype(F32)
        ms = jnp.mean(o * o, axis=-1, keepdims=True)
        return (o * lax.rsqrt(ms + EPS) * g_ref[...]).astype(BF16)

    acc = jnp.dot(nrm(oa_ref, ga_ref), wa_ref[...], preferred_element_type=F32)
    acc = acc + jnp.dot(nrm(ob_ref, gb_ref), wb_ref[...], preferred_element_type=F32)
    acc = acc + jnp.dot(nrm(oc_ref, gc_ref), wc_ref[...], preferred_element_type=F32)
    xn = x_ref[...] + acc
    xn_ref[...] = xn
    ms = jnp.mean(xn * xn, axis=-1, keepdims=True)
    h2 = xn * lax.rsqrt(ms + EPS) * ln_ref[...]
    hi = h2.astype(BF16)
    lo = (h2 - hi.astype(F32)).astype(BF16)
    tm = xn.shape[0]
    half = D_MODEL // 2
    bits_lo = lax.bitcast_convert_type(hi[:, :half].astype(F32), jnp.uint32)
    bits_hi = lax.bitcast_convert_type(hi[:, half:].astype(F32), jnp.uint32)
    words = (bits_hi & jnp.uint32(0xFFFF0000)) | (bits_lo >> 16)
    for s in range(GATE_ROW):
        h_ref[pl.ds(s, tm, stride=SUBLANES), :] = words[:, LANES * s:LANES * (s + 1)]
    for s in range(GATE_ROW + 1, SUBLANES):
        h_ref[pl.ds(s, tm, stride=SUBLANES), :] = jnp.zeros((tm, LANES), jnp.uint32)
    wr = wr_ref[...]
    t = jnp.dot(hi, wr, preferred_element_type=F32)
    u = jnp.dot(lo, wr[:, :LANES], preferred_element_type=F32)
    logits = t[:, :LANES] + t[:, LANES:] + u + br_ref[...]

    lane = lax.broadcasted_iota(jnp.int32, logits.shape, 1).astype(F32)
    big = jnp.float32(3.0e38)
    is_g = lane < N_GROUPS
    gl = jnp.where(is_g, logits, -big)
    mg = jnp.max(gl, axis=-1, keepdims=True)
    grp = jnp.min(jnp.where(gl == mg, lane, big), axis=-1, keepdims=True)
    pg = 1.0 / jnp.sum(jnp.where(is_g, jnp.exp(gl - mg), 0.0), axis=-1, keepdims=True)
    e_lo = N_GROUPS + EXPERTS_PER_GROUP * grp
    sel = (lane >= e_lo) & (lane < e_lo + EXPERTS_PER_GROUP)
    el = jnp.where(sel, logits, -big)
    v1 = jnp.max(el, axis=-1, keepdims=True)
    i1 = jnp.min(jnp.where(el == v1, lane, big), axis=-1, keepdims=True)
    el2 = jnp.where(lane == i1, -big, el)
    v2 = jnp.max(el2, axis=-1, keepdims=True)
    i2 = jnp.min(jnp.where(el2 == v2, lane, big), axis=-1, keepdims=True)
    e21 = jnp.exp(v2 - v1)
    w1 = pg / (1.0 + e21)
    w2 = pg * e21 / (1.0 + e21)
    a = jnp.minimum(i1, i2) - e_lo
    b = jnp.maximum(i1, i2) - e_lo
    cls = grp * PAIRS_PER_GROUP + (a * (7.0 - a) * 0.5 + (b - a - 1.0))
    gates = jnp.where(lane == i1, w1, jnp.where(lane == i2, w2, jnp.where(lane == CLASS_LANE, cls, 0.0)))
    h_ref[pl.ds(GATE_ROW, tm, stride=SUBLANES), :] = lax.bitcast_convert_type(gates, jnp.uint32)
    cls_ref[...] = jnp.broadcast_to(cls, (tm, LANES)).astype(jnp.int32)


def _out_proj(oa, ob, oc, x2d, ga, gb, gc, wa, wb, wc, ln2, wr, br):
    n = x2d.shape[0]
    tm = TM_PROJ
    row = lambda i: (i, 0)
    fixed = lambda i: (0, 0)
    return pl.pallas_call(
        _out_proj_kernel,
        grid=(n // tm,),
        in_specs=[
            pl.BlockSpec((tm, A_WIDTH), row),
            pl.BlockSpec((tm, B_WIDTH), row),
            pl.BlockSpec((tm, C_WIDTH), row),
            pl.BlockSpec((tm, D_MODEL), row),
            pl.BlockSpec((1, A_WIDTH), fixed),
            pl.BlockSpec((1, B_WIDTH), fixed),
            pl.BlockSpec((1, C_WIDTH), fixed),
            pl.BlockSpec((A_WIDTH, D_MODEL), fixed),
            pl.BlockSpec((B_WIDTH, D_MODEL), fixed),
            pl.BlockSpec((C_WIDTH, D_MODEL), fixed),
            pl.BlockSpec((1, D_MODEL), fixed),
            pl.BlockSpec((D_MODEL, 2 * LANES), fixed),
            pl.BlockSpec((1, LANES), fixed),
        ],
        out_specs=[
            pl.BlockSpec((tm, D_MODEL), row),
            pl.BlockSpec((tm * SUBLANES, LANES), row),
            pl.BlockSpec((tm, LANES), row),
        ],
        out_shape=[
            jax.ShapeDtypeStruct((n, D_MODEL), F32),
            jax.ShapeDtypeStruct((n * SUBLANES, LANES), jnp.uint32),
            jax.ShapeDtypeStruct((n, LANES), jnp.int32),
        ],
        compiler_params=_cparams(("parallel",)),
        name="out_proj_router",
    )(oa, ob, oc, x2d, ga, gb, gc, wa, wb, wc, ln2, wr, br)


def _tile_copy(src_hbm, buf, sem, slot, src_row8, dst_tok):
    src = src_hbm.at[pl.ds(pl.multiple_of(src_row8, SUBLANES), SUBLANES), :]
    return pltpu.make_async_copy(src, buf[slot].at[pl.ds(SUBLANES * dst_tok, SUBLANES), :], sem.at[slot])


def _issue_gather(idx_ref, src_hbm, buf, sem, slot, toks):
    for r in range(toks):
        _tile_copy(src_hbm, buf, sem, slot, idx_ref[0, 0, r], r).start(priority=r % DMA_THREADS)


def _wait_gather(src_hbm, buf, sem, slot, toks):
    pltpu.make_async_copy(src_hbm.at[pl.ds(0, SUBLANES * toks), :], buf[slot], sem.at[slot]).wait()


def _tile_row(ref, s, toks):
    return ref[pl.ds(s, toks, stride=SUBLANES), :]


def _dispatch_kernel(ztile_ref, nvalid_ref, pos_ref, h_ref, o_hbm, zbuf, sem):
    tm = TM_DISP
    tile_rows = TM_MOE * SUBLANES
    i = pl.program_id(0)

    @pl.when(i == 0)
    def _():
        zbuf[...] = jnp.zeros(zbuf.shape, zbuf.dtype)

        def zero_tile(row0):
            dst = o_hbm.at[pl.ds(pl.multiple_of(row0, SUBLANES), tile_rows), :]
            return pltpu.make_async_copy(zbuf, dst, sem.at[1])

        for c in range(N_CLASSES):
            @pl.when(ztile_ref[c] >= 0)
            def _(c=c):
                zero_tile(ztile_ref[c]).start()

        n_tiles = o_hbm.shape[0] // tile_rows

        @pl.loop(nvalid_ref[0], n_tiles)
        def _(t):
            zero_tile(t * tile_rows).start()

        for c in range(N_CLASSES):
            @pl.when(ztile_ref[c] >= 0)
            def _(c=c):
                zero_tile(ztile_ref[c]).wait()

        @pl.loop(nvalid_ref[0], n_tiles)
        def _(t):
            zero_tile(t * tile_rows).wait()

    def row_copy(r):
        dst = o_hbm.at[pl.ds(pl.multiple_of(pos_ref[0, 0, r], SUBLANES), SUBLANES), :]
        return pltpu.make_async_copy(h_ref.at[pl.ds(SUBLANES * r, SUBLANES), :], dst, sem.at[0])

    for r in range(tm):
        row_copy(r).start(priority=r % DMA_THREADS)
    pltpu.make_async_copy(h_ref, o_hbm.at[pl.ds(0, tm * SUBLANES), :], sem.at[0]).wait()


def _dispatch(hrow, ztile, nvalid, pos_tiles, nt):
    n = hrow.shape[0] // SUBLANES
    tm = TM_DISP
    grid_spec = pltpu.PrefetchScalarGridSpec(
        num_scalar_prefetch=2,
        grid=(n // tm,),
        in_specs=[
            pl.BlockSpec((1, 1, tm), lambda i, zt, nv: (i, 0, 0), memory_space=pltpu.SMEM),
            pl.BlockSpec((tm * SUBLANES, LANES), lambda i, zt, nv: (i, 0)),
        ],
        out_specs=pl.BlockSpec(memory_space=pl.ANY),
        scratch_shapes=[pltpu.VMEM((TM_MOE * SUBLANES, LANES), jnp.uint32), pltpu.SemaphoreType.DMA((2,))],
    )
    return pl.pallas_call(
        _dispatch_kernel,
        grid_spec=grid_spec,
        out_shape=jax.ShapeDtypeStruct((nt * TM_MOE * SUBLANES, LANES), jnp.uint32),
        compiler_params=_cparams(("arbitrary",)),
        name="moe_dispatch",
    )(ztile, nvalid, pos_tiles, hrow)


def _moe_kernel(ea_ref, eb_ref, nvalid_ref, h_ref, wga_ref, wua_ref, wda_ref, wgb_ref, wub_ref, wdb_ref, y_ref):
    tm = TM_MOE
    i = pl.program_id(0)
    nvalid = nvalid_ref[0]

    @pl.when(i < nvalid)
    def _():
        parts_lo, parts_hi = [], []
        for s in range(D_MODEL // 2 // LANES):
            w = _tile_row(h_ref, s, tm)
            parts_lo.append(lax.bitcast_convert_type(w << 16, F32).astype(BF16))
            parts_hi.append(lax.bitcast_convert_type(w & jnp.uint32(0xFFFF0000), F32).astype(BF16))
        x = jnp.concatenate(parts_lo + parts_hi, axis=1)
        gates = lax.bitcast_convert_type(_tile_row(h_ref, GATE_ROW, tm), F32)
        lane = lax.broadcasted_iota(jnp.int32, gates.shape, 1)

        def expert(wg_ref, wu_ref, wd_ref, e):
            g = jnp.dot(x, wg_ref[0], preferred_element_type=F32)
            u = jnp.dot(x, wu_ref[0], preferred_element_type=F32)
            act = (g / (1.0 + jnp.exp(-g)) * u).astype(BF16)
            y = jnp.dot(act, wd_ref[0], preferred_element_type=F32)
            ge = jnp.sum(jnp.where(lane == e + N_GROUPS, gates, 0.0), axis=-1, keepdims=True)
            return ge * y

        y = expert(wga_ref, wua_ref, wda_ref, ea_ref[i]) + expert(wgb_ref, wub_ref, wdb_ref, eb_ref[i])
        for s in range(SUBLANES):
            y_ref[pl.ds(s, tm, stride=SUBLANES), :] = y[:, LANES * s:LANES * (s + 1)]

    @pl.when(i >= nvalid)
    def _():
        y_ref[...] = jnp.zeros(y_ref.shape, F32)


def _moe_experts(h_sorted, ea, eb, nvalid, wg, wu, wd):
    tm = TM_MOE
    nt = ea.shape[0]
    w_in_spec = lambda sel: pl.BlockSpec((1, D_MODEL, D_EXPERT), sel)
    w_out_spec = lambda sel: pl.BlockSpec((1, D_EXPERT, D_MODEL), sel)
    sel_a = lambda i, ea, eb, nv: (ea[i], 0, 0)
    sel_b = lambda i, ea, eb, nv: (eb[i], 0, 0)
    rows = lambda i, ea, eb, nv: (jnp.minimum(i, nv[0] - 1), 0)
    grid_spec = pltpu.PrefetchScalarGridSpec(
        num_scalar_prefetch=3,
        grid=(nt,),
        in_specs=[
            pl.BlockSpec((tm * SUBLANES, LANES), rows),
            w_in_spec(sel_a), w_in_spec(sel_a), w_out_spec(sel_a),
            w_in_spec(sel_b), w_in_spec(sel_b), w_out_spec(sel_b),
        ],
        out_specs=pl.BlockSpec((tm * SUBLANES, LANES), lambda i, ea, eb, nv: (i, 0)),
    )
    return pl.pallas_call(
        _moe_kernel,
        grid_spec=grid_spec,
        out_shape=jax.ShapeDtypeStruct((nt * tm * SUBLANES, LANES), F32),
        compiler_params=_cparams(("arbitrary",)),
        name="moe_experts",
    )(ea, eb, nvalid, h_sorted, wg, wu, wd, wg, wu, wd)


def _combine_kernel(pos0_ref, pos1_ref, posn_ref, xn_ref, y_hbm, o_ref, buf0, buf1, buf2, sem):
    buf = (buf0, buf1, buf2)
    tm = TM_COMB
    i = pl.program_id(0)

    @pl.when(i == 0)
    def _():
        _issue_gather(pos0_ref, y_hbm, buf, sem, 0, tm)
        _issue_gather(pos1_ref, y_hbm, buf, sem, 1, tm)

    for slot in range(GATHER_SLOTS):
        @pl.when(i % GATHER_SLOTS == slot)
        def _(slot=slot):
            _wait_gather(y_hbm, buf, sem, slot, tm)
            _issue_gather(posn_ref, y_hbm, buf, sem, (slot + 2) % GATHER_SLOTS, tm)
            for s in range(SUBLANES):
                cols = slice(LANES * s, LANES * (s + 1))
                o_ref[:, cols] = xn_ref[:, cols] + _tile_row(buf[slot], s, tm)

            @pl.when(i == pl.num_programs(0) - 1)
            def _():
                _wait_gather(y_hbm, buf, sem, (slot + 1) % GATHER_SLOTS, tm)
                _wait_gather(y_hbm, buf, sem, (slot + 2) % GATHER_SLOTS, tm)


def _combine(pos_tiles, xn, y_sorted):
    n = xn.shape[0]
    tm = TM_COMB
    smem_blk = lambda f: pl.BlockSpec((1, 1, tm), f, memory_space=pltpu.SMEM)
    return pl.pallas_call(
        _combine_kernel,
        grid=(n // tm,),
        in_specs=[
            smem_blk(lambda i: (0, 0, 0)),
            smem_blk(lambda i: (1, 0, 0)),
            smem_blk(lambda i: (i + 2, 0, 0)),
            pl.BlockSpec((tm, D_MODEL), lambda i: (i, 0)),
            pl.BlockSpec(memory_space=pl.ANY),
        ],
        out_specs=pl.BlockSpec((tm, D_MODEL), lambda i: (i, 0)),
        out_shape=jax.ShapeDtypeStruct((n, D_MODEL), F32),
        scratch_shapes=[pltpu.VMEM((tm * SUBLANES, LANES), F32) for _ in range(GATHER_SLOTS)]
        + [pltpu.SemaphoreType.DMA((GATHER_SLOTS,))],
        compiler_params=_cparams(("arbitrary",)),
        name="moe_combine",
    )(pos_tiles, pos_tiles, pos_tiles, xn, y_sorted)


def _moe_plan(cls, n):
    tm = TM_MOE
    nt = n // tm + N_CLASSES
    onehot = (cls[:, None] == jnp.arange(N_CLASSES, dtype=jnp.int32)[None, :]).astype(jnp.int32)
    counts = jnp.sum(onehot, axis=0)
    rank = jnp.sum(jnp.cumsum(onehot, axis=0) * onehot, axis=1) - 1
    tiles = (counts + tm - 1) // tm
    tile_end = jnp.cumsum(tiles)
    tile_start = tile_end - tiles
    pos = jnp.sum(onehot * tile_start[None, :], axis=1) * tm + rank
    pad = GATHER_SLOTS - 1
    ztile = jnp.where(tiles > 0, (tile_end - 1) * (tm * SUBLANES), -1).astype(jnp.int32)
    nvalid = tile_end[-1]
    tile_id = jnp.minimum(jnp.arange(nt, dtype=jnp.int32), nvalid - 1)
    tcls = jnp.sum((tile_id[:, None] >= tile_end[None, :]).astype(jnp.int32), axis=1)
    grp, pair = tcls // PAIRS_PER_GROUP, tcls % PAIRS_PER_GROUP
    pair_a = jnp.asarray(PAIR_A, jnp.int32)
    pair_b = jnp.asarray(PAIR_B, jnp.int32)
    ea = grp * EXPERTS_PER_GROUP + jnp.sum((pair[:, None] == jnp.arange(PAIRS_PER_GROUP)[None]) * pair_a[None], axis=1)
    eb = grp * EXPERTS_PER_GROUP + jnp.sum((pair[:, None] == jnp.arange(PAIRS_PER_GROUP)[None]) * pair_b[None], axis=1)
    pos8 = SUBLANES * pos
    pos_disp = pos8.reshape(n // TM_DISP, 1, TM_DISP)
    pos_comb = jnp.concatenate([pos8, jnp.zeros((pad * TM_COMB,), jnp.int32)]).reshape(n // TM_COMB + pad, 1, TM_COMB)
    return (ea.astype(jnp.int32), eb.astype(jnp.int32), nvalid.reshape(1).astype(jnp.int32), ztile,
            pos_disp, pos_comb)


def _moe(hrow, cls, xn, wg, wu, wd):
    n = xn.shape[0]
    ea, eb, nvalid, ztile, pos_disp, pos_comb = _moe_plan(cls[:, 0], n)
    h_sorted = _dispatch(hrow, ztile, nvalid, pos_disp, ea.shape[0])
    y_sorted = _moe_experts(h_sorted, ea, eb, nvalid, wg, wu, wd)
    return _combine(pos_comb, xn, y_sorted)


def _head_cols(base, heads, swap=False):
    inner = _ROPE_SWAP if swap else np.arange(HEAD_DIM)
    return np.concatenate([base + HEAD_DIM * h + inner for h in heads])


def _proj_columns():
    o_qa, o_ka, o_va = 0, A_WIDTH, A_WIDTH + KV_WIDTH
    o_qb = o_va + KV_WIDTH
    o_kb, o_vb = o_qb + B_WIDTH, o_qb + 2 * B_WIDTH
    o_qc = o_vb + B_WIDTH
    o_kc, o_vc = o_qc + C_WIDTH, o_qc + C_WIDTH + KV_WIDTH
    nat2, nat4 = range(2), range(4)
    return np.concatenate([
        _head_cols(o_qa, PAIR_ORDER), _head_cols(o_ka, nat2),
        _head_cols(o_qb, nat4), _head_cols(o_kb, nat4),
        _head_cols(o_qc, PAIR_ORDER), _head_cols(o_kc, nat2),
        _head_cols(o_qc, PAIR_ORDER, swap=True), _head_cols(o_kc, nat2, swap=True),
        _head_cols(o_va, nat2), _head_cols(o_vb, nat4), _head_cols(o_vc, nat2),
    ])


def _rope_tables():
    nf = HEAD_DIM // 4
    inv = ROPE_THETA ** (-jnp.arange(nf, dtype=F32) / nf)
    pos = jnp.arange(SEQ)
    ang_r = (pos // GRID_W).astype(F32)[:, None] * inv[None, :]
    ang_c = (pos % GRID_W).astype(F32)[:, None] * inv[None, :]
    cos = jnp.concatenate([jnp.cos(ang_r)] * 2 + [jnp.cos(ang_c)] * 2, axis=-1)
    sin = jnp.concatenate([-jnp.sin(ang_r), jnp.sin(ang_r), -jnp.sin(ang_c), jnp.sin(ang_c)], axis=-1)
    return jnp.concatenate([cos, cos], axis=-1), jnp.concatenate([sin, sin], axis=-1)


def _layer_params(l, w_in, ln1, qk_gain, sink, rpb, out_gain, w_out, ln2, w_rg, b_rg, w_re, b_re):
    cols = _proj_columns()
    w_ext = w_in[l][:, cols].astype(BF16)
    qs = HEAD_DIM ** -0.5 * LOG2E
    g = qk_gain[l].astype(F32)
    sw = _ROPE_SWAP
    gain = jnp.concatenate([
        jnp.tile(g[0, 0], A_HEADS) * qs, jnp.tile(g[0, 1], A_KV),
        jnp.tile(g[1, 0], B_HEADS) * qs, jnp.tile(g[1, 1], B_HEADS),
        jnp.tile(g[2, 0], C_HEADS) * qs, jnp.tile(g[2, 1], C_KV),
        jnp.tile(g[2, 0][sw], C_HEADS) * qs, jnp.tile(g[2, 1][sw], C_KV),
    ])[None, :]
    order = list(PAIR_ORDER)
    sink2 = sink[l].astype(F32)[jnp.asarray(order)] * LOG2E
    rows_a = _head_cols(0, PAIR_ORDER)
    rows_b = A_WIDTH + np.arange(B_WIDTH)
    rows_c = _head_cols(A_WIDTH + B_WIDTH, PAIR_ORDER)
    og = out_gain[l].astype(F32)
    wo = w_out[l]
    wr = jnp.concatenate([w_rg[l], w_re[l], jnp.zeros((D_MODEL, LANES - N_GROUPS - N_EXPERTS), F32)], axis=1)
    wr_hi = wr.astype(BF16)
    wr_lo = (wr - wr_hi.astype(F32)).astype(BF16)
    br = jnp.concatenate([b_rg[l].astype(F32), b_re[l].astype(F32),
                          jnp.zeros((LANES - N_GROUPS - N_EXPERTS,), F32)])[None, :]
    return dict(
        w_ext=w_ext, ln1=ln1[l][None, :], gain=gain, sink2=sink2, nbr_bias=_nbr_bias_table(rpb[l]),
        ga=og[rows_a][None, :], gb=og[rows_b][None, :], gc=og[rows_c][None, :],
        wa=wo[rows_a].astype(BF16), wb=wo[rows_b].astype(BF16), wc=wo[rows_c].astype(BF16),
        ln2=ln2[l][None, :], wr=jnp.concatenate([wr_hi, wr_lo], axis=1), br=br,
    )


def _block_ones():
    idx = np.arange(MXU_DIM) // HEAD_DIM
    return jnp.asarray((idx[:, None] == idx[None, :]).astype(np.float32)).astype(BF16)


def _trunk(x, params, shared, moe_w):
    b = x.shape[0]
    x2d = x.reshape(b * SEQ, D_MODEL)
    for l, lp in enumerate(params):
        qa, ka, va, qb, kb, vb, qc, kc, vc = _in_proj(
            x2d, lp["w_ext"], lp["ln1"], lp["gain"], shared["cos"], shared["sin"], shared["ones"])
        seq = lambda z: z.reshape(b, SEQ, z.shape[-1])
        oa = _win_attn(seq(qa), seq(ka), seq(va), shared["win_bias"], lp["sink2"])
        ob = _nbr_attn(seq(qb), seq(kb), seq(vb), lp["nbr_bias"])
        oc = _dense_attn(seq(qc), seq(kc), seq(vc))
        flat = lambda z: z.reshape(b * SEQ, z.shape[-1])
        xn, hrow, cls = _out_proj(flat(oa), flat(ob), flat(oc), x2d, lp["ga"], lp["gb"], lp["gc"],
                             lp["wa"], lp["wb"], lp["wc"], lp["ln2"], lp["wr"], lp["br"])
        wg, wu, wd = moe_w[l]
        x2d = _moe(hrow, cls, xn, wg, wu, wd)
    return x2d.reshape(b, SEQ, D_MODEL)


def kernel(x_prompt, x_sample, ln1, w_in, qk_gain, sink, rpb, out_gain, w_out, ln2, w_router_group,
           b_router_group, w_router_expert, b_router_expert, w_gate, w_up, w_down):
    depth = w_in.shape[0]
    params = [_layer_params(l, w_in, ln1, qk_gain, sink, rpb, out_gain, w_out, ln2, w_router_group,
                            b_router_group, w_router_expert, b_router_expert) for l in range(depth)]
    cos_t, sin_t = _rope_tables()
    shared = dict(cos=cos_t, sin=sin_t, ones=_block_ones(), win_bias=_win_bias_table())
    moe_w = [(w_gate[l].astype(BF16), w_up[l].astype(BF16), w_down[l].astype(BF16)) for l in range(depth)]
    y_prompt = _trunk(x_prompt, params, shared, moe_w)
    y_sample = _trunk(x_sample, params, shared, moe_w)
    return (y_prompt, y_sample)
```

```python
import functools
import math

import numpy as np
import jax
import jax.numpy as jnp
from jax import lax
from jax.experimental import pallas as pl
from jax.experimental.pallas import tpu as pltpu

D_MODEL = 1024
SEQ = 4096
HEAD_DIM = 64
GRID_W = 64
GRID_ROWS = SEQ // GRID_W
A_HEADS, A_KV, A_WINDOW = 6, 2, 128
B_HEADS, B_ROWS, B_COLS = 4, 8, 16
C_HEADS, C_KV = 6, 2
ROPE_THETA = 10000.0
N_GROUPS, EXPERTS_PER_GROUP = 4, 4
N_EXPERTS = N_GROUPS * EXPERTS_PER_GROUP
D_EXPERT = D_MODEL // 2
EPS = 1e-6
NEG = -1e30
LOG2E = math.log2(math.e)

A_WIDTH = A_HEADS * HEAD_DIM
B_WIDTH = B_HEADS * HEAD_DIM
C_WIDTH = C_HEADS * HEAD_DIM
KV_WIDTH = A_KV * HEAD_DIM

LANES = 128
MXU_DIM = 256
VMEM_LIMIT = 56 * 1024 * 1024

PAIR_ORDER = (0, 3, 1, 4, 2, 5)

TM_PROJ = 512
LOCAL_BATCH = 4
TQ_A = 256
KW_A = 512
TQ_B = 256
KW_B = 768
TQ_C = 512
KC_C = 512
TM_MOE = 256
TM_COMB = 512
TM_DISP = 512

PAIR_A = (0, 0, 0, 1, 1, 2)
PAIR_B = (1, 2, 3, 2, 3, 3)
PAIRS_PER_GROUP = len(PAIR_A)
N_CLASSES = N_GROUPS * PAIRS_PER_GROUP
CLASS_LANE = N_GROUPS + N_EXPERTS
SUBLANES = 8
GATE_ROW = D_MODEL // 2 // LANES
GATHER_SLOTS = 3
DMA_THREADS = 2

BF16 = jnp.bfloat16
F32 = jnp.float32


def _cparams(sem):
    return pltpu.CompilerParams(dimension_semantics=sem, vmem_limit_bytes=VMEM_LIMIT)


N_NORMED = 1536
N_PROJ = 2048


def _in_proj_kernel(x_ref, w_ref, ln_ref, gain_ref, cos_ref, sin_ref, ones_ref,
                    qa_ref, ka_ref, va_ref, qb_ref, kb_ref, vb_ref, qc_ref, kc_ref, vc_ref):
    x = x_ref[...]
    ms = jnp.mean(x * x, axis=-1, keepdims=True)
    h = (x * lax.rsqrt(ms + EPS) * ln_ref[...]).astype(BF16)
    p = jnp.dot(h, w_ref[...], preferred_element_type=F32)
    g = gain_ref[...]
    ones_blk = ones_ref[...]

    def chunk(c):
        return p[:, MXU_DIM * c:MXU_DIM * (c + 1)]

    def inv_rms(pc):
        ss = jnp.dot((pc * pc).astype(BF16), ones_blk, preferred_element_type=F32)
        return lax.rsqrt(ss * (1.0 / HEAD_DIM) + EPS)

    r = [inv_rms(chunk(c)) for c in range(N_NORMED // MXU_DIM)]
    pn = [chunk(c) * r[c] * g[:, MXU_DIM * c:MXU_DIM * (c + 1)] for c in range(N_NORMED // MXU_DIM)]
    qa_ref[:, 0:256] = pn[0].astype(BF16)
    qa_ref[:, 256:384] = pn[1][:, :128].astype(BF16)
    ka_ref[...] = pn[1][:, 128:].astype(BF16)
    qb_ref[...] = pn[2].astype(BF16)
    kb_ref[...] = pn[3].astype(BF16)
    cos = cos_ref[...]
    sin = sin_ref[...]
    cos2 = jnp.concatenate([cos, cos], axis=1)
    sin2 = jnp.concatenate([sin, sin], axis=1)
    lane = lax.broadcasted_iota(jnp.int32, (1, MXU_DIM), 1)
    first_half = (lane % (HEAD_DIM // 2)) < HEAD_DIM // 4

    def rotary_partner(z):
        return jnp.where(first_half, pltpu.roll(z, MXU_DIM - HEAD_DIM // 4, axis=1), pltpu.roll(z, HEAD_DIM // 4, axis=1))

    c4 = pn[4] * cos2 + rotary_partner(pn[4]) * sin2
    c5 = pn[5] * cos2 + rotary_partner(pn[5]) * sin2
    qc_ref[:, 0:256] = c4.astype(BF16)
    qc_ref[:, 256:384] = c5[:, :128].astype(BF16)
    kc_ref[...] = c5[:, 128:].astype(BF16)
    va_ref[...] = p[:, 1536:1664].astype(BF16)
    vb_ref[...] = p[:, 1664:1920].astype(BF16)
    vc_ref[:, 0:128] = p[:, 1920:2048].astype(BF16)
    vc_ref[:, 128:256] = jnp.ones((x.shape[0], 128), BF16)


def _in_proj(x2d, w_ext, ln, gain, cos_t, sin_t, ones_blk):
    n = x2d.shape[0]
    tm = TM_PROJ
    tiles_per_seq = SEQ // tm
    row = lambda i: (i, 0)
    fixed = lambda i: (0, 0)
    pos = lambda i: (i % tiles_per_seq, 0)
    widths = (A_WIDTH, KV_WIDTH, KV_WIDTH, B_WIDTH, B_WIDTH, B_WIDTH, C_WIDTH, KV_WIDTH, 2 * KV_WIDTH)
    return pl.pallas_call(
        _in_proj_kernel,
        grid=(n // tm,),
        in_specs=[
            pl.BlockSpec((tm, D_MODEL), row),
            pl.BlockSpec((D_MODEL, N_PROJ), fixed),
            pl.BlockSpec((1, D_MODEL), fixed),
            pl.BlockSpec((1, N_NORMED), fixed),
            pl.BlockSpec((tm, LANES), pos),
            pl.BlockSpec((tm, LANES), pos),
            pl.BlockSpec((MXU_DIM, MXU_DIM), fixed),
        ],
        out_specs=[pl.BlockSpec((tm, w), row) for w in widths],
        out_shape=[jax.ShapeDtypeStruct((n, w), BF16) for w in widths],
        compiler_params=_cparams(("parallel",)),
        name="in_proj",
    )(x2d, w_ext, ln, gain, cos_t, sin_t, ones_blk)


def _half_masks():
    lane = lax.broadcasted_iota(jnp.int32, (1, LANES), 1)
    lo = lane < HEAD_DIM
    return lo, lo.astype(BF16), (~lo).astype(BF16)


def _win_attn_kernel(sink_ref, q_ref, k_ref, v_ref, bias_ref, o_ref):
    j = pl.program_id(1)
    start = pl.multiple_of(jnp.clip(j * TQ_A - A_WINDOW, 0, SEQ - KW_A), 128)
    lo, m_lo, m_hi = _half_masks()
    low_rows = lax.broadcasted_iota(jnp.int32, (2 * TQ_A, 1), 0) < TQ_A
    for bb in range(LOCAL_BATCH):
        kw = k_ref[bb, pl.ds(start, KW_A), :]
        vw = v_ref[bb, pl.ds(start, KW_A), :]
        for p in range(A_HEADS // 2):
            qblk = q_ref[bb, :, LANES * p:LANES * (p + 1)]
            q2 = jnp.concatenate([qblk * m_lo, qblk * m_hi], axis=0)
            s = lax.dot_general(q2, kw, (((1,), (1,)), ((), ())), preferred_element_type=F32)
            s = s + bias_ref[0, p]
            sk = jnp.where(low_rows, sink_ref[2 * p], sink_ref[2 * p + 1])
            m = jnp.maximum(jnp.max(s, axis=-1, keepdims=True), sk)
            e = jnp.exp2(s - m)
            l = jnp.sum(e, axis=-1, keepdims=True) + jnp.exp2(sk - m)
            o2 = jnp.dot(e.astype(BF16), vw, preferred_element_type=F32) / l
            o_ref[bb, :, LANES * p:LANES * (p + 1)] = jnp.where(lo, o2[:TQ_A], o2[TQ_A:]).astype(BF16)


def _win_attn(qa, ka, va, bias, sink2):
    b = qa.shape[0]
    nq = SEQ // TQ_A
    variant = lambda bi, j: (jnp.where(j == 0, 0, jnp.where(j == nq - 1, 2, 1)), 0, 0, 0)
    bb = LOCAL_BATCH
    return pl.pallas_call(
        _win_attn_kernel,
        grid=(b // bb, nq),
        in_specs=[
            pl.BlockSpec(memory_space=pltpu.SMEM),
            pl.BlockSpec((bb, TQ_A, A_WIDTH), lambda bi, j: (bi, j, 0)),
            pl.BlockSpec((bb, SEQ, KV_WIDTH), lambda bi, j: (bi, 0, 0)),
            pl.BlockSpec((bb, SEQ, KV_WIDTH), lambda bi, j: (bi, 0, 0)),
            pl.BlockSpec((1, A_HEADS // 2, 2 * TQ_A, KW_A), variant),
        ],
        out_specs=pl.BlockSpec((bb, TQ_A, A_WIDTH), lambda bi, j: (bi, j, 0)),
        out_shape=jax.ShapeDtypeStruct((b, SEQ, A_WIDTH), BF16),
        compiler_params=_cparams(("parallel", "arbitrary")),
        name="win_attn",
    )(sink2, qa, ka, va, bias)


def _win_bias_table():
    slopes = np.array([2.0 ** (-8.0 * (n + 1) / A_HEADS) for n in range(A_HEADS)], np.float32)[list(PAIR_ORDER)]
    i = np.arange(TQ_A)[:, None]
    jj = np.arange(KW_A)[None, :]
    tabs = []
    for off in (0, A_WINDOW, KW_A - TQ_A):
        dist = np.abs(off + i - jj).astype(np.float32)
        tab = np.where(dist[None] <= A_WINDOW, -slopes[:, None, None] * dist[None] * LOG2E, NEG)
        tabs.append(tab)
    return jnp.asarray(np.stack(tabs).astype(np.float32).reshape(3, A_HEADS // 2, 2 * TQ_A, KW_A))


def _nbr_attn_kernel(q_ref, k_ref, v_ref, bias_ref, o_ref):
    j = pl.program_id(1)
    rows_per_tile = TQ_B // GRID_W
    krow0 = jnp.clip(j * rows_per_tile - B_ROWS // 2, 0, GRID_ROWS - KW_B // GRID_W)
    start = pl.multiple_of(krow0 * GRID_W, 256)
    lo, m_lo, m_hi = _half_masks()
    for bb in range(LOCAL_BATCH):
        for p in range(B_HEADS // 2):
            qblk = q_ref[bb, :, LANES * p:LANES * (p + 1)]
            kw = k_ref[bb, pl.ds(start, KW_B), LANES * p:LANES * (p + 1)]
            vw = v_ref[bb, pl.ds(start, KW_B), LANES * p:LANES * (p + 1)]
            q2 = jnp.concatenate([qblk * m_lo, qblk * m_hi], axis=0)
            s = lax.dot_general(q2, kw, (((1,), (1,)), ((), ())), preferred_element_type=F32)
            s = s + bias_ref[0, p]
            m = jnp.max(s, axis=-1, keepdims=True)
            e = jnp.exp2(s - m)
            l = jnp.sum(e, axis=-1, keepdims=True)
            o2 = jnp.dot(e.astype(BF16), vw, preferred_element_type=F32) / l
            o_ref[bb, :, LANES * p:LANES * (p + 1)] = jnp.where(lo, o2[:TQ_B], o2[TQ_B:]).astype(BF16)


def _nbr_attn(qb, kb, vb, bias):
    b = qb.shape[0]
    nq = SEQ // TQ_B
    variant = lambda bi, j: (jnp.where(j == 0, 0, jnp.where(j == nq - 1, 2, 1)), 0, 0, 0)
    bb = LOCAL_BATCH
    return pl.pallas_call(
        _nbr_attn_kernel,
        grid=(b // bb, nq),
        in_specs=[
            pl.BlockSpec((bb, TQ_B, B_WIDTH), lambda bi, j: (bi, j, 0)),
            pl.BlockSpec((bb, SEQ, B_WIDTH), lambda bi, j: (bi, 0, 0)),
            pl.BlockSpec((bb, SEQ, B_WIDTH), lambda bi, j: (bi, 0, 0)),
            pl.BlockSpec((1, B_HEADS // 2, 2 * TQ_B, KW_B), variant),
        ],
        out_specs=pl.BlockSpec((bb, TQ_B, B_WIDTH), lambda bi, j: (bi, j, 0)),
        out_shape=jax.ShapeDtypeStruct((b, SEQ, B_WIDTH), BF16),
        compiler_params=_cparams(("parallel", "arbitrary")),
        name="nbr_attn",
    )(qb, kb, vb, bias)


def _nbr_bias_table(rpb):
    rows_per_tile = TQ_B // GRID_W
    krows = KW_B // GRID_W
    r = rpb.astype(F32) * LOG2E
    edge = GRID_W - B_COLS
    ext = jnp.concatenate([jnp.repeat(r[..., :1], edge, axis=-1), r, jnp.repeat(r[..., -1:], edge, axis=-1)], axis=-1)
    col = jnp.stack([ext[..., GRID_W - 1 - q:2 * GRID_W - 1 - q] for q in range(GRID_W)], axis=2)
    tabs = []
    for r_first in (0, rows_per_tile, GRID_ROWS - rows_per_tile):
        krow0 = int(np.clip(r_first - B_ROWS // 2, 0, GRID_ROWS - krows))
        slabs = []
        for ql in range(rows_per_tile):
            per_k = [col[:, int(np.clip(krow0 + kl - (r_first + ql) + B_ROWS - 1, 0, 2 * B_ROWS - 2))]
                     for kl in range(krows)]
            slabs.append(jnp.stack(per_k, axis=2))
        tab = jnp.stack(slabs, axis=1).reshape(B_HEADS, TQ_B, KW_B)
        qi = np.arange(TQ_B)
        kj = np.arange(KW_B)
        qr = (r_first + qi // GRID_W)[:, None]
        qcol = (qi % GRID_W)[:, None]
        kr = (krow0 + kj // GRID_W)[None, :]
        kcol = (kj % GRID_W)[None, :]
        r0 = np.clip(qr - B_ROWS // 2, 0, GRID_ROWS - B_ROWS)
        c0 = np.clip(qcol - B_COLS // 2, 0, GRID_W - B_COLS)
        valid = (kr >= r0) & (kr < r0 + B_ROWS) & (kcol >= c0) & (kcol < c0 + B_COLS)
        tabs.append(jnp.where(jnp.asarray(valid)[None], tab, NEG))
    return jnp.stack(tabs).reshape(3, B_HEADS // 2, 2 * TQ_B, KW_B)


def _dense_attn_kernel(q_ref, k_ref, v_ref, o_ref, s_ref):
    lo, m_lo, m_hi = _half_masks()
    nchunk = SEQ // KC_C
    npairs = C_HEADS // 2

    def scores(p):
        qblk = q_ref[0, :, LANES * p:LANES * (p + 1)]
        q2 = jnp.concatenate([qblk * m_lo, qblk * m_hi], axis=0)
        m_run = None
        for c in range(nchunk):
            kc = k_ref[0, KC_C * c:KC_C * (c + 1), :]
            s = lax.dot_general(q2, kc, (((1,), (1,)), ((), ())), preferred_element_type=F32)
            s_ref[p % 2, :, KC_C * c:KC_C * (c + 1)] = s
            for t in range(KC_C // LANES):
                blk = s[:, LANES * t:LANES * (t + 1)]
                m_run = blk if m_run is None else jnp.maximum(m_run, blk)
        return jnp.max(m_run, axis=-1, keepdims=True)

    def weighted_values(p, m):
        acc = None
        for c in range(nchunk):
            e = jnp.exp2(s_ref[p % 2, :, KC_C * c:KC_C * (c + 1)] - m).astype(BF16)
            part = jnp.dot(e, v_ref[0, KC_C * c:KC_C * (c + 1), :], preferred_element_type=F32)
            acc = part if acc is None else acc + part
        on = acc[:, :LANES] / acc[:, LANES:]
        o_ref[0, :, LANES * p:LANES * (p + 1)] = jnp.where(lo, on[:TQ_C], on[TQ_C:]).astype(BF16)

    m_next = scores(0)
    for p in range(npairs):
        m_cur = m_next
        if p + 1 < npairs:
            m_next = scores(p + 1)
        weighted_values(p, m_cur)


def _dense_attn(qc, kc, vc):
    b = qc.shape[0]
    return pl.pallas_call(
        _dense_attn_kernel,
        grid=(b, SEQ // TQ_C),
        in_specs=[
            pl.BlockSpec((1, TQ_C, C_WIDTH), lambda bi, j: (bi, j, 0)),
            pl.BlockSpec((1, SEQ, KV_WIDTH), lambda bi, j: (bi, 0, 0)),
            pl.BlockSpec((1, SEQ, 2 * KV_WIDTH), lambda bi, j: (bi, 0, 0)),
        ],
        out_specs=pl.BlockSpec((1, TQ_C, C_WIDTH), lambda bi, j: (bi, j, 0)),
        out_shape=jax.ShapeDtypeStruct((b, SEQ, C_WIDTH), BF16),
        scratch_shapes=[pltpu.VMEM((2, 2 * TQ_C, SEQ), F32)],
        compiler_params=_cparams(("parallel", "arbitrary")),
        name="dense_attn",
    )(qc, kc, vc)


def _out_proj_kernel(oa_ref, ob_ref, oc_ref, x_ref, ga_ref, gb_ref, gc_ref, wa_ref, wb_ref, wc_ref,
                     ln_ref, wr_ref, br_ref, xn_ref, h_ref, cls_ref):
    def nrm(o_ref, g_ref):
        o = o_ref[...].astype(F32)
        ms = jnp.mean(o * o, axis=-1, keepdims=True)
        return (o * lax.rsqrt(ms + EPS) * g_ref[...]).astype(BF16)

    acc = jnp.dot(nrm(oa_ref, ga_ref), wa_ref[...], preferred_element_type=F32)
    acc = acc + jnp.dot(nrm(ob_ref, gb_ref), wb_ref[...], preferred_element_type=F32)
    acc = acc + jnp.dot(nrm(oc_ref, gc_ref), wc_ref[...], preferred_element_type=F32)
    xn = x_ref[...] + acc
    xn_ref[...] = xn
    ms = jnp.mean(xn * xn, axis=-1, keepdims=True)
    h2 = xn * lax.rsqrt(ms + EPS) * ln_ref[...]
    hi = h2.astype(BF16)
    lo = (h2 - hi.astype(F32)).astype(BF16)
    tm = xn.shape[0]
    half = D_MODEL // 2
    bits_lo = lax.bitcast_convert_type(hi[:, :half].astype(F32), jnp.uint32)
    bits_hi = lax.bitcast_convert_type(hi[:, half:].astype(F32), jnp.uint32)
    words = (bits_hi & jnp.uint32(0xFFFF0000)) | (bits_lo >> 16)
    for s in range(GATE_ROW):
        h_ref[pl.ds(s, tm, stride=SUBLANES), :] = words[:, LANES * s:LANES * (s + 1)]
    for s in range(GATE_ROW + 1, SUBLANES):
        h_ref[pl.ds(s, tm, stride=SUBLANES), :] = jnp.zeros((tm, LANES), jnp.uint32)
    wr = wr_ref[...]
    t = jnp.dot(hi, wr, preferred_element_type=F32)
    u = jnp.dot(lo, wr[:, :LANES], preferred_element_type=F32)
    logits = t[:, :LANES] + t[:, LANES:] + u + br_ref[...]

    lane = lax.broadcasted_iota(jnp.int32, logits.shape, 1).astype(F32)
    big = jnp.float32(3.0e38)
    is_g = lane < N_GROUPS
    gl = jnp.where(is_g, logits, -big)
    mg = jnp.max(gl, axis=-1, keepdims=True)
    grp = jnp.min(jnp.where(gl == mg, lane, big), axis=-1, keepdims=True)
    pg = 1.0 / jnp.sum(jnp.where(is_g, jnp.exp(gl - mg), 0.0), axis=-1, keepdims=True)
    e_lo = N_GROUPS + EXPERTS_PER_GROUP * grp
    sel = (lane >= e_lo) & (lane < e_lo + EXPERTS_PER_GROUP)
    el = jnp.where(sel, logits, -big)
    v1 = jnp.max(el, axis=-1, keepdims=True)
    i1 = jnp.min(jnp.where(el == v1, lane, big), axis=-1, keepdims=True)
    el2 = jnp.where(lane == i1, -big, el)
    v2 = jnp.max(el2, axis=-1, keepdims=True)
    i2 = jnp.min(jnp.where(el2 == v2, lane, big), axis=-1, keepdims=True)
    e21 = jnp.exp(v2 - v1)
    w1 = pg / (1.0 + e21)
    w2 = pg * e21 / (1.0 + e21)
    a = jnp.minimum(i1, i2) - e_lo
    b = jnp.maximum(i1, i2) - e_lo
    cls = grp * PAIRS_PER_GROUP + (a * (7.0 - a) * 0.5 + (b - a - 1.0))
    gates = jnp.where(lane == i1, w1, jnp.where(lane == i2, w2, jnp.where(lane == CLASS_LANE, cls, 0.0)))
    h_ref[pl.ds(GATE_ROW, tm, stride=SUBLANES), :] = lax.bitcast_convert_type(gates, jnp.uint32)
    cls_ref[...] = jnp.broadcast_to(cls, (tm, LANES)).astype(jnp.int32)


def _out_proj(oa, ob, oc, x2d, ga, gb, gc, wa, wb, wc, ln2, wr, br):
    n = x2d.shape[0]
    tm = TM_PROJ
    row = lambda i: (i, 0)
    fixed = lambda i: (0, 0)
    return pl.pallas_call(
        _out_proj_kernel,
        grid=(n // tm,),
        in_specs=[
            pl.BlockSpec((tm, A_WIDTH), row),
            pl.BlockSpec((tm, B_WIDTH), row),
            pl.BlockSpec((tm, C_WIDTH), row),
            pl.BlockSpec((tm, D_MODEL), row),
            pl.BlockSpec((1, A_WIDTH), fixed),
            pl.BlockSpec((1, B_WIDTH), fixed),
            pl.BlockSpec((1, C_WIDTH), fixed),
            pl.BlockSpec((A_WIDTH, D_MODEL), fixed),
            pl.BlockSpec((B_WIDTH, D_MODEL), fixed),
            pl.BlockSpec((C_WIDTH, D_MODEL), fixed),
            pl.BlockSpec((1, D_MODEL), fixed),
            pl.BlockSpec((D_MODEL, 2 * LANES), fixed),
            pl.BlockSpec((1, LANES), fixed),
        ],
        out_specs=[
            pl.BlockSpec((tm, D_MODEL), row),
            pl.BlockSpec((tm * SUBLANES, LANES), row),
            pl.BlockSpec((tm, LANES), row),
        ],
        out_shape=[
            jax.ShapeDtypeStruct((n, D_MODEL), F32),
            jax.ShapeDtypeStruct((n * SUBLANES, LANES), jnp.uint32),
            jax.ShapeDtypeStruct((n, LANES), jnp.int32),
        ],
        compiler_params=_cparams(("parallel",)),
        name="out_proj_router",
    )(oa, ob, oc, x2d, ga, gb, gc, wa, wb, wc, ln2, wr, br)


def _tile_copy(src_hbm, buf, sem, slot, src_row8, dst_tok):
    src = src_hbm.at[pl.ds(pl.multiple_of(src_row8, SUBLANES), SUBLANES), :]
    return pltpu.make_async_copy(src, buf[slot].at[pl.ds(SUBLANES * dst_tok, SUBLANES), :], sem.at[slot])


def _issue_gather(idx_ref, src_hbm, buf, sem, slot, toks):
    for r in range(toks):
        _tile_copy(src_hbm, buf, sem, slot, idx_ref[0, 0, r], r).start(priority=r % DMA_THREADS)


def _wait_gather(src_hbm, buf, sem, slot, toks):
    pltpu.make_async_copy(src_hbm.at[pl.ds(0, SUBLANES * toks), :], buf[slot], sem.at[slot]).wait()


def _tile_row(ref, s, toks):
    return ref[pl.ds(s, toks, stride=SUBLANES), :]


def _dispatch_kernel(ztile_ref, nvalid_ref, pos_ref, h_ref, o_hbm, zbuf, sem):
    tm = TM_DISP
    tile_rows = TM_MOE * SUBLANES
    i = pl.program_id(0)

    @pl.when(i == 0)
    def _():
        zbuf[...] = jnp.zeros(zbuf.shape, zbuf.dtype)

        def zero_tile(row0):
            dst = o_hbm.at[pl.ds(pl.multiple_of(row0, SUBLANES), tile_rows), :]
            return pltpu.make_async_copy(zbuf, dst, sem.at[1])

        for c in range(N_CLASSES):
            @pl.when(ztile_ref[c] >= 0)
            def _(c=c):
                zero_tile(ztile_ref[c]).start()

        n_tiles = o_hbm.shape[0] // tile_rows

        @pl.loop(nvalid_ref[0], n_tiles)
        def _(t):
            zero_tile(t * tile_rows).start()

        for c in range(N_CLASSES):
            @pl.when(ztile_ref[c] >= 0)
            def _(c=c):
                zero_tile(ztile_ref[c]).wait()

        @pl.loop(nvalid_ref[0], n_tiles)
        def _(t):
            zero_tile(t * tile_rows).wait()

    def row_copy(r):
        dst = o_hbm.at[pl.ds(pl.multiple_of(pos_ref[0, 0, r], SUBLANES), SUBLANES), :]
        return pltpu.make_async_copy(h_ref.at[pl.ds(SUBLANES * r, SUBLANES), :], dst, sem.at[0])

    for r in range(tm):
        row_copy(r).start(priority=r % DMA_THREADS)
    pltpu.make_async_copy(h_ref, o_hbm.at[pl.ds(0, tm * SUBLANES), :], sem.at[0]).wait()


def _dispatch(hrow, ztile, nvalid, pos_tiles, nt):
    n = hrow.shape[0] // SUBLANES
    tm = TM_DISP
    grid_spec = pltpu.PrefetchScalarGridSpec(
        num_scalar_prefetch=2,
        grid=(n // tm,),
        in_specs=[
            pl.BlockSpec((1, 1, tm), lambda i, zt, nv: (i, 0, 0), memory_space=pltpu.SMEM),
            pl.BlockSpec((tm * SUBLANES, LANES), lambda i, zt, nv: (i, 0)),
        ],
        out_specs=pl.BlockSpec(memory_space=pl.ANY),
        scratch_shapes=[pltpu.VMEM((TM_MOE * SUBLANES, LANES), jnp.uint32), pltpu.SemaphoreType.DMA((2,))],
    )
    return pl.pallas_call(
        _dispatch_kernel,
        grid_spec=grid_spec,
        out_shape=jax.ShapeDtypeStruct((nt * TM_MOE * SUBLANES, LANES), jnp.uint32),
        compiler_params=_cparams(("arbitrary",)),
        name="moe_dispatch",
    )(ztile, nvalid, pos_tiles, hrow)


def _moe_kernel(ea_ref, eb_ref, nvalid_ref, h_ref, wgua_ref, wda_ref, wgub_ref, wdb_ref, y_ref):
    tm = TM_MOE
    i = pl.program_id(0)
    nvalid = nvalid_ref[0]

    @pl.when(i < nvalid)
    def _():
        parts_lo, parts_hi = [], []
        for s in range(D_MODEL // 2 // LANES):
            w = _tile_row(h_ref, s, tm)
            parts_lo.append(lax.bitcast_convert_type(w << 16, F32).astype(BF16))
            parts_hi.append(lax.bitcast_convert_type(w & jnp.uint32(0xFFFF0000), F32).astype(BF16))
        x = jnp.concatenate(parts_lo + parts_hi, axis=1)
        gates = lax.bitcast_convert_type(_tile_row(h_ref, GATE_ROW, tm), F32)
        lane = lax.broadcasted_iota(jnp.int32, gates.shape, 1)

        def expert(wgu_ref, wd_ref, e):
            gu = jnp.dot(x, wgu_ref[0], preferred_element_type=F32)
            g, u = gu[:, :D_EXPERT], gu[:, D_EXPERT:]
            act = (g / (1.0 + jnp.exp(-g)) * u).astype(BF16)
            y = jnp.dot(act, wd_ref[0], preferred_element_type=F32)
            ge = jnp.sum(jnp.where(lane == e + N_GROUPS, gates, 0.0), axis=-1, keepdims=True)
            return ge * y

        y = expert(wgua_ref, wda_ref, ea_ref[i]) + expert(wgub_ref, wdb_ref, eb_ref[i])
        for s in range(SUBLANES):
            y_ref[pl.ds(s, tm, stride=SUBLANES), :] = y[:, LANES * s:LANES * (s + 1)]

    @pl.when(i >= nvalid)
    def _():
        y_ref[...] = jnp.zeros(y_ref.shape, F32)


def _moe_experts(h_sorted, ea, eb, nvalid, wgu, wd):
    tm = TM_MOE
    nt = ea.shape[0]
    w_in_spec = lambda sel: pl.BlockSpec((1, D_MODEL, 2 * D_EXPERT), sel)
    w_out_spec = lambda sel: pl.BlockSpec((1, D_EXPERT, D_MODEL), sel)
    sel_a = lambda i, ea, eb, nv: (ea[i], 0, 0)
    sel_b = lambda i, ea, eb, nv: (eb[i], 0, 0)
    rows = lambda i, ea, eb, nv: (jnp.minimum(i, nv[0] - 1), 0)
    grid_spec = pltpu.PrefetchScalarGridSpec(
        num_scalar_prefetch=3,
        grid=(nt,),
        in_specs=[
            pl.BlockSpec((tm * SUBLANES, LANES), rows),
            w_in_spec(sel_a), w_out_spec(sel_a),
            w_in_spec(sel_b), w_out_spec(sel_b),
        ],
        out_specs=pl.BlockSpec((tm * SUBLANES, LANES), lambda i, ea, eb, nv: (i, 0)),
    )
    return pl.pallas_call(
        _moe_kernel,
        grid_spec=grid_spec,
        out_shape=jax.ShapeDtypeStruct((nt * tm * SUBLANES, LANES), F32),
        compiler_params=_cparams(("arbitrary",)),
        name="moe_experts",
    )(ea, eb, nvalid, h_sorted, wgu, wd, wgu, wd)


def _combine_kernel(pos0_ref, pos1_ref, posn_ref, xn_ref, y_hbm, o_ref, buf0, buf1, buf2, sem):
    buf = (buf0, buf1, buf2)
    tm = TM_COMB
    i = pl.program_id(0)

    @pl.when(i == 0)
    def _():
        _issue_gather(pos0_ref, y_hbm, buf, sem, 0, tm)
        _issue_gather(pos1_ref, y_hbm, buf, sem, 1, tm)

    for slot in range(GATHER_SLOTS):
        @pl.when(i % GATHER_SLOTS == slot)
        def _(slot=slot):
            _wait_gather(y_hbm, buf, sem, slot, tm)
            _issue_gather(posn_ref, y_hbm, buf, sem, (slot + 2) % GATHER_SLOTS, tm)
            for s in range(SUBLANES):
                cols = slice(LANES * s, LANES * (s + 1))
                o_ref[:, cols] = xn_ref[:, cols] + _tile_row(buf[slot], s, tm)

            @pl.when(i == pl.num_programs(0) - 1)
            def _():
                _wait_gather(y_hbm, buf, sem, (slot + 1) % GATHER_SLOTS, tm)
                _wait_gather(y_hbm, buf, sem, (slot + 2) % GATHER_SLOTS, tm)


def _combine(pos_tiles, xn, y_sorted):
    n = xn.shape[0]
    tm = TM_COMB
    smem_blk = lambda f: pl.BlockSpec((1, 1, tm), f, memory_space=pltpu.SMEM)
    return pl.pallas_call(
        _combine_kernel,
        grid=(n // tm,),
        in_specs=[
            smem_blk(lambda i: (0, 0, 0)),
            smem_blk(lambda i: (1, 0, 0)),
            smem_blk(lambda i: (i + 2, 0, 0)),
            pl.BlockSpec((tm, D_MODEL), lambda i: (i, 0)),
            pl.BlockSpec(memory_space=pl.ANY),
        ],
        out_specs=pl.BlockSpec((tm, D_MODEL), lambda i: (i, 0)),
        out_shape=jax.ShapeDtypeStruct((n, D_MODEL), F32),
        scratch_shapes=[pltpu.VMEM((tm * SUBLANES, LANES), F32) for _ in range(GATHER_SLOTS)]
        + [pltpu.SemaphoreType.DMA((GATHER_SLOTS,))],
        compiler_params=_cparams(("arbitrary",)),
        name="moe_combine",
    )(pos_tiles, pos_tiles, pos_tiles, xn, y_sorted)


def _moe_plan(cls, n):
    tm = TM_MOE
    nt = n // tm + N_CLASSES
    onehot = (cls[:, None] == jnp.arange(N_CLASSES, dtype=jnp.int32)[None, :]).astype(jnp.int32)
    counts = jnp.sum(onehot, axis=0)
    rank = jnp.sum(jnp.cumsum(onehot, axis=0) * onehot, axis=1) - 1
    tiles = (counts + tm - 1) // tm
    tile_end = jnp.cumsum(tiles)
    tile_start = tile_end - tiles
    pos = jnp.sum(onehot * tile_start[None, :], axis=1) * tm + rank
    pad = GATHER_SLOTS - 1
    ztile = jnp.where(tiles > 0, (tile_end - 1) * (tm * SUBLANES), -1).astype(jnp.int32)
    nvalid = tile_end[-1]
    tile_id = jnp.minimum(jnp.arange(nt, dtype=jnp.int32), nvalid - 1)
    tcls = jnp.sum((tile_id[:, None] >= tile_end[None, :]).astype(jnp.int32), axis=1)
    grp, pair = tcls // PAIRS_PER_GROUP, tcls % PAIRS_PER_GROUP
    pair_a = jnp.asarray(PAIR_A, jnp.int32)
    pair_b = jnp.asarray(PAIR_B, jnp.int32)
    ea = grp * EXPERTS_PER_GROUP + jnp.sum((pair[:, None] == jnp.arange(PAIRS_PER_GROUP)[None]) * pair_a[None], axis=1)
    eb = grp * EXPERTS_PER_GROUP + jnp.sum((pair[:, None] == jnp.arange(PAIRS_PER_GROUP)[None]) * pair_b[None], axis=1)
    pos8 = SUBLANES * pos
    pos_disp = pos8.reshape(n // TM_DISP, 1, TM_DISP)
    pos_comb = jnp.concatenate([pos8, jnp.zeros((pad * TM_COMB,), jnp.int32)]).reshape(n // TM_COMB + pad, 1, TM_COMB)
    return (ea.astype(jnp.int32), eb.astype(jnp.int32), nvalid.reshape(1).astype(jnp.int32), ztile,
            pos_disp, pos_comb)


def _moe(hrow, cls, xn, wgu, wd):
    n = xn.shape[0]
    ea, eb, nvalid, ztile, pos_disp, pos_comb = _moe_plan(cls[:, 0], n)
    h_sorted = _dispatch(hrow, ztile, nvalid, pos_disp, ea.shape[0])
    y_sorted = _moe_experts(h_sorted, ea, eb, nvalid, wgu, wd)
    return _combine(pos_comb, xn, y_sorted)


def _head_cols(base, heads):
    return np.concatenate([base + HEAD_DIM * h + np.arange(HEAD_DIM) for h in heads])


def _proj_columns():
    o_qa, o_ka, o_va = 0, A_WIDTH, A_WIDTH + KV_WIDTH
    o_qb = o_va + KV_WIDTH
    o_kb, o_vb = o_qb + B_WIDTH, o_qb + 2 * B_WIDTH
    o_qc = o_vb + B_WIDTH
    o_kc, o_vc = o_qc + C_WIDTH, o_qc + C_WIDTH + KV_WIDTH
    nat2, nat4 = range(2), range(4)
    return np.concatenate([
        _head_cols(o_qa, PAIR_ORDER), _head_cols(o_ka, nat2),
        _head_cols(o_qb, nat4), _head_cols(o_kb, nat4),
        _head_cols(o_qc, PAIR_ORDER), _head_cols(o_kc, nat2),
        _head_cols(o_va, nat2), _head_cols(o_vb, nat4), _head_cols(o_vc, nat2),
    ])


def _rope_tables():
    nf = HEAD_DIM // 4
    inv = (np.float32(ROPE_THETA) ** (-np.arange(nf, dtype=np.float32) / np.float32(nf))).astype(np.float32)
    pos = np.arange(SEQ)
    ang_r = (pos // GRID_W).astype(np.float32)[:, None] * inv[None, :]
    ang_c = (pos % GRID_W).astype(np.float32)[:, None] * inv[None, :]
    cos = np.concatenate([np.cos(ang_r)] * 2 + [np.cos(ang_c)] * 2, axis=-1).astype(np.float32)
    sin = np.concatenate([-np.sin(ang_r), np.sin(ang_r), -np.sin(ang_c), np.sin(ang_c)], axis=-1).astype(np.float32)
    return jnp.asarray(np.concatenate([cos, cos], axis=-1)), jnp.asarray(np.concatenate([sin, sin], axis=-1))


def _layer_params(l, w_in, ln1, qk_gain, sink, rpb, out_gain, w_out, ln2, w_rg, b_rg, w_re, b_re):
    cols = _proj_columns()
    w_ext = w_in[l][:, cols].astype(BF16)
    qs = HEAD_DIM ** -0.5 * LOG2E
    g = qk_gain[l].astype(F32)
    gain = jnp.concatenate([
        jnp.tile(g[0, 0], A_HEADS) * qs, jnp.tile(g[0, 1], A_KV),
        jnp.tile(g[1, 0], B_HEADS) * qs, jnp.tile(g[1, 1], B_HEADS),
        jnp.tile(g[2, 0], C_HEADS) * qs, jnp.tile(g[2, 1], C_KV),
    ])[None, :]
    order = list(PAIR_ORDER)
    sink2 = sink[l].astype(F32)[jnp.asarray(order)] * LOG2E
    rows_a = _head_cols(0, PAIR_ORDER)
    rows_b = A_WIDTH + np.arange(B_WIDTH)
    rows_c = _head_cols(A_WIDTH + B_WIDTH, PAIR_ORDER)
    og = out_gain[l].astype(F32)
    wo = w_out[l]
    wr = jnp.concatenate([w_rg[l], w_re[l], jnp.zeros((D_MODEL, LANES - N_GROUPS - N_EXPERTS), F32)], axis=1)
    wr_hi = wr.astype(BF16)
    wr_lo = (wr - wr_hi.astype(F32)).astype(BF16)
    br = jnp.concatenate([b_rg[l].astype(F32), b_re[l].astype(F32),
                          jnp.zeros((LANES - N_GROUPS - N_EXPERTS,), F32)])[None, :]
    return dict(
        w_ext=w_ext, ln1=ln1[l][None, :], gain=gain, sink2=sink2, nbr_bias=_nbr_bias_table(rpb[l]),
        ga=og[rows_a][None, :], gb=og[rows_b][None, :], gc=og[rows_c][None, :],
        wa=wo[rows_a].astype(BF16), wb=wo[rows_b].astype(BF16), wc=wo[rows_c].astype(BF16),
        ln2=ln2[l][None, :], wr=jnp.concatenate([wr_hi, wr_lo], axis=1), br=br,
    )


def _block_ones():
    idx = np.arange(MXU_DIM) // HEAD_DIM
    return jnp.asarray((idx[:, None] == idx[None, :]).astype(np.float32)).astype(BF16)


def _trunk(x, params, shared, moe_w):
    b = x.shape[0]
    x2d = x.reshape(b * SEQ, D_MODEL)
    for l, lp in enumerate(params):
        qa, ka, va, qb, kb, vb, qc, kc, vc = _in_proj(
            x2d, lp["w_ext"], lp["ln1"], lp["gain"], shared["cos"], shared["sin"], shared["ones"])
        seq = lambda z: z.reshape(b, SEQ, z.shape[-1])
        oa = _win_attn(seq(qa), seq(ka), seq(va), shared["win_bias"], lp["sink2"])
        ob = _nbr_attn(seq(qb), seq(kb), seq(vb), lp["nbr_bias"])
        oc = _dense_attn(seq(qc), seq(kc), seq(vc))
        flat = lambda z: z.reshape(b * SEQ, z.shape[-1])
        xn, hrow, cls = _out_proj(flat(oa), flat(ob), flat(oc), x2d, lp["ga"], lp["gb"], lp["gc"],
                             lp["wa"], lp["wb"], lp["wc"], lp["ln2"], lp["wr"], lp["br"])
        wgu, wd = moe_w[l]
        x2d = _moe(hrow, cls, xn, wgu, wd)
    return x2d.reshape(b, SEQ, D_MODEL)


def kernel(x_prompt, x_sample, ln1, w_in, qk_gain, sink, rpb, out_gain, w_out, ln2, w_router_group,
           b_router_group, w_router_expert, b_router_expert, w_gate, w_up, w_down):
    depth = w_in.shape[0]
    params = [_layer_params(l, w_in, ln1, qk_gain, sink, rpb, out_gain, w_out, ln2, w_router_group,
                            b_router_group, w_router_expert, b_router_expert) for l in range(depth)]
    cos_t, sin_t = _rope_tables()
    shared = dict(cos=cos_t, sin=sin_t, ones=_block_ones(), win_bias=_win_bias_table())
    moe_w = [(jnp.concatenate([w_gate[l].astype(BF16), w_up[l].astype(BF16)], axis=-1), w_down[l].astype(BF16))
             for l in range(depth)]
    y_prompt = _trunk(x_prompt, params, shared, moe_w)
    y_sample = _trunk(x_sample, params, shared, moe_w)
    return (y_prompt, y_sample)
```

```python
import functools
import math

import numpy as np
import jax
import jax.numpy as jnp
from jax import lax
from jax.experimental import pallas as pl
from jax.experimental.pallas import tpu as pltpu

D_MODEL = 1024
SEQ = 4096
HEAD_DIM = 64
GRID_W = 64
GRID_ROWS = SEQ // GRID_W
A_HEADS, A_KV, A_WINDOW = 6, 2, 128
B_HEADS, B_ROWS, B_COLS = 4, 8, 16
C_HEADS, C_KV = 6, 2
ROPE_THETA = 10000.0
N_GROUPS, EXPERTS_PER_GROUP = 4, 4
N_EXPERTS = N_GROUPS * EXPERTS_PER_GROUP
D_EXPERT = D_MODEL // 2
EPS = 1e-6
NEG = -1e30
LOG2E = math.log2(math.e)

A_WIDTH = A_HEADS * HEAD_DIM
B_WIDTH = B_HEADS * HEAD_DIM
C_WIDTH = C_HEADS * HEAD_DIM
KV_WIDTH = A_KV * HEAD_DIM

LANES = 128
MXU_DIM = 256
VMEM_LIMIT = 56 * 1024 * 1024

PAIR_ORDER = (0, 3, 1, 4, 2, 5)

TM_PROJ = 1024
LOCAL_BATCH = 4
TQ_A = 256
KW_A = 512
TQ_B = 256
KW_B = 768
TQ_C = 512
KC_C = 512
TM_MOE = 256
TM_COMB = 256
TM_DISP = 1024

PAIR_A = (0, 0, 0, 1, 1, 2)
PAIR_B = (1, 2, 3, 2, 3, 3)
PAIRS_PER_GROUP = len(PAIR_A)
N_CLASSES = N_GROUPS * PAIRS_PER_GROUP
CLASS_LANE = N_GROUPS + N_EXPERTS
SUBLANES = 8
GATE_ROW = D_MODEL // 2 // LANES
GATHER_SLOTS = 3
DMA_THREADS = 2

BF16 = jnp.bfloat16
F32 = jnp.float32


def _cparams(sem):
    return pltpu.CompilerParams(dimension_semantics=sem, vmem_limit_bytes=VMEM_LIMIT)


N_NORMED = 1536
N_PROJ = 2048


def _in_proj_kernel(x_ref, w_ref, ln_ref, gain_ref, cos_ref, sin_ref, ones_ref,
                    qa_ref, ka_ref, va_ref, qb_ref, kb_ref, vb_ref, qc_ref, kc_ref, vc_ref):
    x = x_ref[...]
    ms = jnp.mean(x * x, axis=-1, keepdims=True)
    h = (x * lax.rsqrt(ms + EPS) * ln_ref[...]).astype(BF16)
    p = jnp.dot(h, w_ref[...], preferred_element_type=F32)
    g = gain_ref[...]
    ones_blk = ones_ref[...]

    def chunk(c):
        return p[:, MXU_DIM * c:MXU_DIM * (c + 1)]

    def inv_rms(pc):
        ss = jnp.dot((pc * pc).astype(BF16), ones_blk, preferred_element_type=F32)
        return lax.rsqrt(ss * (1.0 / HEAD_DIM) + EPS)

    r = [inv_rms(chunk(c)) for c in range(N_NORMED // MXU_DIM)]
    pn = [chunk(c) * r[c] * g[:, MXU_DIM * c:MXU_DIM * (c + 1)] for c in range(N_NORMED // MXU_DIM)]
    qa_ref[:, 0:256] = pn[0].astype(BF16)
    qa_ref[:, 256:384] = pn[1][:, :128].astype(BF16)
    ka_ref[...] = pn[1][:, 128:].astype(BF16)
    qb_ref[...] = pn[2].astype(BF16)
    kb_ref[...] = pn[3].astype(BF16)
    cos = cos_ref[...]
    sin = sin_ref[...]
    cos2 = jnp.concatenate([cos, cos], axis=1)
    sin2 = jnp.concatenate([sin, sin], axis=1)
    lane = lax.broadcasted_iota(jnp.int32, (1, MXU_DIM), 1)
    first_half = (lane % (HEAD_DIM // 2)) < HEAD_DIM // 4

    def rotary_partner(z):
        return jnp.where(first_half, pltpu.roll(z, MXU_DIM - HEAD_DIM // 4, axis=1), pltpu.roll(z, HEAD_DIM // 4, axis=1))

    c4 = pn[4] * cos2 + rotary_partner(pn[4]) * sin2
    c5 = pn[5] * cos2 + rotary_partner(pn[5]) * sin2
    qc_ref[:, 0:256] = c4.astype(BF16)
    qc_ref[:, 256:384] = c5[:, :128].astype(BF16)
    kc_ref[...] = c5[:, 128:].astype(BF16)
    va_ref[...] = p[:, 1536:1664].astype(BF16)
    vb_ref[...] = p[:, 1664:1920].astype(BF16)
    vc_ref[:, 0:128] = p[:, 1920:2048].astype(BF16)
    vc_ref[:, 128:256] = jnp.ones((x.shape[0], 128), BF16)


def _in_proj(x2d, w_ext, ln, gain, cos_t, sin_t, ones_blk):
    n = x2d.shape[0]
    tm = TM_PROJ
    tiles_per_seq = SEQ // tm
    row = lambda i: (i, 0)
    fixed = lambda i: (0, 0)
    pos = lambda i: (i % tiles_per_seq, 0)
    widths = (A_WIDTH, KV_WIDTH, KV_WIDTH, B_WIDTH, B_WIDTH, B_WIDTH, C_WIDTH, KV_WIDTH, 2 * KV_WIDTH)
    return pl.pallas_call(
        _in_proj_kernel,
        grid=(n // tm,),
        in_specs=[
            pl.BlockSpec((tm, D_MODEL), row),
            pl.BlockSpec((D_MODEL, N_PROJ), fixed),
            pl.BlockSpec((1, D_MODEL), fixed),
            pl.BlockSpec((1, N_NORMED), fixed),
            pl.BlockSpec((tm, LANES), pos),
            pl.BlockSpec((tm, LANES), pos),
            pl.BlockSpec((MXU_DIM, MXU_DIM), fixed),
        ],
        out_specs=[pl.BlockSpec((tm, w), row) for w in widths],
        out_shape=[jax.ShapeDtypeStruct((n, w), BF16) for w in widths],
        compiler_params=_cparams(("parallel",)),
        name="in_proj",
    )(x2d, w_ext, ln, gain, cos_t, sin_t, ones_blk)


def _half_masks():
    lane = lax.broadcasted_iota(jnp.int32, (1, LANES), 1)
    lo = lane < HEAD_DIM
    return lo, lo.astype(BF16), (~lo).astype(BF16)


def _win_attn_kernel(sink_ref, q_ref, k_ref, v_ref, bias_ref, o_ref):
    j = pl.program_id(1)
    start = pl.multiple_of(jnp.clip(j * TQ_A - A_WINDOW, 0, SEQ - KW_A), 128)
    lo, m_lo, m_hi = _half_masks()
    low_rows = lax.broadcasted_iota(jnp.int32, (2 * TQ_A, 1), 0) < TQ_A
    for bb in range(LOCAL_BATCH):
        kw = k_ref[bb, pl.ds(start, KW_A), :]
        vw = v_ref[bb, pl.ds(start, KW_A), :]
        for p in range(A_HEADS // 2):
            qblk = q_ref[bb, :, LANES * p:LANES * (p + 1)]
            q2 = jnp.concatenate([qblk * m_lo, qblk * m_hi], axis=0)
            s = lax.dot_general(q2, kw, (((1,), (1,)), ((), ())), preferred_element_type=F32)
            s = s + bias_ref[0, p]
            sk = jnp.where(low_rows, sink_ref[2 * p], sink_ref[2 * p + 1])
            m = jnp.maximum(jnp.max(s, axis=-1, keepdims=True), sk)
            e = jnp.exp2(s - m)
            l = jnp.sum(e, axis=-1, keepdims=True) + jnp.exp2(sk - m)
            o2 = jnp.dot(e.astype(BF16), vw, preferred_element_type=F32) / l
            o_ref[bb, :, LANES * p:LANES * (p + 1)] = jnp.where(lo, o2[:TQ_A], o2[TQ_A:]).astype(BF16)


def _win_attn(qa, ka, va, bias, sink2):
    b = qa.shape[0]
    nq = SEQ // TQ_A
    variant = lambda bi, j: (jnp.where(j == 0, 0, jnp.where(j == nq - 1, 2, 1)), 0, 0, 0)
    bb = LOCAL_BATCH
    return pl.pallas_call(
        _win_attn_kernel,
        grid=(b // bb, nq),
        in_specs=[
            pl.BlockSpec(memory_space=pltpu.SMEM),
            pl.BlockSpec((bb, TQ_A, A_WIDTH), lambda bi, j: (bi, j, 0)),
            pl.BlockSpec((bb, SEQ, KV_WIDTH), lambda bi, j: (bi, 0, 0)),
            pl.BlockSpec((bb, SEQ, KV_WIDTH), lambda bi, j: (bi, 0, 0)),
            pl.BlockSpec((1, A_HEADS // 2, 2 * TQ_A, KW_A), variant),
        ],
        out_specs=pl.BlockSpec((bb, TQ_A, A_WIDTH), lambda bi, j: (bi, j, 0)),
        out_shape=jax.ShapeDtypeStruct((b, SEQ, A_WIDTH), BF16),
        compiler_params=_cparams(("parallel", "arbitrary")),
        name="win_attn",
    )(sink2, qa, ka, va, bias)


def _win_bias_table():
    slopes = np.array([2.0 ** (-8.0 * (n + 1) / A_HEADS) for n in range(A_HEADS)], np.float32)[list(PAIR_ORDER)]
    i = np.arange(TQ_A)[:, None]
    jj = np.arange(KW_A)[None, :]
    tabs = []
    for off in (0, A_WINDOW, KW_A - TQ_A):
        dist = np.abs(off + i - jj).astype(np.float32)
        tab = np.where(dist[None] <= A_WINDOW, -slopes[:, None, None] * dist[None] * LOG2E, NEG)
        tabs.append(tab)
    return jnp.asarray(np.stack(tabs).astype(np.float32).reshape(3, A_HEADS // 2, 2 * TQ_A, KW_A))


def _nbr_attn_kernel(q_ref, k_ref, v_ref, bias_ref, o_ref):
    j = pl.program_id(1)
    rows_per_tile = TQ_B // GRID_W
    krow0 = jnp.clip(j * rows_per_tile - B_ROWS // 2, 0, GRID_ROWS - KW_B // GRID_W)
    start = pl.multiple_of(krow0 * GRID_W, 256)
    lo, m_lo, m_hi = _half_masks()
    for bb in range(LOCAL_BATCH):
        for p in range(B_HEADS // 2):
            qblk = q_ref[bb, :, LANES * p:LANES * (p + 1)]
            kw = k_ref[bb, pl.ds(start, KW_B), LANES * p:LANES * (p + 1)]
            vw = v_ref[bb, pl.ds(start, KW_B), LANES * p:LANES * (p + 1)]
            q2 = jnp.concatenate([qblk * m_lo, qblk * m_hi], axis=0)
            s = lax.dot_general(q2, kw, (((1,), (1,)), ((), ())), preferred_element_type=F32)
            s = s + bias_ref[0, p]
            m = jnp.max(s, axis=-1, keepdims=True)
            e = jnp.exp2(s - m)
            l = jnp.sum(e, axis=-1, keepdims=True)
            o2 = jnp.dot(e.astype(BF16), vw, preferred_element_type=F32) / l
            o_ref[bb, :, LANES * p:LANES * (p + 1)] = jnp.where(lo, o2[:TQ_B], o2[TQ_B:]).astype(BF16)


def _nbr_attn(qb, kb, vb, bias):
    b = qb.shape[0]
    nq = SEQ // TQ_B
    variant = lambda bi, j: (jnp.where(j == 0, 0, jnp.where(j == nq - 1, 2, 1)), 0, 0, 0)
    bb = LOCAL_BATCH
    return pl.pallas_call(
        _nbr_attn_kernel,
        grid=(b // bb, nq),
        in_specs=[
            pl.BlockSpec((bb, TQ_B, B_WIDTH), lambda bi, j: (bi, j, 0)),
            pl.BlockSpec((bb, SEQ, B_WIDTH), lambda bi, j: (bi, 0, 0)),
            pl.BlockSpec((bb, SEQ, B_WIDTH), lambda bi, j: (bi, 0, 0)),
            pl.BlockSpec((1, B_HEADS // 2, 2 * TQ_B, KW_B), variant),
        ],
        out_specs=pl.BlockSpec((bb, TQ_B, B_WIDTH), lambda bi, j: (bi, j, 0)),
        out_shape=jax.ShapeDtypeStruct((b, SEQ, B_WIDTH), BF16),
        compiler_params=_cparams(("parallel", "arbitrary")),
        name="nbr_attn",
    )(qb, kb, vb, bias)


def _nbr_bias_table(rpb):
    rows_per_tile = TQ_B // GRID_W
    krows = KW_B // GRID_W
    r = rpb.astype(F32) * LOG2E
    edge = GRID_W - B_COLS
    ext = jnp.concatenate([jnp.repeat(r[..., :1], edge, axis=-1), r, jnp.repeat(r[..., -1:], edge, axis=-1)], axis=-1)
    col = jnp.stack([ext[..., GRID_W - 1 - q:2 * GRID_W - 1 - q] for q in range(GRID_W)], axis=2)
    tabs = []
    for r_first in (0, rows_per_tile, GRID_ROWS - rows_per_tile):
        krow0 = int(np.clip(r_first - B_ROWS // 2, 0, GRID_ROWS - krows))
        slabs = []
        for ql in range(rows_per_tile):
            per_k = [col[:, int(np.clip(krow0 + kl - (r_first + ql) + B_ROWS - 1, 0, 2 * B_ROWS - 2))]
                     for kl in range(krows)]
            slabs.append(jnp.stack(per_k, axis=2))
        tab = jnp.stack(slabs, axis=1).reshape(B_HEADS, TQ_B, KW_B)
        qi = np.arange(TQ_B)
        kj = np.arange(KW_B)
        qr = (r_first + qi // GRID_W)[:, None]
        qcol = (qi % GRID_W)[:, None]
        kr = (krow0 + kj // GRID_W)[None, :]
        kcol = (kj % GRID_W)[None, :]
        r0 = np.clip(qr - B_ROWS // 2, 0, GRID_ROWS - B_ROWS)
        c0 = np.clip(qcol - B_COLS // 2, 0, GRID_W - B_COLS)
        valid = (kr >= r0) & (kr < r0 + B_ROWS) & (kcol >= c0) & (kcol < c0 + B_COLS)
        tabs.append(jnp.where(jnp.asarray(valid)[None], tab, NEG))
    return jnp.stack(tabs).reshape(3, B_HEADS // 2, 2 * TQ_B, KW_B)


def _dense_attn_kernel(q_ref, k_ref, v_ref, o_ref, s_ref):
    lo, m_lo, m_hi = _half_masks()
    nchunk = SEQ // KC_C
    npairs = C_HEADS // 2

    def scores(p):
        qblk = q_ref[0, :, LANES * p:LANES * (p + 1)]
        q2 = jnp.concatenate([qblk * m_lo, qblk * m_hi], axis=0)
        m_run = None
        for c in range(nchunk):
            kc = k_ref[0, KC_C * c:KC_C * (c + 1), :]
            s = lax.dot_general(q2, kc, (((1,), (1,)), ((), ())), preferred_element_type=F32)
            s_ref[p % 2, :, KC_C * c:KC_C * (c + 1)] = s
            for t in range(KC_C // LANES):
                blk = s[:, LANES * t:LANES * (t + 1)]
                m_run = blk if m_run is None else jnp.maximum(m_run, blk)
        return jnp.max(m_run, axis=-1, keepdims=True)

    def weighted_values(p, m):
        acc = None
        for c in range(nchunk):
            e = jnp.exp2(s_ref[p % 2, :, KC_C * c:KC_C * (c + 1)] - m).astype(BF16)
            part = jnp.dot(e, v_ref[0, KC_C * c:KC_C * (c + 1), :], preferred_element_type=F32)
            acc = part if acc is None else acc + part
        on = acc[:, :LANES] / acc[:, LANES:]
        o_ref[0, :, LANES * p:LANES * (p + 1)] = jnp.where(lo, on[:TQ_C], on[TQ_C:]).astype(BF16)

    m_next = scores(0)
    for p in range(npairs):
        m_cur = m_next
        if p + 1 < npairs:
            m_next = scores(p + 1)
        weighted_values(p, m_cur)


def _dense_attn(qc, kc, vc):
    b = qc.shape[0]
    return pl.pallas_call(
        _dense_attn_kernel,
        grid=(b, SEQ // TQ_C),
        in_specs=[
            pl.BlockSpec((1, TQ_C, C_WIDTH), lambda bi, j: (bi, j, 0)),
            pl.BlockSpec((1, SEQ, KV_WIDTH), lambda bi, j: (bi, 0, 0)),
            pl.BlockSpec((1, SEQ, 2 * KV_WIDTH), lambda bi, j: (bi, 0, 0)),
        ],
        out_specs=pl.BlockSpec((1, TQ_C, C_WIDTH), lambda bi, j: (bi, j, 0)),
        out_shape=jax.ShapeDtypeStruct((b, SEQ, C_WIDTH), BF16),
        scratch_shapes=[pltpu.VMEM((2, 2 * TQ_C, SEQ), F32)],
        compiler_params=_cparams(("parallel", "arbitrary")),
        name="dense_attn",
    )(qc, kc, vc)


def _out_proj_kernel(oa_ref, ob_ref, oc_ref, x_ref, ga_ref, gb_ref, gc_ref, wa_ref, wb_ref, wc_ref,
                     ln_ref, wr_ref, br_ref, xn_ref, h_ref, cls_ref):
    def nrm(o_ref, g_ref):
        o = o_ref[...].astype(F32)
        ms = jnp.mean(o * o, axis=-1, keepdims=True)
        return (o * lax.rsqrt(ms + EPS) * g_ref[...]).astype(BF16)

    acc = jnp.dot(nrm(oa_ref, ga_ref), wa_ref[...], preferred_element_type=F32)
    acc = acc + jnp.dot(nrm(ob_ref, gb_ref), wb_ref[...], preferred_element_type=F32)
    acc = acc + jnp.dot(nrm(oc_ref, gc_ref), wc_ref[...], preferred_element_type=F32)
    xn = x_ref[...] + acc
    xn_ref[...] = xn
    ms = jnp.mean(xn * xn, axis=-1, keepdims=True)
    h2 = xn * lax.rsqrt(ms + EPS) * ln_ref[...]
    hi = h2.astype(BF16)
    lo = (h2 - hi.astype(F32)).astype(BF16)
    tm = xn.shape[0]
    half = D_MODEL // 2
    bits_lo = lax.bitcast_convert_type(hi[:, :half].astype(F32), jnp.uint32)
    bits_hi = lax.bitcast_convert_type(hi[:, half:].astype(F32), jnp.uint32)
    words = (bits_hi & jnp.uint32(0xFFFF0000)) | (bits_lo >> 16)
    for s in range(GATE_ROW):
        h_ref[pl.ds(s, tm, stride=SUBLANES), :] = words[:, LANES * s:LANES * (s + 1)]
    for s in range(GATE_ROW + 1, SUBLANES):
        h_ref[pl.ds(s, tm, stride=SUBLANES), :] = jnp.zeros((tm, LANES), jnp.uint32)
    wr = wr_ref[...]
    t = jnp.dot(hi, wr, preferred_element_type=F32)
    u = jnp.dot(lo, wr[:, :LANES], preferred_element_type=F32)
    logits = t[:, :LANES] + t[:, LANES:] + u + br_ref[...]

    lane = lax.broadcasted_iota(jnp.int32, logits.shape, 1).astype(F32)
    big = jnp.float32(3.0e38)
    is_g = lane < N_GROUPS
    gl = jnp.where(is_g, logits, -big)
    mg = jnp.max(gl, axis=-1, keepdims=True)
    grp = jnp.min(jnp.where(gl == mg, lane, big), axis=-1, keepdims=True)
    pg = 1.0 / jnp.sum(jnp.where(is_g, jnp.exp(gl - mg), 0.0), axis=-1, keepdims=True)
    e_lo = N_GROUPS + EXPERTS_PER_GROUP * grp
    sel = (lane >= e_lo) & (lane < e_lo + EXPERTS_PER_GROUP)
    el = jnp.where(sel, logits, -big)
    v1 = jnp.max(el, axis=-1, keepdims=True)
    i1 = jnp.min(jnp.where(el == v1, lane, big), axis=-1, keepdims=True)
    el2 = jnp.where(lane == i1, -big, el)
    v2 = jnp.max(el2, axis=-1, keepdims=True)
    i2 = jnp.min(jnp.where(el2 == v2, lane, big), axis=-1, keepdims=True)
    e21 = jnp.exp(v2 - v1)
    w1 = pg / (1.0 + e21)
    w2 = pg * e21 / (1.0 + e21)
    a = jnp.minimum(i1, i2) - e_lo
    b = jnp.maximum(i1, i2) - e_lo
    cls = grp * PAIRS_PER_GROUP + (a * (7.0 - a) * 0.5 + (b - a - 1.0))
    gates = jnp.where(lane == i1, w1, jnp.where(lane == i2, w2, jnp.where(lane == CLASS_LANE, cls, 0.0)))
    h_ref[pl.ds(GATE_ROW, tm, stride=SUBLANES), :] = lax.bitcast_convert_type(gates, jnp.uint32)
    cls_ref[...] = jnp.broadcast_to(cls, (tm, LANES)).astype(jnp.int32)


def _out_proj(oa, ob, oc, x2d, ga, gb, gc, wa, wb, wc, ln2, wr, br):
    n = x2d.shape[0]
    tm = TM_PROJ
    row = lambda i: (i, 0)
    fixed = lambda i: (0, 0)
    return pl.pallas_call(
        _out_proj_kernel,
        grid=(n // tm,),
        in_specs=[
            pl.BlockSpec((tm, A_WIDTH), row),
            pl.BlockSpec((tm, B_WIDTH), row),
            pl.BlockSpec((tm, C_WIDTH), row),
            pl.BlockSpec((tm, D_MODEL), row),
            pl.BlockSpec((1, A_WIDTH), fixed),
            pl.BlockSpec((1, B_WIDTH), fixed),
            pl.BlockSpec((1, C_WIDTH), fixed),
            pl.BlockSpec((A_WIDTH, D_MODEL), fixed),
            pl.BlockSpec((B_WIDTH, D_MODEL), fixed),
            pl.BlockSpec((C_WIDTH, D_MODEL), fixed),
            pl.BlockSpec((1, D_MODEL), fixed),
            pl.BlockSpec((D_MODEL, 2 * LANES), fixed),
            pl.BlockSpec((1, LANES), fixed),
        ],
        out_specs=[
            pl.BlockSpec((tm, D_MODEL), row),
            pl.BlockSpec((tm * SUBLANES, LANES), row),
            pl.BlockSpec((tm, LANES), row),
        ],
        out_shape=[
            jax.ShapeDtypeStruct((n, D_MODEL), F32),
            jax.ShapeDtypeStruct((n * SUBLANES, LANES), jnp.uint32),
            jax.ShapeDtypeStruct((n, LANES), jnp.int32),
        ],
        compiler_params=_cparams(("parallel",)),
        name="out_proj_router",
    )(oa, ob, oc, x2d, ga, gb, gc, wa, wb, wc, ln2, wr, br)


def _tile_copy(src_hbm, buf, sem, slot, src_row8, dst_tok):
    src = src_hbm.at[pl.ds(pl.multiple_of(src_row8, SUBLANES), SUBLANES), :]
    return pltpu.make_async_copy(src, buf[slot].at[pl.ds(SUBLANES * dst_tok, SUBLANES), :], sem.at[slot])


def _issue_gather(idx_ref, src_hbm, buf, sem, slot, toks):
    for r in range(toks):
        _tile_copy(src_hbm, buf, sem, slot, idx_ref[0, 0, r], r).start(priority=r % DMA_THREADS)


def _wait_gather(src_hbm, buf, sem, slot, toks):
    pltpu.make_async_copy(src_hbm.at[pl.ds(0, SUBLANES * toks), :], buf[slot], sem.at[slot]).wait()


def _tile_row(ref, s, toks):
    return ref[pl.ds(s, toks, stride=SUBLANES), :]


def _dispatch_kernel(ztile_ref, nvalid_ref, pos_ref, h_ref, o_hbm, zbuf, sem):
    tm = TM_DISP
    tile_rows = TM_MOE * SUBLANES
    i = pl.program_id(0)

    @pl.when(i == 0)
    def _():
        zbuf[...] = jnp.zeros(zbuf.shape, zbuf.dtype)

        def zero_tile(row0):
            dst = o_hbm.at[pl.ds(pl.multiple_of(row0, SUBLANES), tile_rows), :]
            return pltpu.make_async_copy(zbuf, dst, sem.at[1])

        for c in range(N_CLASSES):
            @pl.when(ztile_ref[c] >= 0)
            def _(c=c):
                zero_tile(ztile_ref[c]).start()

        n_tiles = o_hbm.shape[0] // tile_rows

        @pl.loop(nvalid_ref[0], n_tiles)
        def _(t):
            zero_tile(t * tile_rows).start()

        for c in range(N_CLASSES):
            @pl.when(ztile_ref[c] >= 0)
            def _(c=c):
                zero_tile(ztile_ref[c]).wait()

        @pl.loop(nvalid_ref[0], n_tiles)
        def _(t):
            zero_tile(t * tile_rows).wait()

    def row_copy(r):
        dst = o_hbm.at[pl.ds(pl.multiple_of(pos_ref[0, 0, r], SUBLANES), SUBLANES), :]
        return pltpu.make_async_copy(h_ref.at[pl.ds(SUBLANES * r, SUBLANES), :], dst, sem.at[0])

    for r in range(tm):
        row_copy(r).start(priority=r % DMA_THREADS)
    pltpu.make_async_copy(h_ref, o_hbm.at[pl.ds(0, tm * SUBLANES), :], sem.at[0]).wait()


def _dispatch(hrow, ztile, nvalid, pos_tiles, nt):
    n = hrow.shape[0] // SUBLANES
    tm = TM_DISP
    grid_spec = pltpu.PrefetchScalarGridSpec(
        num_scalar_prefetch=2,
        grid=(n // tm,),
        in_specs=[
            pl.BlockSpec((1, 1, tm), lambda i, zt, nv: (i, 0, 0), memory_space=pltpu.SMEM),
            pl.BlockSpec((tm * SUBLANES, LANES), lambda i, zt, nv: (i, 0)),
        ],
        out_specs=pl.BlockSpec(memory_space=pl.ANY),
        scratch_shapes=[pltpu.VMEM((TM_MOE * SUBLANES, LANES), jnp.uint32), pltpu.SemaphoreType.DMA((2,))],
    )
    return pl.pallas_call(
        _dispatch_kernel,
        grid_spec=grid_spec,
        out_shape=jax.ShapeDtypeStruct((nt * TM_MOE * SUBLANES, LANES), jnp.uint32),
        compiler_params=_cparams(("arbitrary",)),
        name="moe_dispatch",
    )(ztile, nvalid, pos_tiles, hrow)


def _moe_kernel(ea_ref, eb_ref, nvalid_ref, h_ref, wga_ref, wua_ref, wda_ref, wgb_ref, wub_ref, wdb_ref, y_ref):
    tm = TM_MOE
    i = pl.program_id(0)
    nvalid = nvalid_ref[0]

    @pl.when(i < nvalid)
    def _():
        parts_lo, parts_hi = [], []
        for s in range(D_MODEL // 2 // LANES):
            w = _tile_row(h_ref, s, tm)
            parts_lo.append(lax.bitcast_convert_type(w << 16, F32).astype(BF16))
            parts_hi.append(lax.bitcast_convert_type(w & jnp.uint32(0xFFFF0000), F32).astype(BF16))
        x = jnp.concatenate(parts_lo + parts_hi, axis=1)
        gates = lax.bitcast_convert_type(_tile_row(h_ref, GATE_ROW, tm), F32)
        lane = lax.broadcasted_iota(jnp.int32, gates.shape, 1)

        def expert(wg_ref, wu_ref, wd_ref, e):
            g = jnp.dot(x, wg_ref[0], preferred_element_type=F32)
            u = jnp.dot(x, wu_ref[0], preferred_element_type=F32)
            act = (g / (1.0 + jnp.exp(-g)) * u).astype(BF16)
            y = jnp.dot(act, wd_ref[0], preferred_element_type=F32)
            ge = jnp.sum(jnp.where(lane == e + N_GROUPS, gates, 0.0), axis=-1, keepdims=True)
            return ge * y

        y = expert(wga_ref, wua_ref, wda_ref, ea_ref[i]) + expert(wgb_ref, wub_ref, wdb_ref, eb_ref[i])
        for s in range(SUBLANES):
            y_ref[pl.ds(s, tm, stride=SUBLANES), :] = y[:, LANES * s:LANES * (s + 1)]

    @pl.when(i >= nvalid)
    def _():
        y_ref[...] = jnp.zeros(y_ref.shape, F32)


def _moe_experts(h_sorted, ea, eb, nvalid, wg, wu, wd):
    tm = TM_MOE
    nt = ea.shape[0]
    w_in_spec = lambda sel: pl.BlockSpec((1, D_MODEL, D_EXPERT), sel)
    w_out_spec = lambda sel: pl.BlockSpec((1, D_EXPERT, D_MODEL), sel)
    sel_a = lambda i, ea, eb, nv: (ea[i], 0, 0)
    sel_b = lambda i, ea, eb, nv: (eb[i], 0, 0)
    rows = lambda i, ea, eb, nv: (jnp.minimum(i, nv[0] - 1), 0)
    grid_spec = pltpu.PrefetchScalarGridSpec(
        num_scalar_prefetch=3,
        grid=(nt,),
        in_specs=[
            pl.BlockSpec((tm * SUBLANES, LANES), rows),
            w_in_spec(sel_a), w_in_spec(sel_a), w_out_spec(sel_a),
            w_in_spec(sel_b), w_in_spec(sel_b), w_out_spec(sel_b),
        ],
        out_specs=pl.BlockSpec((tm * SUBLANES, LANES), lambda i, ea, eb, nv: (i, 0)),
    )
    return pl.pallas_call(
        _moe_kernel,
        grid_spec=grid_spec,
        out_shape=jax.ShapeDtypeStruct((nt * tm * SUBLANES, LANES), F32),
        compiler_params=_cparams(("arbitrary",)),
        name="moe_experts",
    )(ea, eb, nvalid, h_sorted, wg, wu, wd, wg, wu, wd)


def _combine_kernel(pos0_ref, pos1_ref, posn_ref, xn_ref, y_hbm, o_ref, buf0, buf1, buf2, sem):
    buf = (buf0, buf1, buf2)
    tm = TM_COMB
    i = pl.program_id(0)

    @pl.when(i == 0)
    def _():
        _issue_gather(pos0_ref, y_hbm, buf, sem, 0, tm)
        _issue_gather(pos1_ref, y_hbm, buf, sem, 1, tm)

    for slot in range(GATHER_SLOTS):
        @pl.when(i % GATHER_SLOTS == slot)
        def _(slot=slot):
            _wait_gather(y_hbm, buf, sem, slot, tm)
            _issue_gather(posn_ref, y_hbm, buf, sem, (slot + 2) % GATHER_SLOTS, tm)
            for s in range(SUBLANES):
                cols = slice(LANES * s, LANES * (s + 1))
                o_ref[:, cols] = xn_ref[:, cols] + _tile_row(buf[slot], s, tm)

            @pl.when(i == pl.num_programs(0) - 1)
            def _():
                _wait_gather(y_hbm, buf, sem, (slot + 1) % GATHER_SLOTS, tm)
                _wait_gather(y_hbm, buf, sem, (slot + 2) % GATHER_SLOTS, tm)


def _combine(pos_tiles, xn, y_sorted):
    n = xn.shape[0]
    tm = TM_COMB
    smem_blk = lambda f: pl.BlockSpec((1, 1, tm), f, memory_space=pltpu.SMEM)
    return pl.pallas_call(
        _combine_kernel,
        grid=(n // tm,),
        in_specs=[
            smem_blk(lambda i: (0, 0, 0)),
            smem_blk(lambda i: (1, 0, 0)),
            smem_blk(lambda i: (i + 2, 0, 0)),
            pl.BlockSpec((tm, D_MODEL), lambda i: (i, 0)),
            pl.BlockSpec(memory_space=pl.ANY),
        ],
        out_specs=pl.BlockSpec((tm, D_MODEL), lambda i: (i, 0)),
        out_shape=jax.ShapeDtypeStruct((n, D_MODEL), F32),
        scratch_shapes=[pltpu.VMEM((tm * SUBLANES, LANES), F32) for _ in range(GATHER_SLOTS)]
        + [pltpu.SemaphoreType.DMA((GATHER_SLOTS,))],
        compiler_params=_cparams(("arbitrary",)),
        name="moe_combine",
    )(pos_tiles, pos_tiles, pos_tiles, xn, y_sorted)


def _moe_plan(cls, n):
    tm = TM_MOE
    nt = n // tm + N_CLASSES
    onehot = (cls[:, None] == jnp.arange(N_CLASSES, dtype=jnp.int32)[None, :]).astype(jnp.int32)
    counts = jnp.sum(onehot, axis=0)
    rank = jnp.sum(jnp.cumsum(onehot, axis=0) * onehot, axis=1) - 1
    tiles = (counts + tm - 1) // tm
    tile_end = jnp.cumsum(tiles)
    tile_start = tile_end - tiles
    pos = jnp.sum(onehot * tile_start[None, :], axis=1) * tm + rank
    pad = GATHER_SLOTS - 1
    ztile = jnp.where(tiles > 0, (tile_end - 1) * (tm * SUBLANES), -1).astype(jnp.int32)
    nvalid = tile_end[-1]
    tile_id = jnp.minimum(jnp.arange(nt, dtype=jnp.int32), nvalid - 1)
    tcls = jnp.sum((tile_id[:, None] >= tile_end[None, :]).astype(jnp.int32), axis=1)
    grp, pair = tcls // PAIRS_PER_GROUP, tcls % PAIRS_PER_GROUP
    pair_a = jnp.asarray(PAIR_A, jnp.int32)
    pair_b = jnp.asarray(PAIR_B, jnp.int32)
    ea = grp * EXPERTS_PER_GROUP + jnp.sum((pair[:, None] == jnp.arange(PAIRS_PER_GROUP)[None]) * pair_a[None], axis=1)
    eb = grp * EXPERTS_PER_GROUP + jnp.sum((pair[:, None] == jnp.arange(PAIRS_PER_GROUP)[None]) * pair_b[None], axis=1)
    pos8 = SUBLANES * pos
    pos_disp = pos8.reshape(n // TM_DISP, 1, TM_DISP)
    pos_comb = jnp.concatenate([pos8, jnp.zeros((pad * TM_COMB,), jnp.int32)]).reshape(n // TM_COMB + pad, 1, TM_COMB)
    return (ea.astype(jnp.int32), eb.astype(jnp.int32), nvalid.reshape(1).astype(jnp.int32), ztile,
            pos_disp, pos_comb)


def _moe(hrow, cls, xn, wg, wu, wd):
    n = xn.shape[0]
    ea, eb, nvalid, ztile, pos_disp, pos_comb = _moe_plan(cls[:, 0], n)
    h_sorted = _dispatch(hrow, ztile, nvalid, pos_disp, ea.shape[0])
    y_sorted = _moe_experts(h_sorted, ea, eb, nvalid, wg, wu, wd)
    return _combine(pos_comb, xn, y_sorted)


def _head_cols(base, heads):
    return np.concatenate([base + HEAD_DIM * h + np.arange(HEAD_DIM) for h in heads])


def _proj_columns():
    o_qa, o_ka, o_va = 0, A_WIDTH, A_WIDTH + KV_WIDTH
    o_qb = o_va + KV_WIDTH
    o_kb, o_vb = o_qb + B_WIDTH, o_qb + 2 * B_WIDTH
    o_qc = o_vb + B_WIDTH
    o_kc, o_vc = o_qc + C_WIDTH, o_qc + C_WIDTH + KV_WIDTH
    nat2, nat4 = range(2), range(4)
    return np.concatenate([
        _head_cols(o_qa, PAIR_ORDER), _head_cols(o_ka, nat2),
        _head_cols(o_qb, nat4), _head_cols(o_kb, nat4),
        _head_cols(o_qc, PAIR_ORDER), _head_cols(o_kc, nat2),
        _head_cols(o_va, nat2), _head_cols(o_vb, nat4), _head_cols(o_vc, nat2),
    ])


def _rope_tables():
    nf = HEAD_DIM // 4
    inv = (np.float32(ROPE_THETA) ** (-np.arange(nf, dtype=np.float32) / np.float32(nf))).astype(np.float32)
    pos = np.arange(SEQ)
    ang_r = (pos // GRID_W).astype(np.float32)[:, None] * inv[None, :]
    ang_c = (pos % GRID_W).astype(np.float32)[:, None] * inv[None, :]
    cos = np.concatenate([np.cos(ang_r)] * 2 + [np.cos(ang_c)] * 2, axis=-1).astype(np.float32)
    sin = np.concatenate([-np.sin(ang_r), np.sin(ang_r), -np.sin(ang_c), np.sin(ang_c)], axis=-1).astype(np.float32)
    return jnp.asarray(np.concatenate([cos, cos], axis=-1)), jnp.asarray(np.concatenate([sin, sin], axis=-1))


def _layer_params(l, w_in, ln1, qk_gain, sink, rpb, out_gain, w_out, ln2, w_rg, b_rg, w_re, b_re):
    cols = _proj_columns()
    w_ext = w_in[l][:, cols].astype(BF16)
    qs = HEAD_DIM ** -0.5 * LOG2E
    g = qk_gain[l].astype(F32)
    gain = jnp.concatenate([
        jnp.tile(g[0, 0], A_HEADS) * qs, jnp.tile(g[0, 1], A_KV),
        jnp.tile(g[1, 0], B_HEADS) * qs, jnp.tile(g[1, 1], B_HEADS),
        jnp.tile(g[2, 0], C_HEADS) * qs, jnp.tile(g[2, 1], C_KV),
    ])[None, :]
    order = list(PAIR_ORDER)
    sink2 = sink[l].astype(F32)[jnp.asarray(order)] * LOG2E
    rows_a = _head_cols(0, PAIR_ORDER)
    rows_b = A_WIDTH + np.arange(B_WIDTH)
    rows_c = _head_cols(A_WIDTH + B_WIDTH, PAIR_ORDER)
    og = out_gain[l].astype(F32)
    wo = w_out[l]
    wr = jnp.concatenate([w_rg[l], w_re[l], jnp.zeros((D_MODEL, LANES - N_GROUPS - N_EXPERTS), F32)], axis=1)
    wr_hi = wr.astype(BF16)
    wr_lo = (wr - wr_hi.astype(F32)).astype(BF16)
    br = jnp.concatenate([b_rg[l].astype(F32), b_re[l].astype(F32),
                          jnp.zeros((LANES - N_GROUPS - N_EXPERTS,), F32)])[None, :]
    return dict(
        w_ext=w_ext, ln1=ln1[l][None, :], gain=gain, sink2=sink2, nbr_bias=_nbr_bias_table(rpb[l]),
        ga=og[rows_a][None, :], gb=og[rows_b][None, :], gc=og[rows_c][None, :],
        wa=wo[rows_a].astype(BF16), wb=wo[rows_b].astype(BF16), wc=wo[rows_c].astype(BF16),
        ln2=ln2[l][None, :], wr=jnp.concatenate([wr_hi, wr_lo], axis=1), br=br,
    )


def _block_ones():
    idx = np.arange(MXU_DIM) // HEAD_DIM
    return jnp.asarray((idx[:, None] == idx[None, :]).astype(np.float32)).astype(BF16)


def _trunk(x, params, shared, moe_w):
    b = x.shape[0]
    x2d = x.reshape(b * SEQ, D_MODEL)
    for l, lp in enumerate(params):
        qa, ka, va, qb, kb, vb, qc, kc, vc = _in_proj(
            x2d, lp["w_ext"], lp["ln1"], lp["gain"], shared["cos"], shared["sin"], shared["ones"])
        seq = lambda z: z.reshape(b, SEQ, z.shape[-1])
        oa = _win_attn(seq(qa), seq(ka), seq(va), shared["win_bias"], lp["sink2"])
        ob = _nbr_attn(seq(qb), seq(kb), seq(vb), lp["nbr_bias"])
        oc = _dense_attn(seq(qc), seq(kc), seq(vc))
        flat = lambda z: z.reshape(b * SEQ, z.shape[-1])
        xn, hrow, cls = _out_proj(flat(oa), flat(ob), flat(oc), x2d, lp["ga"], lp["gb"], lp["gc"],
                             lp["wa"], lp["wb"], lp["wc"], lp["ln2"], lp["wr"], lp["br"])
        wg, wu, wd = moe_w[l]
        x2d = _moe(hrow, cls, xn, wg, wu, wd)
    return x2d.reshape(b, SEQ, D_MODEL)


def kernel(x_prompt, x_sample, ln1, w_in, qk_gain, sink, rpb, out_gain, w_out, ln2, w_router_group,
           b_router_group, w_router_expert, b_router_expert, w_gate, w_up, w_down):
    depth = w_in.shape[0]
    params = [_layer_params(l, w_in, ln1, qk_gain, sink, rpb, out_gain, w_out, ln2, w_router_group,
                            b_router_group, w_router_expert, b_router_expert) for l in range(depth)]
    cos_t, sin_t = _rope_tables()
    shared = dict(cos=cos_t, sin=sin_t, ones=_block_ones(), win_bias=_win_bias_table())
    moe_w = [(w_gate[l].astype(BF16), w_up[l].astype(BF16), w_down[l].astype(BF16)) for l in range(depth)]
    y_prompt = _trunk(x_prompt, params, shared, moe_w)
    y_sample = _trunk(x_sample, params, shared, moe_w)
    return (y_prompt, y_sample)
```

```python
import functools
import math

import numpy as np
import jax
import jax.numpy as jnp
from jax import lax
from jax.experimental import pallas as pl
from jax.experimental.pallas import tpu as pltpu

D_MODEL = 1024
SEQ = 4096
HEAD_DIM = 64
GRID_W = 64
GRID_ROWS = SEQ // GRID_W
A_HEADS, A_KV, A_WINDOW = 6, 2, 128
B_HEADS, B_ROWS, B_COLS = 4, 8, 16
C_HEADS, C_KV = 6, 2
ROPE_THETA = 10000.0
N_GROUPS, EXPERTS_PER_GROUP = 4, 4
N_EXPERTS = N_GROUPS * EXPERTS_PER_GROUP
D_EXPERT = D_MODEL // 2
EPS = 1e-6
NEG = -1e30
LOG2E = math.log2(math.e)

A_WIDTH = A_HEADS * HEAD_DIM
B_WIDTH = B_HEADS * HEAD_DIM
C_WIDTH = C_HEADS * HEAD_DIM
KV_WIDTH = A_KV * HEAD_DIM

LANES = 128
MXU_DIM = 256
VMEM_LIMIT = 56 * 1024 * 1024

PAIR_ORDER = (0, 3, 1, 4, 2, 5)

TM_PROJ = 1024
LOCAL_BATCH = 4
TQ_A = 256
KW_A = 512
TQ_B = 256
KW_B = 768
TQ_C = 512
KC_C = 512
TM_MOE = 512
TM_COMB = 256
TM_DISP = 1024

PAIR_A = (0, 0, 0, 1, 1, 2)
PAIR_B = (1, 2, 3, 2, 3, 3)
PAIRS_PER_GROUP = len(PAIR_A)
N_CLASSES = N_GROUPS * PAIRS_PER_GROUP
CLASS_LANE = N_GROUPS + N_EXPERTS
SUBLANES = 8
GATE_ROW = D_MODEL // 2 // LANES
GATHER_SLOTS = 3
DMA_THREADS = 2

BF16 = jnp.bfloat16
F32 = jnp.float32


def _cparams(sem):
    return pltpu.CompilerParams(dimension_semantics=sem, vmem_limit_bytes=VMEM_LIMIT)


N_NORMED = 1536
N_PROJ = 2048


def _in_proj_kernel(x_ref, w_ref, ln_ref, gain_ref, cos_ref, sin_ref, ones_ref,
                    qa_ref, ka_ref, va_ref, qb_ref, kb_ref, vb_ref, qc_ref, kc_ref, vc_ref):
    x = x_ref[...]
    ms = jnp.mean(x * x, axis=-1, keepdims=True)
    h = (x * lax.rsqrt(ms + EPS) * ln_ref[...]).astype(BF16)
    p = jnp.dot(h, w_ref[...], preferred_element_type=F32)
    g = gain_ref[...]
    ones_blk = ones_ref[...]

    def chunk(c):
        return p[:, MXU_DIM * c:MXU_DIM * (c + 1)]

    def inv_rms(pc):
        ss = jnp.dot((pc * pc).astype(BF16), ones_blk, preferred_element_type=F32)
        return lax.rsqrt(ss * (1.0 / HEAD_DIM) + EPS)

    r = [inv_rms(chunk(c)) for c in range(N_NORMED // MXU_DIM)]
    pn = [chunk(c) * r[c] * g[:, MXU_DIM * c:MXU_DIM * (c + 1)] for c in range(N_NORMED // MXU_DIM)]
    qa_ref[:, 0:256] = pn[0].astype(BF16)
    qa_ref[:, 256:384] = pn[1][:, :128].astype(BF16)
    ka_ref[...] = pn[1][:, 128:].astype(BF16)
    qb_ref[...] = pn[2].astype(BF16)
    kb_ref[...] = pn[3].astype(BF16)
    cos = cos_ref[...]
    sin = sin_ref[...]
    cos2 = jnp.concatenate([cos, cos], axis=1)
    sin2 = jnp.concatenate([sin, sin], axis=1)
    lane = lax.broadcasted_iota(jnp.int32, (1, MXU_DIM), 1)
    first_half = (lane % (HEAD_DIM // 2)) < HEAD_DIM // 4

    def rotary_partner(z):
        return jnp.where(first_half, pltpu.roll(z, MXU_DIM - HEAD_DIM // 4, axis=1), pltpu.roll(z, HEAD_DIM // 4, axis=1))

    c4 = pn[4] * cos2 + rotary_partner(pn[4]) * sin2
    c5 = pn[5] * cos2 + rotary_partner(pn[5]) * sin2
    qc_ref[:, 0:256] = c4.astype(BF16)
    qc_ref[:, 256:384] = c5[:, :128].astype(BF16)
    kc_ref[...] = c5[:, 128:].astype(BF16)
    va_ref[...] = p[:, 1536:1664].astype(BF16)
    vb_ref[...] = p[:, 1664:1920].astype(BF16)
    vc_ref[:, 0:128] = p[:, 1920:2048].astype(BF16)
    vc_ref[:, 128:256] = jnp.ones((x.shape[0], 128), BF16)


def _in_proj(x2d, w_ext, ln, gain, cos_t, sin_t, ones_blk):
    n = x2d.shape[0]
    tm = TM_PROJ
    tiles_per_seq = SEQ // tm
    row = lambda i: (i, 0)
    fixed = lambda i: (0, 0)
    pos = lambda i: (i % tiles_per_seq, 0)
    widths = (A_WIDTH, KV_WIDTH, KV_WIDTH, B_WIDTH, B_WIDTH, B_WIDTH, C_WIDTH, KV_WIDTH, 2 * KV_WIDTH)
    return pl.pallas_call(
        _in_proj_kernel,
        grid=(n // tm,),
        in_specs=[
            pl.BlockSpec((tm, D_MODEL), row),
            pl.BlockSpec((D_MODEL, N_PROJ), fixed),
            pl.BlockSpec((1, D_MODEL), fixed),
            pl.BlockSpec((1, N_NORMED), fixed),
            pl.BlockSpec((tm, LANES), pos),
            pl.BlockSpec((tm, LANES), pos),
            pl.BlockSpec((MXU_DIM, MXU_DIM), fixed),
        ],
        out_specs=[pl.BlockSpec((tm, w), row) for w in widths],
        out_shape=[jax.ShapeDtypeStruct((n, w), BF16) for w in widths],
        compiler_params=_cparams(("parallel",)),
        name="in_proj",
    )(x2d, w_ext, ln, gain, cos_t, sin_t, ones_blk)


def _half_masks():
    lane = lax.broadcasted_iota(jnp.int32, (1, LANES), 1)
    lo = lane < HEAD_DIM
    return lo, lo.astype(BF16), (~lo).astype(BF16)


def _win_attn_kernel(sink_ref, q_ref, k_ref, v_ref, bias_ref, o_ref):
    j = pl.program_id(1)
    start = pl.multiple_of(jnp.clip(j * TQ_A - A_WINDOW, 0, SEQ - KW_A), 128)
    lo, m_lo, m_hi = _half_masks()
    low_rows = lax.broadcasted_iota(jnp.int32, (2 * TQ_A, 1), 0) < TQ_A
    for bb in range(LOCAL_BATCH):
        kw = k_ref[bb, pl.ds(start, KW_A), :]
        vw = v_ref[bb, pl.ds(start, KW_A), :]
        for p in range(A_HEADS // 2):
            qblk = q_ref[bb, :, LANES * p:LANES * (p + 1)]
            q2 = jnp.concatenate([qblk * m_lo, qblk * m_hi], axis=0)
            s = lax.dot_general(q2, kw, (((1,), (1,)), ((), ())), preferred_element_type=F32)
            s = s + bias_ref[0, p]
            sk = jnp.where(low_rows, sink_ref[2 * p], sink_ref[2 * p + 1])
            m = jnp.maximum(jnp.max(s, axis=-1, keepdims=True), sk)
            e = jnp.exp2(s - m)
            l = jnp.sum(e, axis=-1, keepdims=True) + jnp.exp2(sk - m)
            o2 = jnp.dot(e.astype(BF16), vw, preferred_element_type=F32) / l
            o_ref[bb, :, LANES * p:LANES * (p + 1)] = jnp.where(lo, o2[:TQ_A], o2[TQ_A:]).astype(BF16)


def _win_attn(qa, ka, va, bias, sink2):
    b = qa.shape[0]
    nq = SEQ // TQ_A
    variant = lambda bi, j: (jnp.where(j == 0, 0, jnp.where(j == nq - 1, 2, 1)), 0, 0, 0)
    bb = LOCAL_BATCH
    return pl.pallas_call(
        _win_attn_kernel,
        grid=(b // bb, nq),
        in_specs=[
            pl.BlockSpec(memory_space=pltpu.SMEM),
            pl.BlockSpec((bb, TQ_A, A_WIDTH), lambda bi, j: (bi, j, 0)),
            pl.BlockSpec((bb, SEQ, KV_WIDTH), lambda bi, j: (bi, 0, 0)),
            pl.BlockSpec((bb, SEQ, KV_WIDTH), lambda bi, j: (bi, 0, 0)),
            pl.BlockSpec((1, A_HEADS // 2, 2 * TQ_A, KW_A), variant),
        ],
        out_specs=pl.BlockSpec((bb, TQ_A, A_WIDTH), lambda bi, j: (bi, j, 0)),
        out_shape=jax.ShapeDtypeStruct((b, SEQ, A_WIDTH), BF16),
        compiler_params=_cparams(("parallel", "arbitrary")),
        name="win_attn",
    )(sink2, qa, ka, va, bias)


def _win_bias_table():
    slopes = np.array([2.0 ** (-8.0 * (n + 1) / A_HEADS) for n in range(A_HEADS)], np.float32)[list(PAIR_ORDER)]
    i = np.arange(TQ_A)[:, None]
    jj = np.arange(KW_A)[None, :]
    tabs = []
    for off in (0, A_WINDOW, KW_A - TQ_A):
        dist = np.abs(off + i - jj).astype(np.float32)
        tab = np.where(dist[None] <= A_WINDOW, -slopes[:, None, None] * dist[None] * LOG2E, NEG)
        tabs.append(tab)
    return jnp.asarray(np.stack(tabs).astype(np.float32).reshape(3, A_HEADS // 2, 2 * TQ_A, KW_A))


def _nbr_attn_kernel(q_ref, k_ref, v_ref, bias_ref, o_ref):
    j = pl.program_id(1)
    rows_per_tile = TQ_B // GRID_W
    krow0 = jnp.clip(j * rows_per_tile - B_ROWS // 2, 0, GRID_ROWS - KW_B // GRID_W)
    start = pl.multiple_of(krow0 * GRID_W, 256)
    lo, m_lo, m_hi = _half_masks()
    for bb in range(LOCAL_BATCH):
        for p in range(B_HEADS // 2):
            qblk = q_ref[bb, :, LANES * p:LANES * (p + 1)]
            kw = k_ref[bb, pl.ds(start, KW_B), LANES * p:LANES * (p + 1)]
            vw = v_ref[bb, pl.ds(start, KW_B), LANES * p:LANES * (p + 1)]
            q2 = jnp.concatenate([qblk * m_lo, qblk * m_hi], axis=0)
            s = lax.dot_general(q2, kw, (((1,), (1,)), ((), ())), preferred_element_type=F32)
            s = s + bias_ref[0, p]
            m = jnp.max(s, axis=-1, keepdims=True)
            e = jnp.exp2(s - m)
            l = jnp.sum(e, axis=-1, keepdims=True)
            o2 = jnp.dot(e.astype(BF16), vw, preferred_element_type=F32) / l
            o_ref[bb, :, LANES * p:LANES * (p + 1)] = jnp.where(lo, o2[:TQ_B], o2[TQ_B:]).astype(BF16)


def _nbr_attn(qb, kb, vb, bias):
    b = qb.shape[0]
    nq = SEQ // TQ_B
    variant = lambda bi, j: (jnp.where(j == 0, 0, jnp.where(j == nq - 1, 2, 1)), 0, 0, 0)
    bb = LOCAL_BATCH
    return pl.pallas_call(
        _nbr_attn_kernel,
        grid=(b // bb, nq),
        in_specs=[
            pl.BlockSpec((bb, TQ_B, B_WIDTH), lambda bi, j: (bi, j, 0)),
            pl.BlockSpec((bb, SEQ, B_WIDTH), lambda bi, j: (bi, 0, 0)),
            pl.BlockSpec((bb, SEQ, B_WIDTH), lambda bi, j: (bi, 0, 0)),
            pl.BlockSpec((1, B_HEADS // 2, 2 * TQ_B, KW_B), variant),
        ],
        out_specs=pl.BlockSpec((bb, TQ_B, B_WIDTH), lambda bi, j: (bi, j, 0)),
        out_shape=jax.ShapeDtypeStruct((b, SEQ, B_WIDTH), BF16),
        compiler_params=_cparams(("parallel", "arbitrary")),
        name="nbr_attn",
    )(qb, kb, vb, bias)


def _nbr_bias_table(rpb):
    rows_per_tile = TQ_B // GRID_W
    krows = KW_B // GRID_W
    r = rpb.astype(F32) * LOG2E
    edge = GRID_W - B_COLS
    ext = jnp.concatenate([jnp.repeat(r[..., :1], edge, axis=-1), r, jnp.repeat(r[..., -1:], edge, axis=-1)], axis=-1)
    col = jnp.stack([ext[..., GRID_W - 1 - q:2 * GRID_W - 1 - q] for q in range(GRID_W)], axis=2)
    tabs = []
    for r_first in (0, rows_per_tile, GRID_ROWS - rows_per_tile):
        krow0 = int(np.clip(r_first - B_ROWS // 2, 0, GRID_ROWS - krows))
        slabs = []
        for ql in range(rows_per_tile):
            per_k = [col[:, int(np.clip(krow0 + kl - (r_first + ql) + B_ROWS - 1, 0, 2 * B_ROWS - 2))]
                     for kl in range(krows)]
            slabs.append(jnp.stack(per_k, axis=2))
        tab = jnp.stack(slabs, axis=1).reshape(B_HEADS, TQ_B, KW_B)
        qi = np.arange(TQ_B)
        kj = np.arange(KW_B)
        qr = (r_first + qi // GRID_W)[:, None]
        qcol = (qi % GRID_W)[:, None]
        kr = (krow0 + kj // GRID_W)[None, :]
        kcol = (kj % GRID_W)[None, :]
        r0 = np.clip(qr - B_ROWS // 2, 0, GRID_ROWS - B_ROWS)
        c0 = np.clip(qcol - B_COLS // 2, 0, GRID_W - B_COLS)
        valid = (kr >= r0) & (kr < r0 + B_ROWS) & (kcol >= c0) & (kcol < c0 + B_COLS)
        tabs.append(jnp.where(jnp.asarray(valid)[None], tab, NEG))
    return jnp.stack(tabs).reshape(3, B_HEADS // 2, 2 * TQ_B, KW_B)


def _dense_attn_kernel(q_ref, k_ref, v_ref, o_ref, s_ref):
    lo, m_lo, m_hi = _half_masks()
    nchunk = SEQ // KC_C
    npairs = C_HEADS // 2

    def scores(p):
        qblk = q_ref[0, :, LANES * p:LANES * (p + 1)]
        q2 = jnp.concatenate([qblk * m_lo, qblk * m_hi], axis=0)
        m_run = None
        for c in range(nchunk):
            kc = k_ref[0, KC_C * c:KC_C * (c + 1), :]
            s = lax.dot_general(q2, kc, (((1,), (1,)), ((), ())), preferred_element_type=F32)
            s_ref[p % 2, :, KC_C * c:KC_C * (c + 1)] = s
            for t in range(KC_C // LANES):
                blk = s[:, LANES * t:LANES * (t + 1)]
                m_run = blk if m_run is None else jnp.maximum(m_run, blk)
        return jnp.max(m_run, axis=-1, keepdims=True)

    def weighted_values(p, m):
        acc = None
        for c in range(nchunk):
            e = jnp.exp2(s_ref[p % 2, :, KC_C * c:KC_C * (c + 1)] - m).astype(BF16)
            part = jnp.dot(e, v_ref[0, KC_C * c:KC_C * (c + 1), :], preferred_element_type=F32)
            acc = part if acc is None else acc + part
        on = acc[:, :LANES] / acc[:, LANES:]
        o_ref[0, :, LANES * p:LANES * (p + 1)] = jnp.where(lo, on[:TQ_C], on[TQ_C:]).astype(BF16)

    m_next = scores(0)
    for p in range(npairs):
        m_cur = m_next
        if p + 1 < npairs:
            m_next = scores(p + 1)
        weighted_values(p, m_cur)


def _dense_attn(qc, kc, vc):
    b = qc.shape[0]
    return pl.pallas_call(
        _dense_attn_kernel,
        grid=(b, SEQ // TQ_C),
        in_specs=[
            pl.BlockSpec((1, TQ_C, C_WIDTH), lambda bi, j: (bi, j, 0)),
            pl.BlockSpec((1, SEQ, KV_WIDTH), lambda bi, j: (bi, 0, 0)),
            pl.BlockSpec((1, SEQ, 2 * KV_WIDTH), lambda bi, j: (bi, 0, 0)),
        ],
        out_specs=pl.BlockSpec((1, TQ_C, C_WIDTH), lambda bi, j: (bi, j, 0)),
        out_shape=jax.ShapeDtypeStruct((b, SEQ, C_WIDTH), BF16),
        scratch_shapes=[pltpu.VMEM((2, 2 * TQ_C, SEQ), F32)],
        compiler_params=_cparams(("parallel", "arbitrary")),
        name="dense_attn",
    )(qc, kc, vc)


def _out_proj_kernel(oa_ref, ob_ref, oc_ref, x_ref, ga_ref, gb_ref, gc_ref, wa_ref, wb_ref, wc_ref,
                     ln_ref, wr_ref, br_ref, xn_ref, h_ref, cls_ref):
    def nrm(o_ref, g_ref):
        o = o_ref[...].astype(F32)
        ms = jnp.mean(o * o, axis=-1, keepdims=True)
        return (o * lax.rsqrt(ms + EPS) * g_ref[...]).astype(BF16)

    acc = jnp.dot(nrm(oa_ref, ga_ref), wa_ref[...], preferred_element_type=F32)
    acc = acc + jnp.dot(nrm(ob_ref, gb_ref), wb_ref[...], preferred_element_type=F32)
    acc = acc + jnp.dot(nrm(oc_ref, gc_ref), wc_ref[...], preferred_element_type=F32)
    xn = x_ref[...] + acc
    xn_ref[...] = xn
    ms = jnp.mean(xn * xn, axis=-1, keepdims=True)
    h2 = xn * lax.rsqrt(ms + EPS) * ln_ref[...]
    hi = h2.astype(BF16)
    lo = (h2 - hi.astype(F32)).astype(BF16)
    tm = xn.shape[0]
    half = D_MODEL // 2
    bits_lo = lax.bitcast_convert_type(hi[:, :half].astype(F32), jnp.uint32)
    bits_hi = lax.bitcast_convert_type(hi[:, half:].astype(F32), jnp.uint32)
    words = (bits_hi & jnp.uint32(0xFFFF0000)) | (bits_lo >> 16)
    for s in range(GATE_ROW):
        h_ref[pl.ds(s, tm, stride=SUBLANES), :] = words[:, LANES * s:LANES * (s + 1)]
    for s in range(GATE_ROW + 1, SUBLANES):
        h_ref[pl.ds(s, tm, stride=SUBLANES), :] = jnp.zeros((tm, LANES), jnp.uint32)
    wr = wr_ref[...]
    t = jnp.dot(hi, wr, preferred_element_type=F32)
    u = jnp.dot(lo, wr[:, :LANES], preferred_element_type=F32)
    logits = t[:, :LANES] + t[:, LANES:] + u + br_ref[...]

    lane = lax.broadcasted_iota(jnp.int32, logits.shape, 1).astype(F32)
    big = jnp.float32(3.0e38)
    is_g = lane < N_GROUPS
    gl = jnp.where(is_g, logits, -big)
    mg = jnp.max(gl, axis=-1, keepdims=True)
    grp = jnp.min(jnp.where(gl == mg, lane, big), axis=-1, keepdims=True)
    pg = 1.0 / jnp.sum(jnp.where(is_g, jnp.exp(gl - mg), 0.0), axis=-1, keepdims=True)
    e_lo = N_GROUPS + EXPERTS_PER_GROUP * grp
    sel = (lane >= e_lo) & (lane < e_lo + EXPERTS_PER_GROUP)
    el = jnp.where(sel, logits, -big)
    v1 = jnp.max(el, axis=-1, keepdims=True)
    i1 = jnp.min(jnp.where(el == v1, lane, big), axis=-1, keepdims=True)
    el2 = jnp.where(lane == i1, -big, el)
    v2 = jnp.max(el2, axis=-1, keepdims=True)
    i2 = jnp.min(jnp.where(el2 == v2, lane, big), axis=-1, keepdims=True)
    e21 = jnp.exp(v2 - v1)
    w1 = pg / (1.0 + e21)
    w2 = pg * e21 / (1.0 + e21)
    a = jnp.minimum(i1, i2) - e_lo
    b = jnp.maximum(i1, i2) - e_lo
    cls = grp * PAIRS_PER_GROUP + (a * (7.0 - a) * 0.5 + (b - a - 1.0))
    gates = jnp.where(lane == i1, w1, jnp.where(lane == i2, w2, jnp.where(lane == CLASS_LANE, cls, 0.0)))
    h_ref[pl.ds(GATE_ROW, tm, stride=SUBLANES), :] = lax.bitcast_convert_type(gates, jnp.uint32)
    cls_ref[...] = jnp.broadcast_to(cls, (tm, LANES)).astype(jnp.int32)


def _out_proj(oa, ob, oc, x2d, ga, gb, gc, wa, wb, wc, ln2, wr, br):
    n = x2d.shape[0]
    tm = TM_PROJ
    row = lambda i: (i, 0)
    fixed = lambda i: (0, 0)
    return pl.pallas_call(
        _out_proj_kernel,
        grid=(n // tm,),
        in_specs=[
            pl.BlockSpec((tm, A_WIDTH), row),
            pl.BlockSpec((tm, B_WIDTH), row),
            pl.BlockSpec((tm, C_WIDTH), row),
            pl.BlockSpec((tm, D_MODEL), row),
            pl.BlockSpec((1, A_WIDTH), fixed),
            pl.BlockSpec((1, B_WIDTH), fixed),
            pl.BlockSpec((1, C_WIDTH), fixed),
            pl.BlockSpec((A_WIDTH, D_MODEL), fixed),
            pl.BlockSpec((B_WIDTH, D_MODEL), fixed),
            pl.BlockSpec((C_WIDTH, D_MODEL), fixed),
            pl.BlockSpec((1, D_MODEL), fixed),
            pl.BlockSpec((D_MODEL, 2 * LANES), fixed),
            pl.BlockSpec((1, LANES), fixed),
        ],
        out_specs=[
            pl.BlockSpec((tm, D_MODEL), row),
            pl.BlockSpec((tm * SUBLANES, LANES), row),
            pl.BlockSpec((tm, LANES), row),
        ],
        out_shape=[
            jax.ShapeDtypeStruct((n, D_MODEL), F32),
            jax.ShapeDtypeStruct((n * SUBLANES, LANES), jnp.uint32),
            jax.ShapeDtypeStruct((n, LANES), jnp.int32),
        ],
        compiler_params=_cparams(("parallel",)),
        name="out_proj_router",
    )(oa, ob, oc, x2d, ga, gb, gc, wa, wb, wc, ln2, wr, br)


def _tile_copy(src_hbm, buf, sem, slot, src_row8, dst_tok):
    src = src_hbm.at[pl.ds(pl.multiple_of(src_row8, SUBLANES), SUBLANES), :]
    return pltpu.make_async_copy(src, buf[slot].at[pl.ds(SUBLANES * dst_tok, SUBLANES), :], sem.at[slot])


def _issue_gather(idx_ref, src_hbm, buf, sem, slot, toks):
    for r in range(toks):
        _tile_copy(src_hbm, buf, sem, slot, idx_ref[0, 0, r], r).start(priority=r % DMA_THREADS)


def _wait_gather(src_hbm, buf, sem, slot, toks):
    pltpu.make_async_copy(src_hbm.at[pl.ds(0, SUBLANES * toks), :], buf[slot], sem.at[slot]).wait()


def _tile_row(ref, s, toks):
    return ref[pl.ds(s, toks, stride=SUBLANES), :]


def _dispatch_kernel(ztile_ref, nvalid_ref, pos_ref, h_ref, o_hbm, zbuf, sem):
    tm = TM_DISP
    tile_rows = TM_MOE * SUBLANES
    i = pl.program_id(0)

    @pl.when(i == 0)
    def _():
        zbuf[...] = jnp.zeros(zbuf.shape, zbuf.dtype)

        def zero_tile(row0):
            dst = o_hbm.at[pl.ds(pl.multiple_of(row0, SUBLANES), tile_rows), :]
            return pltpu.make_async_copy(zbuf, dst, sem.at[1])

        for c in range(N_CLASSES):
            @pl.when(ztile_ref[c] >= 0)
            def _(c=c):
                zero_tile(ztile_ref[c]).start()

        n_tiles = o_hbm.shape[0] // tile_rows

        @pl.loop(nvalid_ref[0], n_tiles)
        def _(t):
            zero_tile(t * tile_rows).start()

        for c in range(N_CLASSES):
            @pl.when(ztile_ref[c] >= 0)
            def _(c=c):
                zero_tile(ztile_ref[c]).wait()

        @pl.loop(nvalid_ref[0], n_tiles)
        def _(t):
            zero_tile(t * tile_rows).wait()

    def row_copy(r):
        dst = o_hbm.at[pl.ds(pl.multiple_of(pos_ref[0, 0, r], SUBLANES), SUBLANES), :]
        return pltpu.make_async_copy(h_ref.at[pl.ds(SUBLANES * r, SUBLANES), :], dst, sem.at[0])

    for r in range(tm):
        row_copy(r).start(priority=r % DMA_THREADS)
    pltpu.make_async_copy(h_ref, o_hbm.at[pl.ds(0, tm * SUBLANES), :], sem.at[0]).wait()


def _dispatch(hrow, ztile, nvalid, pos_tiles, nt):
    n = hrow.shape[0] // SUBLANES
    tm = TM_DISP
    grid_spec = pltpu.PrefetchScalarGridSpec(
        num_scalar_prefetch=2,
        grid=(n // tm,),
        in_specs=[
            pl.BlockSpec((1, 1, tm), lambda i, zt, nv: (i, 0, 0), memory_space=pltpu.SMEM),
            pl.BlockSpec((tm * SUBLANES, LANES), lambda i, zt, nv: (i, 0)),
        ],
        out_specs=pl.BlockSpec(memory_space=pl.ANY),
        scratch_shapes=[pltpu.VMEM((TM_MOE * SUBLANES, LANES), jnp.uint32), pltpu.SemaphoreType.DMA((2,))],
    )
    return pl.pallas_call(
        _dispatch_kernel,
        grid_spec=grid_spec,
        out_shape=jax.ShapeDtypeStruct((nt * TM_MOE * SUBLANES, LANES), jnp.uint32),
        compiler_params=_cparams(("arbitrary",)),
        name="moe_dispatch",
    )(ztile, nvalid, pos_tiles, hrow)


def _moe_kernel(ea_ref, eb_ref, nvalid_ref, h_ref, wga_ref, wua_ref, wda_ref, wgb_ref, wub_ref, wdb_ref, y_ref):
    tm = TM_MOE
    i = pl.program_id(0)
    nvalid = nvalid_ref[0]

    @pl.when(i < nvalid)
    def _():
        parts_lo, parts_hi = [], []
        for s in range(D_MODEL // 2 // LANES):
            w = _tile_row(h_ref, s, tm)
            parts_lo.append(lax.bitcast_convert_type(w << 16, F32).astype(BF16))
            parts_hi.append(lax.bitcast_convert_type(w & jnp.uint32(0xFFFF0000), F32).astype(BF16))
        x = jnp.concatenate(parts_lo + parts_hi, axis=1)
        gates = lax.bitcast_convert_type(_tile_row(h_ref, GATE_ROW, tm), F32)
        lane = lax.broadcasted_iota(jnp.int32, gates.shape, 1)

        def expert(wg_ref, wu_ref, wd_ref, e):
            g = jnp.dot(x, wg_ref[0], preferred_element_type=F32)
            u = jnp.dot(x, wu_ref[0], preferred_element_type=F32)
            act = (g / (1.0 + jnp.exp(-g)) * u).astype(BF16)
            y = jnp.dot(act, wd_ref[0], preferred_element_type=F32)
            ge = jnp.sum(jnp.where(lane == e + N_GROUPS, gates, 0.0), axis=-1, keepdims=True)
            return ge * y

        y = expert(wga_ref, wua_ref, wda_ref, ea_ref[i]) + expert(wgb_ref, wub_ref, wdb_ref, eb_ref[i])
        for s in range(SUBLANES):
            y_ref[pl.ds(s, tm, stride=SUBLANES), :] = y[:, LANES * s:LANES * (s + 1)]

    @pl.when(i >= nvalid)
    def _():
        y_ref[...] = jnp.zeros(y_ref.shape, F32)


def _moe_experts(h_sorted, ea, eb, nvalid, wg, wu, wd):
    tm = TM_MOE
    nt = ea.shape[0]
    w_in_spec = lambda sel: pl.BlockSpec((1, D_MODEL, D_EXPERT), sel)
    w_out_spec = lambda sel: pl.BlockSpec((1, D_EXPERT, D_MODEL), sel)
    sel_a = lambda i, ea, eb, nv: (ea[i], 0, 0)
    sel_b = lambda i, ea, eb, nv: (eb[i], 0, 0)
    rows = lambda i, ea, eb, nv: (jnp.minimum(i, nv[0] - 1), 0)
    grid_spec = pltpu.PrefetchScalarGridSpec(
        num_scalar_prefetch=3,
        grid=(nt,),
        in_specs=[
            pl.BlockSpec((tm * SUBLANES, LANES), rows),
            w_in_spec(sel_a), w_in_spec(sel_a), w_out_spec(sel_a),
            w_in_spec(sel_b), w_in_spec(sel_b), w_out_spec(sel_b),
        ],
        out_specs=pl.BlockSpec((tm * SUBLANES, LANES), lambda i, ea, eb, nv: (i, 0)),
    )
    return pl.pallas_call(
        _moe_kernel,
        grid_spec=grid_spec,
        out_shape=jax.ShapeDtypeStruct((nt * tm * SUBLANES, LANES), F32),
        compiler_params=_cparams(("arbitrary",)),
        name="moe_experts",
    )(ea, eb, nvalid, h_sorted, wg, wu, wd, wg, wu, wd)


def _combine_kernel(pos0_ref, pos1_ref, posn_ref, xn_ref, y_hbm, o_ref, buf0, buf1, buf2, sem):
    buf = (buf0, buf1, buf2)
    tm = TM_COMB
    i = pl.program_id(0)

    @pl.when(i == 0)
    def _():
        _issue_gather(pos0_ref, y_hbm, buf, sem, 0, tm)
        _issue_gather(pos1_ref, y_hbm, buf, sem, 1, tm)

    for slot in range(GATHER_SLOTS):
        @pl.when(i % GATHER_SLOTS == slot)
        def _(slot=slot):
            _wait_gather(y_hbm, buf, sem, slot, tm)
            _issue_gather(posn_ref, y_hbm, buf, sem, (slot + 2) % GATHER_SLOTS, tm)
            for s in range(SUBLANES):
                cols = slice(LANES * s, LANES * (s + 1))
                o_ref[:, cols] = xn_ref[:, cols] + _tile_row(buf[slot], s, tm)

            @pl.when(i == pl.num_programs(0) - 1)
            def _():
                _wait_gather(y_hbm, buf, sem, (slot + 1) % GATHER_SLOTS, tm)
                _wait_gather(y_hbm, buf, sem, (slot + 2) % GATHER_SLOTS, tm)


def _combine(pos_tiles, xn, y_sorted):
    n = xn.shape[0]
    tm = TM_COMB
    smem_blk = lambda f: pl.BlockSpec((1, 1, tm), f, memory_space=pltpu.SMEM)
    return pl.pallas_call(
        _combine_kernel,
        grid=(n // tm,),
        in_specs=[
            smem_blk(lambda i: (0, 0, 0)),
            smem_blk(lambda i: (1, 0, 0)),
            smem_blk(lambda i: (i + 2, 0, 0)),
            pl.BlockSpec((tm, D_MODEL), lambda i: (i, 0)),
            pl.BlockSpec(memory_space=pl.ANY),
        ],
        out_specs=pl.BlockSpec((tm, D_MODEL), lambda i: (i, 0)),
        out_shape=jax.ShapeDtypeStruct((n, D_MODEL), F32),
        scratch_shapes=[pltpu.VMEM((tm * SUBLANES, LANES), F32) for _ in range(GATHER_SLOTS)]
        + [pltpu.SemaphoreType.DMA((GATHER_SLOTS,))],
        compiler_params=_cparams(("arbitrary",)),
        name="moe_combine",
    )(pos_tiles, pos_tiles, pos_tiles, xn, y_sorted)


def _moe_plan(cls, n):
    tm = TM_MOE
    nt = n // tm + N_CLASSES
    onehot = (cls[:, None] == jnp.arange(N_CLASSES, dtype=jnp.int32)[None, :]).astype(jnp.int32)
    counts = jnp.sum(onehot, axis=0)
    rank = jnp.sum(jnp.cumsum(onehot, axis=0) * onehot, axis=1) - 1
    tiles = (counts + tm - 1) // tm
    tile_end = jnp.cumsum(tiles)
    tile_start = tile_end - tiles
    pos = jnp.sum(onehot * tile_start[None, :], axis=1) * tm + rank
    pad = GATHER_SLOTS - 1
    ztile = jnp.where(tiles > 0, (tile_end - 1) * (tm * SUBLANES), -1).astype(jnp.int32)
    nvalid = tile_end[-1]
    tile_id = jnp.minimum(jnp.arange(nt, dtype=jnp.int32), nvalid - 1)
    tcls = jnp.sum((tile_id[:, None] >= tile_end[None, :]).astype(jnp.int32), axis=1)
    grp, pair = tcls // PAIRS_PER_GROUP, tcls % PAIRS_PER_GROUP
    pair_a = jnp.asarray(PAIR_A, jnp.int32)
    pair_b = jnp.asarray(PAIR_B, jnp.int32)
    ea = grp * EXPERTS_PER_GROUP + jnp.sum((pair[:, None] == jnp.arange(PAIRS_PER_GROUP)[None]) * pair_a[None], axis=1)
    eb = grp * EXPERTS_PER_GROUP + jnp.sum((pair[:, None] == jnp.arange(PAIRS_PER_GROUP)[None]) * pair_b[None], axis=1)
    pos8 = SUBLANES * pos
    pos_disp = pos8.reshape(n // TM_DISP, 1, TM_DISP)
    pos_comb = jnp.concatenate([pos8, jnp.zeros((pad * TM_COMB,), jnp.int32)]).reshape(n // TM_COMB + pad, 1, TM_COMB)
    return (ea.astype(jnp.int32), eb.astype(jnp.int32), nvalid.reshape(1).astype(jnp.int32), ztile,
            pos_disp, pos_comb)


def _moe(hrow, cls, xn, wg, wu, wd):
    n = xn.shape[0]
    ea, eb, nvalid, ztile, pos_disp, pos_comb = _moe_plan(cls[:, 0], n)
    h_sorted = _dispatch(hrow, ztile, nvalid, pos_disp, ea.shape[0])
    y_sorted = _moe_experts(h_sorted, ea, eb, nvalid, wg, wu, wd)
    return _combine(pos_comb, xn, y_sorted)


def _head_cols(base, heads):
    return np.concatenate([base + HEAD_DIM * h + np.arange(HEAD_DIM) for h in heads])


def _proj_columns():
    o_qa, o_ka, o_va = 0, A_WIDTH, A_WIDTH + KV_WIDTH
    o_qb = o_va + KV_WIDTH
    o_kb, o_vb = o_qb + B_WIDTH, o_qb + 2 * B_WIDTH
    o_qc = o_vb + B_WIDTH
    o_kc, o_vc = o_qc + C_WIDTH, o_qc + C_WIDTH + KV_WIDTH
    nat2, nat4 = range(2), range(4)
    return np.concatenate([
        _head_cols(o_qa, PAIR_ORDER), _head_cols(o_ka, nat2),
        _head_cols(o_qb, nat4), _head_cols(o_kb, nat4),
        _head_cols(o_qc, PAIR_ORDER), _head_cols(o_kc, nat2),
        _head_cols(o_va, nat2), _head_cols(o_vb, nat4), _head_cols(o_vc, nat2),
    ])


def _rope_tables():
    nf = HEAD_DIM // 4
    inv = (np.float32(ROPE_THETA) ** (-np.arange(nf, dtype=np.float32) / np.float32(nf))).astype(np.float32)
    pos = np.arange(SEQ)
    ang_r = (pos // GRID_W).astype(np.float32)[:, None] * inv[None, :]
    ang_c = (pos % GRID_W).astype(np.float32)[:, None] * inv[None, :]
    cos = np.concatenate([np.cos(ang_r)] * 2 + [np.cos(ang_c)] * 2, axis=-1).astype(np.float32)
    sin = np.concatenate([-np.sin(ang_r), np.sin(ang_r), -np.sin(ang_c), np.sin(ang_c)], axis=-1).astype(np.float32)
    return jnp.asarray(np.concatenate([cos, cos], axis=-1)), jnp.asarray(np.concatenate([sin, sin], axis=-1))


def _layer_params(l, w_in, ln1, qk_gain, sink, rpb, out_gain, w_out, ln2, w_rg, b_rg, w_re, b_re):
    cols = _proj_columns()
    w_ext = w_in[l][:, cols].astype(BF16)
    qs = HEAD_DIM ** -0.5 * LOG2E
    g = qk_gain[l].astype(F32)
    gain = jnp.concatenate([
        jnp.tile(g[0, 0], A_HEADS) * qs, jnp.tile(g[0, 1], A_KV),
        jnp.tile(g[1, 0], B_HEADS) * qs, jnp.tile(g[1, 1], B_HEADS),
        jnp.tile(g[2, 0], C_HEADS) * qs, jnp.tile(g[2, 1], C_KV),
    ])[None, :]
    order = list(PAIR_ORDER)
    sink2 = sink[l].astype(F32)[jnp.asarray(order)] * LOG2E
    rows_a = _head_cols(0, PAIR_ORDER)
    rows_b = A_WIDTH + np.arange(B_WIDTH)
    rows_c = _head_cols(A_WIDTH + B_WIDTH, PAIR_ORDER)
    og = out_gain[l].astype(F32)
    wo = w_out[l]
    wr = jnp.concatenate([w_rg[l], w_re[l], jnp.zeros((D_MODEL, LANES - N_GROUPS - N_EXPERTS), F32)], axis=1)
    wr_hi = wr.astype(BF16)
    wr_lo = (wr - wr_hi.astype(F32)).astype(BF16)
    br = jnp.concatenate([b_rg[l].astype(F32), b_re[l].astype(F32),
                          jnp.zeros((LANES - N_GROUPS - N_EXPERTS,), F32)])[None, :]
    return dict(
        w_ext=w_ext, ln1=ln1[l][None, :], gain=gain, sink2=sink2, nbr_bias=_nbr_bias_table(rpb[l]),
        ga=og[rows_a][None, :], gb=og[rows_b][None, :], gc=og[rows_c][None, :],
        wa=wo[rows_a].astype(BF16), wb=wo[rows_b].astype(BF16), wc=wo[rows_c].astype(BF16),
        ln2=ln2[l][None, :], wr=jnp.concatenate([wr_hi, wr_lo], axis=1), br=br,
    )


def _block_ones():
    idx = np.arange(MXU_DIM) // HEAD_DIM
    return jnp.asarray((idx[:, None] == idx[None, :]).astype(np.float32)).astype(BF16)


def _trunk(x, params, shared, moe_w):
    b = x.shape[0]
    x2d = x.reshape(b * SEQ, D_MODEL)
    for l, lp in enumerate(params):
        qa, ka, va, qb, kb, vb, qc, kc, vc = _in_proj(
            x2d, lp["w_ext"], lp["ln1"], lp["gain"], shared["cos"], shared["sin"], shared["ones"])
        seq = lambda z: z.reshape(b, SEQ, z.shape[-1])
        oa = _win_attn(seq(qa), seq(ka), seq(va), shared["win_bias"], lp["sink2"])
        ob = _nbr_attn(seq(qb), seq(kb), seq(vb), lp["nbr_bias"])
        oc = _dense_attn(seq(qc), seq(kc), seq(vc))
        flat = lambda z: z.reshape(b * SEQ, z.shape[-1])
        xn, hrow, cls = _out_proj(flat(oa), flat(ob), flat(oc), x2d, lp["ga"], lp["gb"], lp["gc"],
                             lp["wa"], lp["wb"], lp["wc"], lp["ln2"], lp["wr"], lp["br"])
        wg, wu, wd = moe_w[l]
        x2d = _moe(hrow, cls, xn, wg, wu, wd)
    return x2d.reshape(b, SEQ, D_MODEL)


def kernel(x_prompt, x_sample, ln1, w_in, qk_gain, sink, rpb, out_gain, w_out, ln2, w_router_group,
           b_router_group, w_router_expert, b_router_expert, w_gate, w_up, w_down):
    depth = w_in.shape[0]
    params = [_layer_params(l, w_in, ln1, qk_gain, sink, rpb, out_gain, w_out, ln2, w_router_group,
                            b_router_group, w_router_expert, b_router_expert) for l in range(depth)]
    cos_t, sin_t = _rope_tables()
    shared = dict(cos=cos_t, sin=sin_t, ones=_block_ones(), win_bias=_win_bias_table())
    moe_w = [(w_gate[l].astype(BF16), w_up[l].astype(BF16), w_down[l].astype(BF16)) for l in range(depth)]
    y_prompt = _trunk(x_prompt, params, shared, moe_w)
    y_sample = _trunk(x_sample, params, shared, moe_w)
    return (y_prompt, y_sample)
```

```python
import functools
import math

import numpy as np
import jax
import jax.numpy as jnp
from jax import lax
from jax.experimental import pallas as pl
from jax.experimental.pallas import tpu as pltpu

D_MODEL = 1024
SEQ = 4096
HEAD_DIM = 64
GRID_W = 64
GRID_ROWS = SEQ // GRID_W
A_HEADS, A_KV, A_WINDOW = 6, 2, 128
B_HEADS, B_ROWS, B_COLS = 4, 8, 16
C_HEADS, C_KV = 6, 2
ROPE_THETA = 10000.0
N_GROUPS, EXPERTS_PER_GROUP = 4, 4
N_EXPERTS = N_GROUPS * EXPERTS_PER_GROUP
D_EXPERT = D_MODEL // 2
EPS = 1e-6
NEG = -1e30
LOG2E = math.log2(math.e)

A_WIDTH = A_HEADS * HEAD_DIM
B_WIDTH = B_HEADS * HEAD_DIM
C_WIDTH = C_HEADS * HEAD_DIM
KV_WIDTH = A_KV * HEAD_DIM

LANES = 128
MXU_DIM = 256
VMEM_LIMIT = 56 * 1024 * 1024

PAIR_ORDER = (0, 3, 1, 4, 2, 5)

TM_PROJ = 1024
LOCAL_BATCH = 4
TQ_A = 256
KW_A = 512
TQ_B = 256
KW_B = 768
TQ_C = 512
KC_C = 512
TM_MOE = 512
TM_COMB = 256
TM_DISP = 1024

PAIR_A = (0, 0, 0, 1, 1, 2)
PAIR_B = (1, 2, 3, 2, 3, 3)
PAIRS_PER_GROUP = len(PAIR_A)
N_CLASSES = N_GROUPS * PAIRS_PER_GROUP
CLASS_LANE = N_GROUPS + N_EXPERTS
SUBLANES = 8
GATE_ROW = D_MODEL // 2 // LANES
GATHER_SLOTS = 3
DMA_THREADS = 2

BF16 = jnp.bfloat16
F32 = jnp.float32


def _cparams(sem):
    return pltpu.CompilerParams(dimension_semantics=sem, vmem_limit_bytes=VMEM_LIMIT)


N_NORMED = 1536
N_PROJ = 2048


def _in_proj_kernel(x_ref, w_ref, ln_ref, gain_ref, cos_ref, sin_ref, ones_ref,
                    qa_ref, ka_ref, va_ref, qb_ref, kb_ref, vb_ref, qc_ref, kc_ref, vc_ref):
    x = x_ref[...]
    ms = jnp.mean(x * x, axis=-1, keepdims=True)
    h = (x * lax.rsqrt(ms + EPS) * ln_ref[...]).astype(BF16)
    p = jnp.dot(h, w_ref[...], preferred_element_type=F32)
    g = gain_ref[...]
    ones_blk = ones_ref[...]

    def chunk(c):
        return p[:, MXU_DIM * c:MXU_DIM * (c + 1)]

    def inv_rms(pc):
        ss = jnp.dot((pc * pc).astype(BF16), ones_blk, preferred_element_type=F32)
        return lax.rsqrt(ss * (1.0 / HEAD_DIM) + EPS)

    r = [inv_rms(chunk(c)) for c in range(N_NORMED // MXU_DIM)]
    pn = [chunk(c) * r[c] * g[:, MXU_DIM * c:MXU_DIM * (c + 1)] for c in range(N_NORMED // MXU_DIM)]
    qa_ref[:, 0:256] = pn[0].astype(BF16)
    qa_ref[:, 256:384] = pn[1][:, :128].astype(BF16)
    ka_ref[...] = pn[1][:, 128:].astype(BF16)
    qb_ref[...] = pn[2].astype(BF16)
    kb_ref[...] = pn[3].astype(BF16)
    cos = cos_ref[...]
    sin = sin_ref[...]
    cos2 = jnp.concatenate([cos, cos], axis=1)
    sin2 = jnp.concatenate([sin, sin], axis=1)
    lane = lax.broadcasted_iota(jnp.int32, (1, MXU_DIM), 1)
    first_half = (lane % (HEAD_DIM // 2)) < HEAD_DIM // 4

    def rotary_partner(z):
        return jnp.where(first_half, pltpu.roll(z, MXU_DIM - HEAD_DIM // 4, axis=1), pltpu.roll(z, HEAD_DIM // 4, axis=1))

    c4 = pn[4] * cos2 + rotary_partner(pn[4]) * sin2
    c5 = pn[5] * cos2 + rotary_partner(pn[5]) * sin2
    qc_ref[:, 0:256] = c4.astype(BF16)
    qc_ref[:, 256:384] = c5[:, :128].astype(BF16)
    kc_ref[...] = c5[:, 128:].astype(BF16)
    va_ref[...] = p[:, 1536:1664].astype(BF16)
    vb_ref[...] = p[:, 1664:1920].astype(BF16)
    vc_ref[:, 0:128] = p[:, 1920:2048].astype(BF16)
    vc_ref[:, 128:256] = jnp.ones((x.shape[0], 128), BF16)


def _in_proj(x2d, w_ext, ln, gain, cos_t, sin_t, ones_blk):
    n = x2d.shape[0]
    tm = TM_PROJ
    tiles_per_seq = SEQ // tm
    row = lambda i: (i, 0)
    fixed = lambda i: (0, 0)
    pos = lambda i: (i % tiles_per_seq, 0)
    widths = (A_WIDTH, KV_WIDTH, KV_WIDTH, B_WIDTH, B_WIDTH, B_WIDTH, C_WIDTH, KV_WIDTH, 2 * KV_WIDTH)
    return pl.pallas_call(
        _in_proj_kernel,
        grid=(n // tm,),
        in_specs=[
            pl.BlockSpec((tm, D_MODEL), row),
            pl.BlockSpec((D_MODEL, N_PROJ), fixed),
            pl.BlockSpec((1, D_MODEL), fixed),
            pl.BlockSpec((1, N_NORMED), fixed),
            pl.BlockSpec((tm, LANES), pos),
            pl.BlockSpec((tm, LANES), pos),
            pl.BlockSpec((MXU_DIM, MXU_DIM), fixed),
        ],
        out_specs=[pl.BlockSpec((tm, w), row) for w in widths],
        out_shape=[jax.ShapeDtypeStruct((n, w), BF16) for w in widths],
        compiler_params=_cparams(("parallel",)),
        name="in_proj",
    )(x2d, w_ext, ln, gain, cos_t, sin_t, ones_blk)


def _half_masks():
    lane = lax.broadcasted_iota(jnp.int32, (1, LANES), 1)
    lo = lane < HEAD_DIM
    return lo, lo.astype(BF16), (~lo).astype(BF16)


def _win_attn_kernel(sink_ref, q_ref, k_ref, v_ref, bias_ref, o_ref):
    j = pl.program_id(1)
    start = pl.multiple_of(jnp.clip(j * TQ_A - A_WINDOW, 0, SEQ - KW_A), 128)
    lo, m_lo, m_hi = _half_masks()
    low_rows = lax.broadcasted_iota(jnp.int32, (2 * TQ_A, 1), 0) < TQ_A
    for bb in range(LOCAL_BATCH):
        kw = k_ref[bb, pl.ds(start, KW_A), :]
        vw = v_ref[bb, pl.ds(start, KW_A), :]
        for p in range(A_HEADS // 2):
            qblk = q_ref[bb, :, LANES * p:LANES * (p + 1)]
            q2 = jnp.concatenate([qblk * m_lo, qblk * m_hi], axis=0)
            s = lax.dot_general(q2, kw, (((1,), (1,)), ((), ())), preferred_element_type=F32)
            s = s + bias_ref[0, p]
            sk = jnp.where(low_rows, sink_ref[2 * p], sink_ref[2 * p + 1])
            m = jnp.maximum(jnp.max(s, axis=-1, keepdims=True), sk)
            e = jnp.exp2(s - m)
            l = jnp.sum(e, axis=-1, keepdims=True) + jnp.exp2(sk - m)
            o2 = jnp.dot(e.astype(BF16), vw, preferred_element_type=F32) / l
            o_ref[bb, :, LANES * p:LANES * (p + 1)] = jnp.where(lo, o2[:TQ_A], o2[TQ_A:]).astype(BF16)


def _win_attn(qa, ka, va, bias, sink2):
    b = qa.shape[0]
    nq = SEQ // TQ_A
    variant = lambda bi, j: (jnp.where(j == 0, 0, jnp.where(j == nq - 1, 2, 1)), 0, 0, 0)
    bb = LOCAL_BATCH
    return pl.pallas_call(
        _win_attn_kernel,
        grid=(b // bb, nq),
        in_specs=[
            pl.BlockSpec(memory_space=pltpu.SMEM),
            pl.BlockSpec((bb, TQ_A, A_WIDTH), lambda bi, j: (bi, j, 0)),
            pl.BlockSpec((bb, SEQ, KV_WIDTH), lambda bi, j: (bi, 0, 0)),
            pl.BlockSpec((bb, SEQ, KV_WIDTH), lambda bi, j: (bi, 0, 0)),
            pl.BlockSpec((1, A_HEADS // 2, 2 * TQ_A, KW_A), variant),
        ],
        out_specs=pl.BlockSpec((bb, TQ_A, A_WIDTH), lambda bi, j: (bi, j, 0)),
        out_shape=jax.ShapeDtypeStruct((b, SEQ, A_WIDTH), BF16),
        compiler_params=_cparams(("parallel", "arbitrary")),
        name="win_attn",
    )(sink2, qa, ka, va, bias)


def _win_bias_table():
    slopes = np.array([2.0 ** (-8.0 * (n + 1) / A_HEADS) for n in range(A_HEADS)], np.float32)[list(PAIR_ORDER)]
    i = np.arange(TQ_A)[:, None]
    jj = np.arange(KW_A)[None, :]
    tabs = []
    for off in (0, A_WINDOW, KW_A - TQ_A):
        dist = np.abs(off + i - jj).astype(np.float32)
        tab = np.where(dist[None] <= A_WINDOW, -slopes[:, None, None] * dist[None] * LOG2E, NEG)
        tabs.append(tab)
    return jnp.asarray(np.stack(tabs).astype(np.float32).reshape(3, A_HEADS // 2, 2 * TQ_A, KW_A))


def _nbr_attn_kernel(q_ref, k_ref, v_ref, bias_ref, o_ref):
    j = pl.program_id(1)
    rows_per_tile = TQ_B // GRID_W
    krow0 = jnp.clip(j * rows_per_tile - B_ROWS // 2, 0, GRID_ROWS - KW_B // GRID_W)
    start = pl.multiple_of(krow0 * GRID_W, 256)
    lo, m_lo, m_hi = _half_masks()
    for bb in range(LOCAL_BATCH):
        for p in range(B_HEADS // 2):
            qblk = q_ref[bb, :, LANES * p:LANES * (p + 1)]
            kw = k_ref[bb, pl.ds(start, KW_B), LANES * p:LANES * (p + 1)]
            vw = v_ref[bb, pl.ds(start, KW_B), LANES * p:LANES * (p + 1)]
            q2 = jnp.concatenate([qblk * m_lo, qblk * m_hi], axis=0)
            s = lax.dot_general(q2, kw, (((1,), (1,)), ((), ())), preferred_element_type=F32)
            s = s + bias_ref[0, p]
            m = jnp.max(s, axis=-1, keepdims=True)
            e = jnp.exp2(s - m)
            l = jnp.sum(e, axis=-1, keepdims=True)
            o2 = jnp.dot(e.astype(BF16), vw, preferred_element_type=F32) / l
            o_ref[bb, :, LANES * p:LANES * (p + 1)] = jnp.where(lo, o2[:TQ_B], o2[TQ_B:]).astype(BF16)


def _nbr_attn(qb, kb, vb, bias):
    b = qb.shape[0]
    nq = SEQ // TQ_B
    variant = lambda bi, j: (jnp.where(j == 0, 0, jnp.where(j == nq - 1, 2, 1)), 0, 0, 0)
    bb = LOCAL_BATCH
    return pl.pallas_call(
        _nbr_attn_kernel,
        grid=(b // bb, nq),
        in_specs=[
            pl.BlockSpec((bb, TQ_B, B_WIDTH), lambda bi, j: (bi, j, 0)),
            pl.BlockSpec((bb, SEQ, B_WIDTH), lambda bi, j: (bi, 0, 0)),
            pl.BlockSpec((bb, SEQ, B_WIDTH), lambda bi, j: (bi, 0, 0)),
            pl.BlockSpec((1, B_HEADS // 2, 2 * TQ_B, KW_B), variant),
        ],
        out_specs=pl.BlockSpec((bb, TQ_B, B_WIDTH), lambda bi, j: (bi, j, 0)),
        out_shape=jax.ShapeDtypeStruct((b, SEQ, B_WIDTH), BF16),
        compiler_params=_cparams(("parallel", "arbitrary")),
        name="nbr_attn",
    )(qb, kb, vb, bias)


def _nbr_bias_table(rpb):
    rows_per_tile = TQ_B // GRID_W
    krows = KW_B // GRID_W
    r = rpb.astype(F32) * LOG2E
    edge = GRID_W - B_COLS
    ext = jnp.concatenate([jnp.repeat(r[..., :1], edge, axis=-1), r, jnp.repeat(r[..., -1:], edge, axis=-1)], axis=-1)
    col = jnp.stack([ext[..., GRID_W - 1 - q:2 * GRID_W - 1 - q] for q in range(GRID_W)], axis=2)
    tabs = []
    for r_first in (0, rows_per_tile, GRID_ROWS - rows_per_tile):
        krow0 = int(np.clip(r_first - B_ROWS // 2, 0, GRID_ROWS - krows))
        slabs = []
        for ql in range(rows_per_tile):
            per_k = [col[:, int(np.clip(krow0 + kl - (r_first + ql) + B_ROWS - 1, 0, 2 * B_ROWS - 2))]
                     for kl in range(krows)]
            slabs.append(jnp.concatenate(per_k, axis=-1))
        tab = jnp.concatenate(slabs, axis=1)
        qi = np.arange(TQ_B)
        kj = np.arange(KW_B)
        qr = (r_first + qi // GRID_W)[:, None]
        qcol = (qi % GRID_W)[:, None]
        kr = (krow0 + kj // GRID_W)[None, :]
        kcol = (kj % GRID_W)[None, :]
        r0 = np.clip(qr - B_ROWS // 2, 0, GRID_ROWS - B_ROWS)
        c0 = np.clip(qcol - B_COLS // 2, 0, GRID_W - B_COLS)
        valid = (kr >= r0) & (kr < r0 + B_ROWS) & (kcol >= c0) & (kcol < c0 + B_COLS)
        tabs.append(jnp.where(jnp.asarray(valid)[None], tab, NEG))
    return jnp.stack(tabs).reshape(3, B_HEADS // 2, 2 * TQ_B, KW_B)


def _dense_attn_kernel(q_ref, k_ref, v_ref, o_ref, s_ref):
    lo, m_lo, m_hi = _half_masks()
    nchunk = SEQ // KC_C
    npairs = C_HEADS // 2

    def scores(p):
        qblk = q_ref[0, :, LANES * p:LANES * (p + 1)]
        q2 = jnp.concatenate([qblk * m_lo, qblk * m_hi], axis=0)
        m_run = None
        for c in range(nchunk):
            kc = k_ref[0, KC_C * c:KC_C * (c + 1), :]
            s = lax.dot_general(q2, kc, (((1,), (1,)), ((), ())), preferred_element_type=F32)
            s_ref[p % 2, :, KC_C * c:KC_C * (c + 1)] = s
            for t in range(KC_C // LANES):
                blk = s[:, LANES * t:LANES * (t + 1)]
                m_run = blk if m_run is None else jnp.maximum(m_run, blk)
        return jnp.max(m_run, axis=-1, keepdims=True)

    def weighted_values(p, m):
        acc = None
        for c in range(nchunk):
            e = jnp.exp2(s_ref[p % 2, :, KC_C * c:KC_C * (c + 1)] - m).astype(BF16)
            part = jnp.dot(e, v_ref[0, KC_C * c:KC_C * (c + 1), :], preferred_element_type=F32)
            acc = part if acc is None else acc + part
        on = acc[:, :LANES] / acc[:, LANES:]
        o_ref[0, :, LANES * p:LANES * (p + 1)] = jnp.where(lo, on[:TQ_C], on[TQ_C:]).astype(BF16)

    m_next = scores(0)
    for p in range(npairs):
        m_cur = m_next
        if p + 1 < npairs:
            m_next = scores(p + 1)
        weighted_values(p, m_cur)


def _dense_attn(qc, kc, vc):
    b = qc.shape[0]
    return pl.pallas_call(
        _dense_attn_kernel,
        grid=(b, SEQ // TQ_C),
        in_specs=[
            pl.BlockSpec((1, TQ_C, C_WIDTH), lambda bi, j: (bi, j, 0)),
            pl.BlockSpec((1, SEQ, KV_WIDTH), lambda bi, j: (bi, 0, 0)),
            pl.BlockSpec((1, SEQ, 2 * KV_WIDTH), lambda bi, j: (bi, 0, 0)),
        ],
        out_specs=pl.BlockSpec((1, TQ_C, C_WIDTH), lambda bi, j: (bi, j, 0)),
        out_shape=jax.ShapeDtypeStruct((b, SEQ, C_WIDTH), BF16),
        scratch_shapes=[pltpu.VMEM((2, 2 * TQ_C, SEQ), F32)],
        compiler_params=_cparams(("parallel", "arbitrary")),
        name="dense_attn",
    )(qc, kc, vc)


def _out_proj_kernel(oa_ref, ob_ref, oc_ref, x_ref, ga_ref, gb_ref, gc_ref, wa_ref, wb_ref, wc_ref,
                     ln_ref, wr_ref, br_ref, xn_ref, h_ref, cls_ref):
    def nrm(o_ref, g_ref):
        o = o_ref[...].astype(F32)
        ms = jnp.mean(o * o, axis=-1, keepdims=True)
        return (o * lax.rsqrt(ms + EPS) * g_ref[...]).astype(BF16)

    acc = jnp.dot(nrm(oa_ref, ga_ref), wa_ref[...], preferred_element_type=F32)
    acc = acc + jnp.dot(nrm(ob_ref, gb_ref), wb_ref[...], preferred_element_type=F32)
    acc = acc + jnp.dot(nrm(oc_ref, gc_ref), wc_ref[...], preferred_element_type=F32)
    xn = x_ref[...] + acc
    xn_ref[...] = xn
    ms = jnp.mean(xn * xn, axis=-1, keepdims=True)
    h2 = xn * lax.rsqrt(ms + EPS) * ln_ref[...]
    hi = h2.astype(BF16)
    lo = (h2 - hi.astype(F32)).astype(BF16)
    tm = xn.shape[0]
    half = D_MODEL // 2
    bits_lo = lax.bitcast_convert_type(hi[:, :half].astype(F32), jnp.uint32)
    bits_hi = lax.bitcast_convert_type(hi[:, half:].astype(F32), jnp.uint32)
    words = (bits_hi & jnp.uint32(0xFFFF0000)) | (bits_lo >> 16)
    for s in range(GATE_ROW):
        h_ref[pl.ds(s, tm, stride=SUBLANES), :] = words[:, LANES * s:LANES * (s + 1)]
    for s in range(GATE_ROW + 1, SUBLANES):
        h_ref[pl.ds(s, tm, stride=SUBLANES), :] = jnp.zeros((tm, LANES), jnp.uint32)
    wr = wr_ref[...]
    t = jnp.dot(hi, wr, preferred_element_type=F32)
    u = jnp.dot(lo, wr[:, :LANES], preferred_element_type=F32)
    logits = t[:, :LANES] + t[:, LANES:] + u + br_ref[...]

    lane = lax.broadcasted_iota(jnp.int32, logits.shape, 1).astype(F32)
    big = jnp.float32(3.0e38)
    is_g = lane < N_GROUPS
    gl = jnp.where(is_g, logits, -big)
    mg = jnp.max(gl, axis=-1, keepdims=True)
    grp = jnp.min(jnp.where(gl == mg, lane, big), axis=-1, keepdims=True)
    pg = 1.0 / jnp.sum(jnp.where(is_g, jnp.exp(gl - mg), 0.0), axis=-1, keepdims=True)
    e_lo = N_GROUPS + EXPERTS_PER_GROUP * grp
    sel = (lane >= e_lo) & (lane < e_lo + EXPERTS_PER_GROUP)
    el = jnp.where(sel, logits, -big)
    v1 = jnp.max(el, axis=-1, keepdims=True)
    i1 = jnp.min(jnp.where(el == v1, lane, big), axis=-1, keepdims=True)
    el2 = jnp.where(lane == i1, -big, el)
    v2 = jnp.max(el2, axis=-1, keepdims=True)
    i2 = jnp.min(jnp.where(el2 == v2, lane, big), axis=-1, keepdims=True)
    e21 = jnp.exp(v2 - v1)
    w1 = pg / (1.0 + e21)
    w2 = pg * e21 / (1.0 + e21)
    a = jnp.minimum(i1, i2) - e_lo
    b = jnp.maximum(i1, i2) - e_lo
    cls = grp * PAIRS_PER_GROUP + (a * (7.0 - a) * 0.5 + (b - a - 1.0))
    gates = jnp.where(lane == i1, w1, jnp.where(lane == i2, w2, jnp.where(lane == CLASS_LANE, cls, 0.0)))
    h_ref[pl.ds(GATE_ROW, tm, stride=SUBLANES), :] = lax.bitcast_convert_type(gates, jnp.uint32)
    cls_ref[...] = jnp.broadcast_to(cls, (tm, LANES)).astype(jnp.int32)


def _out_proj(oa, ob, oc, x2d, ga, gb, gc, wa, wb, wc, ln2, wr, br):
    n = x2d.shape[0]
    tm = TM_PROJ
    row = lambda i: (i, 0)
    fixed = lambda i: (0, 0)
    return pl.pallas_call(
        _out_proj_kernel,
        grid=(n // tm,),
        in_specs=[
            pl.BlockSpec((tm, A_WIDTH), row),
            pl.BlockSpec((tm, B_WIDTH), row),
            pl.BlockSpec((tm, C_WIDTH), row),
            pl.BlockSpec((tm, D_MODEL), row),
            pl.BlockSpec((1, A_WIDTH), fixed),
            pl.BlockSpec((1, B_WIDTH), fixed),
            pl.BlockSpec((1, C_WIDTH), fixed),
            pl.BlockSpec((A_WIDTH, D_MODEL), fixed),
            pl.BlockSpec((B_WIDTH, D_MODEL), fixed),
            pl.BlockSpec((C_WIDTH, D_MODEL), fixed),
            pl.BlockSpec((1, D_MODEL), fixed),
            pl.BlockSpec((D_MODEL, 2 * LANES), fixed),
            pl.BlockSpec((1, LANES), fixed),
        ],
        out_specs=[
            pl.BlockSpec((tm, D_MODEL), row),
            pl.BlockSpec((tm * SUBLANES, LANES), row),
            pl.BlockSpec((tm, LANES), row),
        ],
        out_shape=[
            jax.ShapeDtypeStruct((n, D_MODEL), F32),
            jax.ShapeDtypeStruct((n * SUBLANES, LANES), jnp.uint32),
            jax.ShapeDtypeStruct((n, LANES), jnp.int32),
        ],
        compiler_params=_cparams(("parallel",)),
        name="out_proj_router",
    )(oa, ob, oc, x2d, ga, gb, gc, wa, wb, wc, ln2, wr, br)


def _tile_copy(src_hbm, buf, sem, slot, src_row8, dst_tok):
    src = src_hbm.at[pl.ds(pl.multiple_of(src_row8, SUBLANES), SUBLANES), :]
    return pltpu.make_async_copy(src, buf[slot].at[pl.ds(SUBLANES * dst_tok, SUBLANES), :], sem.at[slot])


def _issue_gather(idx_ref, src_hbm, buf, sem, slot, toks):
    for r in range(toks):
        _tile_copy(src_hbm, buf, sem, slot, idx_ref[0, 0, r], r).start(priority=r % DMA_THREADS)


def _wait_gather(src_hbm, buf, sem, slot, toks):
    pltpu.make_async_copy(src_hbm.at[pl.ds(0, SUBLANES * toks), :], buf[slot], sem.at[slot]).wait()


def _tile_row(ref, s, toks):
    return ref[pl.ds(s, toks, stride=SUBLANES), :]


def _dispatch_kernel(ztile_ref, nvalid_ref, pos_ref, h_ref, o_hbm, zbuf, sem):
    tm = TM_DISP
    tile_rows = TM_MOE * SUBLANES
    i = pl.program_id(0)

    @pl.when(i == 0)
    def _():
        zbuf[...] = jnp.zeros(zbuf.shape, zbuf.dtype)

        def zero_tile(row0):
            dst = o_hbm.at[pl.ds(pl.multiple_of(row0, SUBLANES), tile_rows), :]
            return pltpu.make_async_copy(zbuf, dst, sem.at[1])

        for c in range(N_CLASSES):
            @pl.when(ztile_ref[c] >= 0)
            def _(c=c):
                zero_tile(ztile_ref[c]).start()

        n_tiles = o_hbm.shape[0] // tile_rows

        @pl.loop(nvalid_ref[0], n_tiles)
        def _(t):
            zero_tile(t * tile_rows).start()

        for c in range(N_CLASSES):
            @pl.when(ztile_ref[c] >= 0)
            def _(c=c):
                zero_tile(ztile_ref[c]).wait()

        @pl.loop(nvalid_ref[0], n_tiles)
        def _(t):
            zero_tile(t * tile_rows).wait()

    def row_copy(r):
        dst = o_hbm.at[pl.ds(pl.multiple_of(pos_ref[0, 0, r], SUBLANES), SUBLANES), :]
        return pltpu.make_async_copy(h_ref.at[pl.ds(SUBLANES * r, SUBLANES), :], dst, sem.at[0])

    for r in range(tm):
        row_copy(r).start(priority=r % DMA_THREADS)
    pltpu.make_async_copy(h_ref, o_hbm.at[pl.ds(0, tm * SUBLANES), :], sem.at[0]).wait()


def _dispatch(hrow, ztile, nvalid, pos_tiles, nt):
    n = hrow.shape[0] // SUBLANES
    tm = TM_DISP
    grid_spec = pltpu.PrefetchScalarGridSpec(
        num_scalar_prefetch=2,
        grid=(n // tm,),
        in_specs=[
            pl.BlockSpec((1, 1, tm), lambda i, zt, nv: (i, 0, 0), memory_space=pltpu.SMEM),
            pl.BlockSpec((tm * SUBLANES, LANES), lambda i, zt, nv: (i, 0)),
        ],
        out_specs=pl.BlockSpec(memory_space=pl.ANY),
        scratch_shapes=[pltpu.VMEM((TM_MOE * SUBLANES, LANES), jnp.uint32), pltpu.SemaphoreType.DMA((2,))],
    )
    return pl.pallas_call(
        _dispatch_kernel,
        grid_spec=grid_spec,
        out_shape=jax.ShapeDtypeStruct((nt * TM_MOE * SUBLANES, LANES), jnp.uint32),
        compiler_params=_cparams(("arbitrary",)),
        name="moe_dispatch",
    )(ztile, nvalid, pos_tiles, hrow)


def _moe_kernel(ea_ref, eb_ref, nvalid_ref, h_ref, wga_ref, wua_ref, wda_ref, wgb_ref, wub_ref, wdb_ref, y_ref):
    tm = TM_MOE
    i = pl.program_id(0)
    nvalid = nvalid_ref[0]

    @pl.when(i < nvalid)
    def _():
        parts_lo, parts_hi = [], []
        for s in range(D_MODEL // 2 // LANES):
            w = _tile_row(h_ref, s, tm)
            parts_lo.append(lax.bitcast_convert_type(w << 16, F32).astype(BF16))
            parts_hi.append(lax.bitcast_convert_type(w & jnp.uint32(0xFFFF0000), F32).astype(BF16))
        x = jnp.concatenate(parts_lo + parts_hi, axis=1)
        gates = lax.bitcast_convert_type(_tile_row(h_ref, GATE_ROW, tm), F32)
        lane = lax.broadcasted_iota(jnp.int32, gates.shape, 1)

        def expert(wg_ref, wu_ref, wd_ref, e):
            g = jnp.dot(x, wg_ref[0], preferred_element_type=F32)
            u = jnp.dot(x, wu_ref[0], preferred_element_type=F32)
            act = (g / (1.0 + jnp.exp(-g)) * u).astype(BF16)
            y = jnp.dot(act, wd_ref[0], preferred_element_type=F32)
            ge = jnp.sum(jnp.where(lane == e + N_GROUPS, gates, 0.0), axis=-1, keepdims=True)
            return ge * y

        y = expert(wga_ref, wua_ref, wda_ref, ea_ref[i]) + expert(wgb_ref, wub_ref, wdb_ref, eb_ref[i])
        for s in range(SUBLANES):
            y_ref[pl.ds(s, tm, stride=SUBLANES), :] = y[:, LANES * s:LANES * (s + 1)]

    @pl.when(i >= nvalid)
    def _():
        y_ref[...] = jnp.zeros(y_ref.shape, F32)


def _moe_experts(h_sorted, ea, eb, nvalid, wg, wu, wd):
    tm = TM_MOE
    nt = ea.shape[0]
    w_in_spec = lambda sel: pl.BlockSpec((1, D_MODEL, D_EXPERT), sel)
    w_out_spec = lambda sel: pl.BlockSpec((1, D_EXPERT, D_MODEL), sel)
    sel_a = lambda i, ea, eb, nv: (ea[i], 0, 0)
    sel_b = lambda i, ea, eb, nv: (eb[i], 0, 0)
    rows = lambda i, ea, eb, nv: (jnp.minimum(i, nv[0] - 1), 0)
    grid_spec = pltpu.PrefetchScalarGridSpec(
        num_scalar_prefetch=3,
        grid=(nt,),
        in_specs=[
            pl.BlockSpec((tm * SUBLANES, LANES), rows),
            w_in_spec(sel_a), w_in_spec(sel_a), w_out_spec(sel_a),
            w_in_spec(sel_b), w_in_spec(sel_b), w_out_spec(sel_b),
        ],
        out_specs=pl.BlockSpec((tm * SUBLANES, LANES), lambda i, ea, eb, nv: (i, 0)),
    )
    return pl.pallas_call(
        _moe_kernel,
        grid_spec=grid_spec,
        out_shape=jax.ShapeDtypeStruct((nt * tm * SUBLANES, LANES), F32),
        compiler_params=_cparams(("arbitrary",)),
        name="moe_experts",
    )(ea, eb, nvalid, h_sorted, wg, wu, wd, wg, wu, wd)


def _combine_kernel(pos0_ref, pos1_ref, posn_ref, xn_ref, y_hbm, o_ref, buf0, buf1, buf2, sem):
    buf = (buf0, buf1, buf2)
    tm = TM_COMB
    i = pl.program_id(0)

    @pl.when(i == 0)
    def _():
        _issue_gather(pos0_ref, y_hbm, buf, sem, 0, tm)
        _issue_gather(pos1_ref, y_hbm, buf, sem, 1, tm)

    for slot in range(GATHER_SLOTS):
        @pl.when(i % GATHER_SLOTS == slot)
        def _(slot=slot):
            _wait_gather(y_hbm, buf, sem, slot, tm)
            _issue_gather(posn_ref, y_hbm, buf, sem, (slot + 2) % GATHER_SLOTS, tm)
            for s in range(SUBLANES):
                cols = slice(LANES * s, LANES * (s + 1))
                o_ref[:, cols] = xn_ref[:, cols] + _tile_row(buf[slot], s, tm)

            @pl.when(i == pl.num_programs(0) - 1)
            def _():
                _wait_gather(y_hbm, buf, sem, (slot + 1) % GATHER_SLOTS, tm)
                _wait_gather(y_hbm, buf, sem, (slot + 2) % GATHER_SLOTS, tm)


def _combine(pos_tiles, xn, y_sorted):
    n = xn.shape[0]
    tm = TM_COMB
    smem_blk = lambda f: pl.BlockSpec((1, 1, tm), f, memory_space=pltpu.SMEM)
    return pl.pallas_call(
        _combine_kernel,
        grid=(n // tm,),
        in_specs=[
            smem_blk(lambda i: (0, 0, 0)),
            smem_blk(lambda i: (1, 0, 0)),
            smem_blk(lambda i: (i + 2, 0, 0)),
            pl.BlockSpec((tm, D_MODEL), lambda i: (i, 0)),
            pl.BlockSpec(memory_space=pl.ANY),
        ],
        out_specs=pl.BlockSpec((tm, D_MODEL), lambda i: (i, 0)),
        out_shape=jax.ShapeDtypeStruct((n, D_MODEL), F32),
        scratch_shapes=[pltpu.VMEM((tm * SUBLANES, LANES), F32) for _ in range(GATHER_SLOTS)]
        + [pltpu.SemaphoreType.DMA((GATHER_SLOTS,))],
        compiler_params=_cparams(("arbitrary",)),
        name="moe_combine",
    )(pos_tiles, pos_tiles, pos_tiles, xn, y_sorted)


def _moe_plan(cls, n):
    tm = TM_MOE
    nt = n // tm + N_CLASSES
    onehot = (cls[:, None] == jnp.arange(N_CLASSES, dtype=jnp.int32)[None, :]).astype(jnp.int32)
    counts = jnp.sum(onehot, axis=0)
    rank = jnp.sum(jnp.cumsum(onehot, axis=0) * onehot, axis=1) - 1
    tiles = (counts + tm - 1) // tm
    tile_end = jnp.cumsum(tiles)
    tile_start = tile_end - tiles
    pos = jnp.sum(onehot * tile_start[None, :], axis=1) * tm + rank
    pad = GATHER_SLOTS - 1
    ztile = jnp.where(tiles > 0, (tile_end - 1) * (tm * SUBLANES), -1).astype(jnp.int32)
    nvalid = tile_end[-1]
    tile_id = jnp.minimum(jnp.arange(nt, dtype=jnp.int32), nvalid - 1)
    tcls = jnp.sum((tile_id[:, None] >= tile_end[None, :]).astype(jnp.int32), axis=1)
    grp, pair = tcls // PAIRS_PER_GROUP, tcls % PAIRS_PER_GROUP
    pair_a = jnp.asarray(PAIR_A, jnp.int32)
    pair_b = jnp.asarray(PAIR_B, jnp.int32)
    ea = grp * EXPERTS_PER_GROUP + jnp.sum((pair[:, None] == jnp.arange(PAIRS_PER_GROUP)[None]) * pair_a[None], axis=1)
    eb = grp * EXPERTS_PER_GROUP + jnp.sum((pair[:, None] == jnp.arange(PAIRS_PER_GROUP)[None]) * pair_b[None], axis=1)
    pos8 = SUBLANES * pos
    pos_disp = pos8.reshape(n // TM_DISP, 1, TM_DISP)
    pos_comb = jnp.concatenate([pos8, jnp.zeros((pad * TM_COMB,), jnp.int32)]).reshape(n // TM_COMB + pad, 1, TM_COMB)
    return (ea.astype(jnp.int32), eb.astype(jnp.int32), nvalid.reshape(1).astype(jnp.int32), ztile,
            pos_disp, pos_comb)


def _moe(hrow, cls, xn, wg, wu, wd):
    n = xn.shape[0]
    ea, eb, nvalid, ztile, pos_disp, pos_comb = _moe_plan(cls[:, 0], n)
    h_sorted = _dispatch(hrow, ztile, nvalid, pos_disp, ea.shape[0])
    y_sorted = _moe_experts(h_sorted, ea, eb, nvalid, wg, wu, wd)
    return _combine(pos_comb, xn, y_sorted)


def _head_cols(base, heads):
    return np.concatenate([base + HEAD_DIM * h + np.arange(HEAD_DIM) for h in heads])


def _proj_columns():
    o_qa, o_ka, o_va = 0, A_WIDTH, A_WIDTH + KV_WIDTH
    o_qb = o_va + KV_WIDTH
    o_kb, o_vb = o_qb + B_WIDTH, o_qb + 2 * B_WIDTH
    o_qc = o_vb + B_WIDTH
    o_kc, o_vc = o_qc + C_WIDTH, o_qc + C_WIDTH + KV_WIDTH
    nat2, nat4 = range(2), range(4)
    return np.concatenate([
        _head_cols(o_qa, PAIR_ORDER), _head_cols(o_ka, nat2),
        _head_cols(o_qb, nat4), _head_cols(o_kb, nat4),
        _head_cols(o_qc, PAIR_ORDER), _head_cols(o_kc, nat2),
        _head_cols(o_va, nat2), _head_cols(o_vb, nat4), _head_cols(o_vc, nat2),
    ])


def _rope_tables():
    nf = HEAD_DIM // 4
    inv = (np.float32(ROPE_THETA) ** (-np.arange(nf, dtype=np.float32) / np.float32(nf))).astype(np.float32)
    pos = np.arange(SEQ)
    ang_r = (pos // GRID_W).astype(np.float32)[:, None] * inv[None, :]
    ang_c = (pos % GRID_W).astype(np.float32)[:, None] * inv[None, :]
    cos = np.concatenate([np.cos(ang_r)] * 2 + [np.cos(ang_c)] * 2, axis=-1).astype(np.float32)
    sin = np.concatenate([-np.sin(ang_r), np.sin(ang_r), -np.sin(ang_c), np.sin(ang_c)], axis=-1).astype(np.float32)
    return jnp.asarray(np.concatenate([cos, cos], axis=-1)), jnp.asarray(np.concatenate([sin, sin], axis=-1))


def _layer_params(l, w_in, ln1, qk_gain, sink, rpb, out_gain, w_out, ln2, w_rg, b_rg, w_re, b_re):
    cols = _proj_columns()
    w_ext = w_in[l][:, cols].astype(BF16)
    qs = HEAD_DIM ** -0.5 * LOG2E
    g = qk_gain[l].astype(F32)
    gain = jnp.concatenate([
        jnp.tile(g[0, 0], A_HEADS) * qs, jnp.tile(g[0, 1], A_KV),
        jnp.tile(g[1, 0], B_HEADS) * qs, jnp.tile(g[1, 1], B_HEADS),
        jnp.tile(g[2, 0], C_HEADS) * qs, jnp.tile(g[2, 1], C_KV),
    ])[None, :]
    order = list(PAIR_ORDER)
    sink2 = sink[l].astype(F32)[jnp.asarray(order)] * LOG2E
    rows_a = _head_cols(0, PAIR_ORDER)
    rows_b = A_WIDTH + np.arange(B_WIDTH)
    rows_c = _head_cols(A_WIDTH + B_WIDTH, PAIR_ORDER)
    og = out_gain[l].astype(F32)
    wo = w_out[l]
    wr = jnp.concatenate([w_rg[l], w_re[l], jnp.zeros((D_MODEL, LANES - N_GROUPS - N_EXPERTS), F32)], axis=1)
    wr_hi = wr.astype(BF16)
    wr_lo = (wr - wr_hi.astype(F32)).astype(BF16)
    br = jnp.concatenate([b_rg[l].astype(F32), b_re[l].astype(F32),
                          jnp.zeros((LANES - N_GROUPS - N_EXPERTS,), F32)])[None, :]
    return dict(
        w_ext=w_ext, ln1=ln1[l][None, :], gain=gain, sink2=sink2, nbr_bias=_nbr_bias_table(rpb[l]),
        ga=og[rows_a][None, :], gb=og[rows_b][None, :], gc=og[rows_c][None, :],
        wa=wo[rows_a].astype(BF16), wb=wo[rows_b].astype(BF16), wc=wo[rows_c].astype(BF16),
        ln2=ln2[l][None, :], wr=jnp.concatenate([wr_hi, wr_lo], axis=1), br=br,
    )


def _block_ones():
    idx = np.arange(MXU_DIM) // HEAD_DIM
    return jnp.asarray((idx[:, None] == idx[None, :]).astype(np.float32)).astype(BF16)


def _trunk(x, params, shared, moe_w):
    b = x.shape[0]
    x2d = x.reshape(b * SEQ, D_MODEL)
    for l, lp in enumerate(params):
        qa, ka, va, qb, kb, vb, qc, kc, vc = _in_proj(
            x2d, lp["w_ext"], lp["ln1"], lp["gain"], shared["cos"], shared["sin"], shared["ones"])
        seq = lambda z: z.reshape(b, SEQ, z.shape[-1])
        oa = _win_attn(seq(qa), seq(ka), seq(va), shared["win_bias"], lp["sink2"])
        ob = _nbr_attn(seq(qb), seq(kb), seq(vb), lp["nbr_bias"])
        oc = _dense_attn(seq(qc), seq(kc), seq(vc))
        flat = lambda z: z.reshape(b * SEQ, z.shape[-1])
        xn, hrow, cls = _out_proj(flat(oa), flat(ob), flat(oc), x2d, lp["ga"], lp["gb"], lp["gc"],
                             lp["wa"], lp["wb"], lp["wc"], lp["ln2"], lp["wr"], lp["br"])
        wg, wu, wd = moe_w[l]
        x2d = _moe(hrow, cls, xn, wg, wu, wd)
    return x2d.reshape(b, SEQ, D_MODEL)


def kernel(x_prompt, x_sample, ln1, w_in, qk_gain, sink, rpb, out_gain, w_out, ln2, w_router_group,
           b_router_group, w_router_expert, b_router_expert, w_gate, w_up, w_down):
    depth = w_in.shape[0]
    params = [_layer_params(l, w_in, ln1, qk_gain, sink, rpb, out_gain, w_out, ln2, w_router_group,
                            b_router_group, w_router_expert, b_router_expert) for l in range(depth)]
    cos_t, sin_t = _rope_tables()
    shared = dict(cos=cos_t, sin=sin_t, ones=_block_ones(), win_bias=_win_bias_table())
    moe_w = [(w_gate[l].astype(BF16), w_up[l].astype(BF16), w_down[l].astype(BF16)) for l in range(depth)]
    y_prompt = _trunk(x_prompt, params, shared, moe_w)
    y_sample = _trunk(x_sample, params, shared, moe_w)
    return (y_prompt, y_sample)
```

```python
import functools
import math

import numpy as np
import jax
import jax.numpy as jnp
from jax import lax
from jax.experimental import pallas as pl
from jax.experimental.pallas import tpu as pltpu

D_MODEL = 1024
SEQ = 4096
HEAD_DIM = 64
GRID_W = 64
GRID_ROWS = SEQ // GRID_W
A_HEADS, A_KV, A_WINDOW = 6, 2, 128
B_HEADS, B_ROWS, B_COLS = 4, 8, 16
C_HEADS, C_KV = 6, 2
ROPE_THETA = 10000.0
N_GROUPS, EXPERTS_PER_GROUP = 4, 4
N_EXPERTS = N_GROUPS * EXPERTS_PER_GROUP
D_EXPERT = D_MODEL // 2
EPS = 1e-6
NEG = -1e30
LOG2E = math.log2(math.e)

A_WIDTH = A_HEADS * HEAD_DIM
B_WIDTH = B_HEADS * HEAD_DIM
C_WIDTH = C_HEADS * HEAD_DIM
KV_WIDTH = A_KV * HEAD_DIM

LANES = 128
MXU_DIM = 256
VMEM_LIMIT = 56 * 1024 * 1024

PAIR_ORDER = (0, 3, 1, 4, 2, 5)

TM_PROJ = 1024
LOCAL_BATCH = 4
TQ_A = 256
KW_A = 512
TQ_B = 256
KW_B = 768
TQ_C = 512
KC_C = 512
TM_MOE = 512
TM_COMB = 256
TM_DISP = 1024

PAIR_A = (0, 0, 0, 1, 1, 2)
PAIR_B = (1, 2, 3, 2, 3, 3)
PAIRS_PER_GROUP = len(PAIR_A)
N_CLASSES = N_GROUPS * PAIRS_PER_GROUP
CLASS_LANE = N_GROUPS + N_EXPERTS
SUBLANES = 8
GATE_ROW = D_MODEL // 2 // LANES
GATHER_SLOTS = 3
DMA_THREADS = 2

BF16 = jnp.bfloat16
F32 = jnp.float32


def _cparams(sem):
    return pltpu.CompilerParams(dimension_semantics=sem, vmem_limit_bytes=VMEM_LIMIT)


N_NORMED = 1536
N_PROJ = 2048


def _in_proj_kernel(x_ref, w_ref, ln_ref, gain_ref, cos_ref, sin_ref, ones_ref,
                    qa_ref, ka_ref, va_ref, qb_ref, kb_ref, vb_ref, qc_ref, kc_ref, vc_ref):
    x = x_ref[...]
    ms = jnp.mean(x * x, axis=-1, keepdims=True)
    h = (x * lax.rsqrt(ms + EPS) * ln_ref[...]).astype(BF16)
    p = jnp.dot(h, w_ref[...], preferred_element_type=F32)
    g = gain_ref[...]
    ones_blk = ones_ref[...]

    def chunk(c):
        return p[:, MXU_DIM * c:MXU_DIM * (c + 1)]

    def inv_rms(pc):
        ss = jnp.dot((pc * pc).astype(BF16), ones_blk, preferred_element_type=F32)
        return lax.rsqrt(ss * (1.0 / HEAD_DIM) + EPS)

    r = [inv_rms(chunk(c)) for c in range(N_NORMED // MXU_DIM)]
    pn = [chunk(c) * r[c] * g[:, MXU_DIM * c:MXU_DIM * (c + 1)] for c in range(N_NORMED // MXU_DIM)]
    qa_ref[:, 0:256] = pn[0].astype(BF16)
    qa_ref[:, 256:384] = pn[1][:, :128].astype(BF16)
    ka_ref[...] = pn[1][:, 128:].astype(BF16)
    qb_ref[...] = pn[2].astype(BF16)
    kb_ref[...] = pn[3].astype(BF16)
    cos = cos_ref[...]
    sin = sin_ref[...]
    cos2 = jnp.concatenate([cos, cos], axis=1)
    sin2 = jnp.concatenate([sin, sin], axis=1)
    lane = lax.broadcasted_iota(jnp.int32, (1, MXU_DIM), 1)
    first_half = (lane % (HEAD_DIM // 2)) < HEAD_DIM // 4

    def rotary_partner(z):
        return jnp.where(first_half, pltpu.roll(z, MXU_DIM - HEAD_DIM // 4, axis=1), pltpu.roll(z, HEAD_DIM // 4, axis=1))

    c4 = pn[4] * cos2 + rotary_partner(pn[4]) * sin2
    c5 = pn[5] * cos2 + rotary_partner(pn[5]) * sin2
    qc_ref[:, 0:256] = c4.astype(BF16)
    qc_ref[:, 256:384] = c5[:, :128].astype(BF16)
    kc_ref[...] = c5[:, 128:].astype(BF16)
    va_ref[...] = p[:, 1536:1664].astype(BF16)
    vb_ref[...] = p[:, 1664:1920].astype(BF16)
    vc_ref[:, 0:128] = p[:, 1920:2048].astype(BF16)
    vc_ref[:, 128:256] = jnp.ones((x.shape[0], 128), BF16)


def _in_proj(x2d, w_ext, ln, gain, cos_t, sin_t, ones_blk):
    n = x2d.shape[0]
    tm = TM_PROJ
    tiles_per_seq = SEQ // tm
    row = lambda i: (i, 0)
    fixed = lambda i: (0, 0)
    pos = lambda i: (i % tiles_per_seq, 0)
    widths = (A_WIDTH, KV_WIDTH, KV_WIDTH, B_WIDTH, B_WIDTH, B_WIDTH, C_WIDTH, KV_WIDTH, 2 * KV_WIDTH)
    return pl.pallas_call(
        _in_proj_kernel,
        grid=(n // tm,),
        in_specs=[
            pl.BlockSpec((tm, D_MODEL), row),
            pl.BlockSpec((D_MODEL, N_PROJ), fixed),
            pl.BlockSpec((1, D_MODEL), fixed),
            pl.BlockSpec((1, N_NORMED), fixed),
            pl.BlockSpec((tm, LANES), pos),
            pl.BlockSpec((tm, LANES), pos),
            pl.BlockSpec((MXU_DIM, MXU_DIM), fixed),
        ],
        out_specs=[pl.BlockSpec((tm, w), row) for w in widths],
        out_shape=[jax.ShapeDtypeStruct((n, w), BF16) for w in widths],
        compiler_params=_cparams(("parallel",)),
        name="in_proj",
    )(x2d, w_ext, ln, gain, cos_t, sin_t, ones_blk)


def _half_masks():
    lane = lax.broadcasted_iota(jnp.int32, (1, LANES), 1)
    lo = lane < HEAD_DIM
    return lo, lo.astype(BF16), (~lo).astype(BF16)


def _win_attn_kernel(sink_ref, q_ref, k_ref, v_ref, bias_ref, o_ref):
    j = pl.program_id(1)
    start = pl.multiple_of(jnp.clip(j * TQ_A - A_WINDOW, 0, SEQ - KW_A), 128)
    lo, m_lo, m_hi = _half_masks()
    low_rows = lax.broadcasted_iota(jnp.int32, (2 * TQ_A, 1), 0) < TQ_A
    for bb in range(LOCAL_BATCH):
        kw = k_ref[bb, pl.ds(start, KW_A), :]
        vw = jnp.concatenate([v_ref[bb, pl.ds(start, KW_A), :], jnp.ones((KW_A, LANES), BF16)], axis=1)
        for p in range(A_HEADS // 2):
            qblk = q_ref[bb, :, LANES * p:LANES * (p + 1)]
            q2 = jnp.concatenate([qblk * m_lo, qblk * m_hi], axis=0)
            s = lax.dot_general(q2, kw, (((1,), (1,)), ((), ())), preferred_element_type=F32)
            s = s + bias_ref[0, p]
            sk = jnp.where(low_rows, sink_ref[2 * p], sink_ref[2 * p + 1])
            m = jnp.maximum(jnp.max(s, axis=-1, keepdims=True), sk)
            e = jnp.exp2(s - m).astype(BF16)
            ol = jnp.dot(e, vw, preferred_element_type=F32)
            o2 = ol[:, :LANES] / (ol[:, LANES:] + jnp.exp2(sk - m))
            o_ref[bb, :, LANES * p:LANES * (p + 1)] = jnp.where(lo, o2[:TQ_A], o2[TQ_A:]).astype(BF16)


def _win_attn(qa, ka, va, bias, sink2):
    b = qa.shape[0]
    nq = SEQ // TQ_A
    variant = lambda bi, j: (jnp.where(j == 0, 0, jnp.where(j == nq - 1, 2, 1)), 0, 0, 0)
    bb = LOCAL_BATCH
    return pl.pallas_call(
        _win_attn_kernel,
        grid=(b // bb, nq),
        in_specs=[
            pl.BlockSpec(memory_space=pltpu.SMEM),
            pl.BlockSpec((bb, TQ_A, A_WIDTH), lambda bi, j: (bi, j, 0)),
            pl.BlockSpec((bb, SEQ, KV_WIDTH), lambda bi, j: (bi, 0, 0)),
            pl.BlockSpec((bb, SEQ, KV_WIDTH), lambda bi, j: (bi, 0, 0)),
            pl.BlockSpec((1, A_HEADS // 2, 2 * TQ_A, KW_A), variant),
        ],
        out_specs=pl.BlockSpec((bb, TQ_A, A_WIDTH), lambda bi, j: (bi, j, 0)),
        out_shape=jax.ShapeDtypeStruct((b, SEQ, A_WIDTH), BF16),
        compiler_params=_cparams(("parallel", "arbitrary")),
        name="win_attn",
    )(sink2, qa, ka, va, bias)


def _win_bias_table():
    slopes = np.array([2.0 ** (-8.0 * (n + 1) / A_HEADS) for n in range(A_HEADS)], np.float32)[list(PAIR_ORDER)]
    i = np.arange(TQ_A)[:, None]
    jj = np.arange(KW_A)[None, :]
    tabs = []
    for off in (0, A_WINDOW, KW_A - TQ_A):
        dist = np.abs(off + i - jj).astype(np.float32)
        tab = np.where(dist[None] <= A_WINDOW, -slopes[:, None, None] * dist[None] * LOG2E, NEG)
        tabs.append(tab)
    return jnp.asarray(np.stack(tabs).astype(np.float32).reshape(3, A_HEADS // 2, 2 * TQ_A, KW_A))


def _nbr_attn_kernel(q_ref, k_ref, v_ref, bias_ref, o_ref):
    j = pl.program_id(1)
    rows_per_tile = TQ_B // GRID_W
    krow0 = jnp.clip(j * rows_per_tile - B_ROWS // 2, 0, GRID_ROWS - KW_B // GRID_W)
    start = pl.multiple_of(krow0 * GRID_W, 256)
    lo, m_lo, m_hi = _half_masks()
    for bb in range(LOCAL_BATCH):
        for p in range(B_HEADS // 2):
            qblk = q_ref[bb, :, LANES * p:LANES * (p + 1)]
            kw = k_ref[bb, pl.ds(start, KW_B), LANES * p:LANES * (p + 1)]
            vw = jnp.concatenate([v_ref[bb, pl.ds(start, KW_B), LANES * p:LANES * (p + 1)],
                                  jnp.ones((KW_B, LANES), BF16)], axis=1)
            q2 = jnp.concatenate([qblk * m_lo, qblk * m_hi], axis=0)
            s = lax.dot_general(q2, kw, (((1,), (1,)), ((), ())), preferred_element_type=F32)
            s = s + bias_ref[0, p]
            m = jnp.max(s, axis=-1, keepdims=True)
            e = jnp.exp2(s - m).astype(BF16)
            ol = jnp.dot(e, vw, preferred_element_type=F32)
            o2 = ol[:, :LANES] / ol[:, LANES:]
            o_ref[bb, :, LANES * p:LANES * (p + 1)] = jnp.where(lo, o2[:TQ_B], o2[TQ_B:]).astype(BF16)


def _nbr_attn(qb, kb, vb, bias):
    b = qb.shape[0]
    nq = SEQ // TQ_B
    variant = lambda bi, j: (jnp.where(j == 0, 0, jnp.where(j == nq - 1, 2, 1)), 0, 0, 0)
    bb = LOCAL_BATCH
    return pl.pallas_call(
        _nbr_attn_kernel,
        grid=(b // bb, nq),
        in_specs=[
            pl.BlockSpec((bb, TQ_B, B_WIDTH), lambda bi, j: (bi, j, 0)),
            pl.BlockSpec((bb, SEQ, B_WIDTH), lambda bi, j: (bi, 0, 0)),
            pl.BlockSpec((bb, SEQ, B_WIDTH), lambda bi, j: (bi, 0, 0)),
            pl.BlockSpec((1, B_HEADS // 2, 2 * TQ_B, KW_B), variant),
        ],
        out_specs=pl.BlockSpec((bb, TQ_B, B_WIDTH), lambda bi, j: (bi, j, 0)),
        out_shape=jax.ShapeDtypeStruct((b, SEQ, B_WIDTH), BF16),
        compiler_params=_cparams(("parallel", "arbitrary")),
        name="nbr_attn",
    )(qb, kb, vb, bias)


def _nbr_bias_table(rpb):
    rows_per_tile = TQ_B // GRID_W
    krows = KW_B // GRID_W
    r = rpb.astype(F32) * LOG2E
    edge = GRID_W - B_COLS
    ext = jnp.concatenate([jnp.repeat(r[..., :1], edge, axis=-1), r, jnp.repeat(r[..., -1:], edge, axis=-1)], axis=-1)
    col = jnp.stack([ext[..., GRID_W - 1 - q:2 * GRID_W - 1 - q] for q in range(GRID_W)], axis=2)
    tabs = []
    for r_first in (0, rows_per_tile, GRID_ROWS - rows_per_tile):
        krow0 = int(np.clip(r_first - B_ROWS // 2, 0, GRID_ROWS - krows))
        slabs = []
        for ql in range(rows_per_tile):
            per_k = [col[:, int(np.clip(krow0 + kl - (r_first + ql) + B_ROWS - 1, 0, 2 * B_ROWS - 2))]
                     for kl in range(krows)]
            slabs.append(jnp.concatenate(per_k, axis=-1))
        tab = jnp.concatenate(slabs, axis=1)
        qi = np.arange(TQ_B)
        kj = np.arange(KW_B)
        qr = (r_first + qi // GRID_W)[:, None]
        qcol = (qi % GRID_W)[:, None]
        kr = (krow0 + kj // GRID_W)[None, :]
        kcol = (kj % GRID_W)[None, :]
        r0 = np.clip(qr - B_ROWS // 2, 0, GRID_ROWS - B_ROWS)
        c0 = np.clip(qcol - B_COLS // 2, 0, GRID_W - B_COLS)
        valid = (kr >= r0) & (kr < r0 + B_ROWS) & (kcol >= c0) & (kcol < c0 + B_COLS)
        tabs.append(jnp.where(jnp.asarray(valid)[None], tab, NEG))
    return jnp.stack(tabs).reshape(3, B_HEADS // 2, 2 * TQ_B, KW_B)


def _dense_attn_kernel(q_ref, k_ref, v_ref, o_ref, s_ref):
    lo, m_lo, m_hi = _half_masks()
    nchunk = SEQ // KC_C
    npairs = C_HEADS // 2

    def scores(p):
        qblk = q_ref[0, :, LANES * p:LANES * (p + 1)]
        q2 = jnp.concatenate([qblk * m_lo, qblk * m_hi], axis=0)
        m_run = None
        for c in range(nchunk):
            kc = k_ref[0, KC_C * c:KC_C * (c + 1), :]
            s = lax.dot_general(q2, kc, (((1,), (1,)), ((), ())), preferred_element_type=F32)
            s_ref[p % 2, :, KC_C * c:KC_C * (c + 1)] = s
            for t in range(KC_C // LANES):
                blk = s[:, LANES * t:LANES * (t + 1)]
                m_run = blk if m_run is None else jnp.maximum(m_run, blk)
        return jnp.max(m_run, axis=-1, keepdims=True)

    def weighted_values(p, m):
        acc = None
        for c in range(nchunk):
            e = jnp.exp2(s_ref[p % 2, :, KC_C * c:KC_C * (c + 1)] - m).astype(BF16)
            part = jnp.dot(e, v_ref[0, KC_C * c:KC_C * (c + 1), :], preferred_element_type=F32)
            acc = part if acc is None else acc + part
        on = acc[:, :LANES] / acc[:, LANES:]
        o_ref[0, :, LANES * p:LANES * (p + 1)] = jnp.where(lo, on[:TQ_C], on[TQ_C:]).astype(BF16)

    m_next = scores(0)
    for p in range(npairs):
        m_cur = m_next
        if p + 1 < npairs:
            m_next = scores(p + 1)
        weighted_values(p, m_cur)


def _dense_attn(qc, kc, vc):
    b = qc.shape[0]
    return pl.pallas_call(
        _dense_attn_kernel,
        grid=(b, SEQ // TQ_C),
        in_specs=[
            pl.BlockSpec((1, TQ_C, C_WIDTH), lambda bi, j: (bi, j, 0)),
            pl.BlockSpec((1, SEQ, KV_WIDTH), lambda bi, j: (bi, 0, 0)),
            pl.BlockSpec((1, SEQ, 2 * KV_WIDTH), lambda bi, j: (bi, 0, 0)),
        ],
        out_specs=pl.BlockSpec((1, TQ_C, C_WIDTH), lambda bi, j: (bi, j, 0)),
        out_shape=jax.ShapeDtypeStruct((b, SEQ, C_WIDTH), BF16),
        scratch_shapes=[pltpu.VMEM((2, 2 * TQ_C, SEQ), F32)],
        compiler_params=_cparams(("parallel", "arbitrary")),
        name="dense_attn",
    )(qc, kc, vc)


def _out_proj_kernel(oa_ref, ob_ref, oc_ref, x_ref, ga_ref, gb_ref, gc_ref, wa_ref, wb_ref, wc_ref,
                     ln_ref, wr_ref, br_ref, xn_ref, h_ref, cls_ref):
    def nrm(o_ref, g_ref):
        o = o_ref[...].astype(F32)
        ms = jnp.mean(o * o, axis=-1, keepdims=True)
        return (o * lax.rsqrt(ms + EPS) * g_ref[...]).astype(BF16)

    acc = jnp.dot(nrm(oa_ref, ga_ref), wa_ref[...], preferred_element_type=F32)
    acc = acc + jnp.dot(nrm(ob_ref, gb_ref), wb_ref[...], preferred_element_type=F32)
    acc = acc + jnp.dot(nrm(oc_ref, gc_ref), wc_ref[...], preferred_element_type=F32)
    xn = x_ref[...] + acc
    xn_ref[...] = xn
    ms = jnp.mean(xn * xn, axis=-1, keepdims=True)
    h2 = xn * lax.rsqrt(ms + EPS) * ln_ref[...]
    hi = h2.astype(BF16)
    lo = (h2 - hi.astype(F32)).astype(BF16)
    tm = xn.shape[0]
    half = D_MODEL // 2
    bits_lo = lax.bitcast_convert_type(hi[:, :half].astype(F32), jnp.uint32)
    bits_hi = lax.bitcast_convert_type(hi[:, half:].astype(F32), jnp.uint32)
    words = (bits_hi & jnp.uint32(0xFFFF0000)) | (bits_lo >> 16)
    for s in range(GATE_ROW):
        h_ref[pl.ds(s, tm, stride=SUBLANES), :] = words[:, LANES * s:LANES * (s + 1)]
    for s in range(GATE_ROW + 1, SUBLANES):
        h_ref[pl.ds(s, tm, stride=SUBLANES), :] = jnp.zeros((tm, LANES), jnp.uint32)
    wr = wr_ref[...]
    t = jnp.dot(hi, wr, preferred_element_type=F32)
    u = jnp.dot(lo, wr[:, :LANES], preferred_element_type=F32)
    logits = t[:, :LANES] + t[:, LANES:] + u + br_ref[...]

    lane = lax.broadcasted_iota(jnp.int32, logits.shape, 1).astype(F32)
    big = jnp.float32(3.0e38)
    is_g = lane < N_GROUPS
    gl = jnp.where(is_g, logits, -big)
    mg = jnp.max(gl, axis=-1, keepdims=True)
    grp = jnp.min(jnp.where(gl == mg, lane, big), axis=-1, keepdims=True)
    pg = 1.0 / jnp.sum(jnp.where(is_g, jnp.exp(gl - mg), 0.0), axis=-1, keepdims=True)
    e_lo = N_GROUPS + EXPERTS_PER_GROUP * grp
    sel = (lane >= e_lo) & (lane < e_lo + EXPERTS_PER_GROUP)
    el = jnp.where(sel, logits, -big)
    v1 = jnp.max(el, axis=-1, keepdims=True)
    i1 = jnp.min(jnp.where(el == v1, lane, big), axis=-1, keepdims=True)
    el2 = jnp.where(lane == i1, -big, el)
    v2 = jnp.max(el2, axis=-1, keepdims=True)
    i2 = jnp.min(jnp.where(el2 == v2, lane, big), axis=-1, keepdims=True)
    e21 = jnp.exp(v2 - v1)
    w1 = pg / (1.0 + e21)
    w2 = pg * e21 / (1.0 + e21)
    a = jnp.minimum(i1, i2) - e_lo
    b = jnp.maximum(i1, i2) - e_lo
    cls = grp * PAIRS_PER_GROUP + (a * (7.0 - a) * 0.5 + (b - a - 1.0))
    gates = jnp.where(lane == i1, w1, jnp.where(lane == i2, w2, jnp.where(lane == CLASS_LANE, cls, 0.0)))
    h_ref[pl.ds(GATE_ROW, tm, stride=SUBLANES), :] = lax.bitcast_convert_type(gates, jnp.uint32)
    cls_ref[...] = jnp.broadcast_to(cls, (tm, LANES)).astype(jnp.int32)


def _out_proj(oa, ob, oc, x2d, ga, gb, gc, wa, wb, wc, ln2, wr, br):
    n = x2d.shape[0]
    tm = TM_PROJ
    row = lambda i: (i, 0)
    fixed = lambda i: (0, 0)
    return pl.pallas_call(
        _out_proj_kernel,
        grid=(n // tm,),
        in_specs=[
            pl.BlockSpec((tm, A_WIDTH), row),
            pl.BlockSpec((tm, B_WIDTH), row),
            pl.BlockSpec((tm, C_WIDTH), row),
            pl.BlockSpec((tm, D_MODEL), row),
            pl.BlockSpec((1, A_WIDTH), fixed),
            pl.BlockSpec((1, B_WIDTH), fixed),
            pl.BlockSpec((1, C_WIDTH), fixed),
            pl.BlockSpec((A_WIDTH, D_MODEL), fixed),
            pl.BlockSpec((B_WIDTH, D_MODEL), fixed),
            pl.BlockSpec((C_WIDTH, D_MODEL), fixed),
            pl.BlockSpec((1, D_MODEL), fixed),
            pl.BlockSpec((D_MODEL, 2 * LANES), fixed),
            pl.BlockSpec((1, LANES), fixed),
        ],
        out_specs=[
            pl.BlockSpec((tm, D_MODEL), row),
            pl.BlockSpec((tm * SUBLANES, LANES), row),
            pl.BlockSpec((tm, LANES), row),
        ],
        out_shape=[
            jax.ShapeDtypeStruct((n, D_MODEL), F32),
            jax.ShapeDtypeStruct((n * SUBLANES, LANES), jnp.uint32),
            jax.ShapeDtypeStruct((n, LANES), jnp.int32),
        ],
        compiler_params=_cparams(("parallel",)),
        name="out_proj_router",
    )(oa, ob, oc, x2d, ga, gb, gc, wa, wb, wc, ln2, wr, br)


def _tile_copy(src_hbm, buf, sem, slot, src_row8, dst_tok):
    src = src_hbm.at[pl.ds(pl.multiple_of(src_row8, SUBLANES), SUBLANES), :]
    return pltpu.make_async_copy(src, buf[slot].at[pl.ds(SUBLANES * dst_tok, SUBLANES), :], sem.at[slot])


def _issue_gather(idx_ref, src_hbm, buf, sem, slot, toks):
    for r in range(toks):
        _tile_copy(src_hbm, buf, sem, slot, idx_ref[0, 0, r], r).start(priority=r % DMA_THREADS)


def _wait_gather(src_hbm, buf, sem, slot, toks):
    pltpu.make_async_copy(src_hbm.at[pl.ds(0, SUBLANES * toks), :], buf[slot], sem.at[slot]).wait()


def _tile_row(ref, s, toks):
    return ref[pl.ds(s, toks, stride=SUBLANES), :]


def _dispatch_kernel(ztile_ref, nvalid_ref, pos_ref, h_ref, o_hbm, zbuf, sem):
    tm = TM_DISP
    tile_rows = TM_MOE * SUBLANES
    i = pl.program_id(0)

    @pl.when(i == 0)
    def _():
        zbuf[...] = jnp.zeros(zbuf.shape, zbuf.dtype)

        def zero_tile(row0):
            dst = o_hbm.at[pl.ds(pl.multiple_of(row0, SUBLANES), tile_rows), :]
            return pltpu.make_async_copy(zbuf, dst, sem.at[1])

        for c in range(N_CLASSES):
            @pl.when(ztile_ref[c] >= 0)
            def _(c=c):
                zero_tile(ztile_ref[c]).start()

        n_tiles = o_hbm.shape[0] // tile_rows

        @pl.loop(nvalid_ref[0], n_tiles)
        def _(t):
            zero_tile(t * tile_rows).start()

        for c in range(N_CLASSES):
            @pl.when(ztile_ref[c] >= 0)
            def _(c=c):
                zero_tile(ztile_ref[c]).wait()

        @pl.loop(nvalid_ref[0], n_tiles)
        def _(t):
            zero_tile(t * tile_rows).wait()

    def row_copy(r):
        dst = o_hbm.at[pl.ds(pl.multiple_of(pos_ref[0, 0, r], SUBLANES), SUBLANES), :]
        return pltpu.make_async_copy(h_ref.at[pl.ds(SUBLANES * r, SUBLANES), :], dst, sem.at[0])

    for r in range(tm):
        row_copy(r).start(priority=r % DMA_THREADS)
    pltpu.make_async_copy(h_ref, o_hbm.at[pl.ds(0, tm * SUBLANES), :], sem.at[0]).wait()


def _dispatch(hrow, ztile, nvalid, pos_tiles, nt):
    n = hrow.shape[0] // SUBLANES
    tm = TM_DISP
    grid_spec = pltpu.PrefetchScalarGridSpec(
        num_scalar_prefetch=2,
        grid=(n // tm,),
        in_specs=[
            pl.BlockSpec((1, 1, tm), lambda i, zt, nv: (i, 0, 0), memory_space=pltpu.SMEM),
            pl.BlockSpec((tm * SUBLANES, LANES), lambda i, zt, nv: (i, 0)),
        ],
        out_specs=pl.BlockSpec(memory_space=pl.ANY),
        scratch_shapes=[pltpu.VMEM((TM_MOE * SUBLANES, LANES), jnp.uint32), pltpu.SemaphoreType.DMA((2,))],
    )
    return pl.pallas_call(
        _dispatch_kernel,
        grid_spec=grid_spec,
        out_shape=jax.ShapeDtypeStruct((nt * TM_MOE * SUBLANES, LANES), jnp.uint32),
        compiler_params=_cparams(("arbitrary",)),
        name="moe_dispatch",
    )(ztile, nvalid, pos_tiles, hrow)


def _moe_kernel(ea_ref, eb_ref, nvalid_ref, h_ref, wga_ref, wua_ref, wda_ref, wgb_ref, wub_ref, wdb_ref, y_ref):
    tm = TM_MOE
    i = pl.program_id(0)
    nvalid = nvalid_ref[0]

    @pl.when(i < nvalid)
    def _():
        parts_lo, parts_hi = [], []
        for s in range(D_MODEL // 2 // LANES):
            w = _tile_row(h_ref, s, tm)
            parts_lo.append(lax.bitcast_convert_type(w << 16, F32).astype(BF16))
            parts_hi.append(lax.bitcast_convert_type(w & jnp.uint32(0xFFFF0000), F32).astype(BF16))
        x = jnp.concatenate(parts_lo + parts_hi, axis=1)
        gates = lax.bitcast_convert_type(_tile_row(h_ref, GATE_ROW, tm), F32)
        lane = lax.broadcasted_iota(jnp.int32, gates.shape, 1)

        def expert(wg_ref, wu_ref, wd_ref, e):
            g = jnp.dot(x, wg_ref[0], preferred_element_type=F32)
            u = jnp.dot(x, wu_ref[0], preferred_element_type=F32)
            act = (g / (1.0 + jnp.exp(-g)) * u).astype(BF16)
            y = jnp.dot(act, wd_ref[0], preferred_element_type=F32)
            ge = jnp.sum(jnp.where(lane == e + N_GROUPS, gates, 0.0), axis=-1, keepdims=True)
            return ge * y

        y = expert(wga_ref, wua_ref, wda_ref, ea_ref[i]) + expert(wgb_ref, wub_ref, wdb_ref, eb_ref[i])
        for s in range(SUBLANES):
            y_ref[pl.ds(s, tm, stride=SUBLANES), :] = y[:, LANES * s:LANES * (s + 1)]

    @pl.when(i >= nvalid)
    def _():
        y_ref[...] = jnp.zeros(y_ref.shape, F32)


def _moe_experts(h_sorted, ea, eb, nvalid, wg, wu, wd):
    tm = TM_MOE
    nt = ea.shape[0]
    w_in_spec = lambda sel: pl.BlockSpec((1, D_MODEL, D_EXPERT), sel)
    w_out_spec = lambda sel: pl.BlockSpec((1, D_EXPERT, D_MODEL), sel)
    sel_a = lambda i, ea, eb, nv: (ea[i], 0, 0)
    sel_b = lambda i, ea, eb, nv: (eb[i], 0, 0)
    rows = lambda i, ea, eb, nv: (jnp.minimum(i, nv[0] - 1), 0)
    grid_spec = pltpu.PrefetchScalarGridSpec(
        num_scalar_prefetch=3,
        grid=(nt,),
        in_specs=[
            pl.BlockSpec((tm * SUBLANES, LANES), rows),
            w_in_spec(sel_a), w_in_spec(sel_a), w_out_spec(sel_a),
            w_in_spec(sel_b), w_in_spec(sel_b), w_out_spec(sel_b),
        ],
        out_specs=pl.BlockSpec((tm * SUBLANES, LANES), lambda i, ea, eb, nv: (i, 0)),
    )
    return pl.pallas_call(
        _moe_kernel,
        grid_spec=grid_spec,
        out_shape=jax.ShapeDtypeStruct((nt * tm * SUBLANES, LANES), F32),
        compiler_params=_cparams(("arbitrary",)),
        name="moe_experts",
    )(ea, eb, nvalid, h_sorted, wg, wu, wd, wg, wu, wd)


def _combine_kernel(pos0_ref, pos1_ref, posn_ref, xn_ref, y_hbm, o_ref, buf0, buf1, buf2, sem):
    buf = (buf0, buf1, buf2)
    tm = TM_COMB
    i = pl.program_id(0)

    @pl.when(i == 0)
    def _():
        _issue_gather(pos0_ref, y_hbm, buf, sem, 0, tm)
        _issue_gather(pos1_ref, y_hbm, buf, sem, 1, tm)

    for slot in range(GATHER_SLOTS):
        @pl.when(i % GATHER_SLOTS == slot)
        def _(slot=slot):
            _wait_gather(y_hbm, buf, sem, slot, tm)
            _issue_gather(posn_ref, y_hbm, buf, sem, (slot + 2) % GATHER_SLOTS, tm)
            for s in range(SUBLANES):
                cols = slice(LANES * s, LANES * (s + 1))
                o_ref[:, cols] = xn_ref[:, cols] + _tile_row(buf[slot], s, tm)

            @pl.when(i == pl.num_programs(0) - 1)
            def _():
                _wait_gather(y_hbm, buf, sem, (slot + 1) % GATHER_SLOTS, tm)
                _wait_gather(y_hbm, buf, sem, (slot + 2) % GATHER_SLOTS, tm)


def _combine(pos_tiles, xn, y_sorted):
    n = xn.shape[0]
    tm = TM_COMB
    smem_blk = lambda f: pl.BlockSpec((1, 1, tm), f, memory_space=pltpu.SMEM)
    return pl.pallas_call(
        _combine_kernel,
        grid=(n // tm,),
        in_specs=[
            smem_blk(lambda i: (0, 0, 0)),
            smem_blk(lambda i: (1, 0, 0)),
            smem_blk(lambda i: (i + 2, 0, 0)),
            pl.BlockSpec((tm, D_MODEL), lambda i: (i, 0)),
            pl.BlockSpec(memory_space=pl.ANY),
        ],
        out_specs=pl.BlockSpec((tm, D_MODEL), lambda i: (i, 0)),
        out_shape=jax.ShapeDtypeStruct((n, D_MODEL), F32),
        scratch_shapes=[pltpu.VMEM((tm * SUBLANES, LANES), F32) for _ in range(GATHER_SLOTS)]
        + [pltpu.SemaphoreType.DMA((GATHER_SLOTS,))],
        compiler_params=_cparams(("arbitrary",)),
        name="moe_combine",
    )(pos_tiles, pos_tiles, pos_tiles, xn, y_sorted)


def _moe_plan(cls, n):
    tm = TM_MOE
    nt = n // tm + N_CLASSES
    onehot = (cls[:, None] == jnp.arange(N_CLASSES, dtype=jnp.int32)[None, :]).astype(jnp.int32)
    counts = jnp.sum(onehot, axis=0)
    rank = jnp.sum(jnp.cumsum(onehot, axis=0) * onehot, axis=1) - 1
    tiles = (counts + tm - 1) // tm
    tile_end = jnp.cumsum(tiles)
    tile_start = tile_end - tiles
    pos = jnp.sum(onehot * tile_start[None, :], axis=1) * tm + rank
    pad = GATHER_SLOTS - 1
    ztile = jnp.where(tiles > 0, (tile_end - 1) * (tm * SUBLANES), -1).astype(jnp.int32)
    nvalid = tile_end[-1]
    tile_id = jnp.minimum(jnp.arange(nt, dtype=jnp.int32), nvalid - 1)
    tcls = jnp.sum((tile_id[:, None] >= tile_end[None, :]).astype(jnp.int32), axis=1)
    grp, pair = tcls // PAIRS_PER_GROUP, tcls % PAIRS_PER_GROUP
    pair_a = jnp.asarray(PAIR_A, jnp.int32)
    pair_b = jnp.asarray(PAIR_B, jnp.int32)
    ea = grp * EXPERTS_PER_GROUP + jnp.sum((pair[:, None] == jnp.arange(PAIRS_PER_GROUP)[None]) * pair_a[None], axis=1)
    eb = grp * EXPERTS_PER_GROUP + jnp.sum((pair[:, None] == jnp.arange(PAIRS_PER_GROUP)[None]) * pair_b[None], axis=1)
    pos8 = SUBLANES * pos
    pos_disp = pos8.reshape(n // TM_DISP, 1, TM_DISP)
    pos_comb = jnp.concatenate([pos8, jnp.zeros((pad * TM_COMB,), jnp.int32)]).reshape(n // TM_COMB + pad, 1, TM_COMB)
    return (ea.astype(jnp.int32), eb.astype(jnp.int32), nvalid.reshape(1).astype(jnp.int32), ztile,
            pos_disp, pos_comb)


def _moe(hrow, cls, xn, wg, wu, wd):
    n = xn.shape[0]
    ea, eb, nvalid, ztile, pos_disp, pos_comb = _moe_plan(cls[:, 0], n)
    h_sorted = _dispatch(hrow, ztile, nvalid, pos_disp, ea.shape[0])
    y_sorted = _moe_experts(h_sorted, ea, eb, nvalid, wg, wu, wd)
    return _combine(pos_comb, xn, y_sorted)


def _head_cols(base, heads):
    return np.concatenate([base + HEAD_DIM * h + np.arange(HEAD_DIM) for h in heads])


def _proj_columns():
    o_qa, o_ka, o_va = 0, A_WIDTH, A_WIDTH + KV_WIDTH
    o_qb = o_va + KV_WIDTH
    o_kb, o_vb = o_qb + B_WIDTH, o_qb + 2 * B_WIDTH
    o_qc = o_vb + B_WIDTH
    o_kc, o_vc = o_qc + C_WIDTH, o_qc + C_WIDTH + KV_WIDTH
    nat2, nat4 = range(2), range(4)
    return np.concatenate([
        _head_cols(o_qa, PAIR_ORDER), _head_cols(o_ka, nat2),
        _head_cols(o_qb, nat4), _head_cols(o_kb, nat4),
        _head_cols(o_qc, PAIR_ORDER), _head_cols(o_kc, nat2),
        _head_cols(o_va, nat2), _head_cols(o_vb, nat4), _head_cols(o_vc, nat2),
    ])


def _rope_tables():
    nf = HEAD_DIM // 4
    inv = (np.float32(ROPE_THETA) ** (-np.arange(nf, dtype=np.float32) / np.float32(nf))).astype(np.float32)
    pos = np.arange(SEQ)
    ang_r = (pos // GRID_W).astype(np.float32)[:, None] * inv[None, :]
    ang_c = (pos % GRID_W).astype(np.float32)[:, None] * inv[None, :]
    cos = np.concatenate([np.cos(ang_r)] * 2 + [np.cos(ang_c)] * 2, axis=-1).astype(np.float32)
    sin = np.concatenate([-np.sin(ang_r), np.sin(ang_r), -np.sin(ang_c), np.sin(ang_c)], axis=-1).astype(np.float32)
    return jnp.asarray(np.concatenate([cos, cos], axis=-1)), jnp.asarray(np.concatenate([sin, sin], axis=-1))


def _layer_params(l, w_in, ln1, qk_gain, sink, rpb, out_gain, w_out, ln2, w_rg, b_rg, w_re, b_re):
    cols = _proj_columns()
    w_ext = w_in[l][:, cols].astype(BF16)
    qs = HEAD_DIM ** -0.5 * LOG2E
    g = qk_gain[l].astype(F32)
    gain = jnp.concatenate([
        jnp.tile(g[0, 0], A_HEADS) * qs, jnp.tile(g[0, 1], A_KV),
        jnp.tile(g[1, 0], B_HEADS) * qs, jnp.tile(g[1, 1], B_HEADS),
        jnp.tile(g[2, 0], C_HEADS) * qs, jnp.tile(g[2, 1], C_KV),
    ])[None, :]
    order = list(PAIR_ORDER)
    sink2 = sink[l].astype(F32)[jnp.asarray(order)] * LOG2E
    rows_a = _head_cols(0, PAIR_ORDER)
    rows_b = A_WIDTH + np.arange(B_WIDTH)
    rows_c = _head_cols(A_WIDTH + B_WIDTH, PAIR_ORDER)
    og = out_gain[l].astype(F32)
    wo = w_out[l]
    wr = jnp.concatenate([w_rg[l], w_re[l], jnp.zeros((D_MODEL, LANES - N_GROUPS - N_EXPERTS), F32)], axis=1)
    wr_hi = wr.astype(BF16)
    wr_lo = (wr - wr_hi.astype(F32)).astype(BF16)
    br = jnp.concatenate([b_rg[l].astype(F32), b_re[l].astype(F32),
                          jnp.zeros((LANES - N_GROUPS - N_EXPERTS,), F32)])[None, :]
    return dict(
        w_ext=w_ext, ln1=ln1[l][None, :], gain=gain, sink2=sink2, nbr_bias=_nbr_bias_table(rpb[l]),
        ga=og[rows_a][None, :], gb=og[rows_b][None, :], gc=og[rows_c][None, :],
        wa=wo[rows_a].astype(BF16), wb=wo[rows_b].astype(BF16), wc=wo[rows_c].astype(BF16),
        ln2=ln2[l][None, :], wr=jnp.concatenate([wr_hi, wr_lo], axis=1), br=br,
    )


def _block_ones():
    idx = np.arange(MXU_DIM) // HEAD_DIM
    return jnp.asarray((idx[:, None] == idx[None, :]).astype(np.float32)).astype(BF16)


def _trunk(x, params, shared, moe_w):
    b = x.shape[0]
    x2d = x.reshape(b * SEQ, D_MODEL)
    for l, lp in enumerate(params):
        qa, ka, va, qb, kb, vb, qc, kc, vc = _in_proj(
            x2d, lp["w_ext"], lp["ln1"], lp["gain"], shared["cos"], shared["sin"], shared["ones"])
        seq = lambda z: z.reshape(b, SEQ, z.shape[-1])
        oa = _win_attn(seq(qa), seq(ka), seq(va), shared["win_bias"], lp["sink2"])
        ob = _nbr_attn(seq(qb), seq(kb), seq(vb), lp["nbr_bias"])
        oc = _dense_attn(seq(qc), seq(kc), seq(vc))
        flat = lambda z: z.reshape(b * SEQ, z.shape[-1])
        xn, hrow, cls = _out_proj(flat(oa), flat(ob), flat(oc), x2d, lp["ga"], lp["gb"], lp["gc"],
                             lp["wa"], lp["wb"], lp["wc"], lp["ln2"], lp["wr"], lp["br"])
        wg, wu, wd = moe_w[l]
        x2d = _moe(hrow, cls, xn, wg, wu, wd)
    return x2d.reshape(b, SEQ, D_MODEL)


def kernel(x_prompt, x_sample, ln1, w_in, qk_gain, sink, rpb, out_gain, w_out, ln2, w_router_group,
           b_router_group, w_router_expert, b_router_expert, w_gate, w_up, w_down):
    depth = w_in.shape[0]
    params = [_layer_params(l, w_in, ln1, qk_gain, sink, rpb, out_gain, w_out, ln2, w_router_group,
                            b_router_group, w_router_expert, b_router_expert) for l in range(depth)]
    cos_t, sin_t = _rope_tables()
    shared = dict(cos=cos_t, sin=sin_t, ones=_block_ones(), win_bias=_win_bias_table())
    moe_w = [(w_gate[l].astype(BF16), w_up[l].astype(BF16), w_down[l].astype(BF16)) for l in range(depth)]
    y_prompt = _trunk(x_prompt, params, shared, moe_w)
    y_sample = _trunk(x_sample, params, shared, moe_w)
    return (y_prompt, y_sample)
```

```python
import functools
import math

import numpy as np
import jax
import jax.numpy as jnp
from jax import lax
from jax.experimental import pallas as pl
from jax.experimental.pallas import tpu as pltpu

D_MODEL = 1024
SEQ = 4096
HEAD_DIM = 64
GRID_W = 64
GRID_ROWS = SEQ // GRID_W
A_HEADS, A_KV, A_WINDOW = 6, 2, 128
B_HEADS, B_ROWS, B_COLS = 4, 8, 16
C_HEADS, C_KV = 6, 2
ROPE_THETA = 10000.0
N_GROUPS, EXPERTS_PER_GROUP = 4, 4
N_EXPERTS = N_GROUPS * EXPERTS_PER_GROUP
D_EXPERT = D_MODEL // 2
EPS = 1e-6
NEG = -1e30
LOG2E = math.log2(math.e)

A_WIDTH = A_HEADS * HEAD_DIM
B_WIDTH = B_HEADS * HEAD_DIM
C_WIDTH = C_HEADS * HEAD_DIM
KV_WIDTH = A_KV * HEAD_DIM

LANES = 128
MXU_DIM = 256
VMEM_LIMIT = 56 * 1024 * 1024

PAIR_ORDER = (0, 3, 1, 4, 2, 5)

TM_PROJ = 1024
LOCAL_BATCH = 4
TQ_A = 256
KW_A = 512
TQ_B = 256
KW_B = 768
TQ_C = 512
KC_C = 512
TM_MOE = 512
TM_COMB = 256
TM_DISP = 2048

PAIR_A = (0, 0, 0, 1, 1, 2)
PAIR_B = (1, 2, 3, 2, 3, 3)
PAIRS_PER_GROUP = len(PAIR_A)
N_CLASSES = N_GROUPS * PAIRS_PER_GROUP
CLASS_LANE = N_GROUPS + N_EXPERTS
SUBLANES = 8
GATE_ROW = D_MODEL // 2 // LANES
GATHER_SLOTS = 3
DMA_THREADS = 2

BF16 = jnp.bfloat16
F32 = jnp.float32


def _cparams(sem):
    return pltpu.CompilerParams(dimension_semantics=sem, vmem_limit_bytes=VMEM_LIMIT)


N_NORMED = 1536
N_PROJ = 2048


def _in_proj_kernel(x_ref, w_ref, ln_ref, gain_ref, cos_ref, sin_ref, ones_ref,
                    qa_ref, ka_ref, va_ref, qb_ref, kb_ref, vb_ref, qc_ref, kc_ref, vc_ref):
    x = x_ref[...]
    ms = jnp.mean(x * x, axis=-1, keepdims=True)
    h = (x * lax.rsqrt(ms + EPS) * ln_ref[...]).astype(BF16)
    p = jnp.dot(h, w_ref[...], preferred_element_type=F32)
    g = gain_ref[...]
    ones_blk = ones_ref[...]

    def chunk(c):
        return p[:, MXU_DIM * c:MXU_DIM * (c + 1)]

    def inv_rms(pc):
        ss = jnp.dot((pc * pc).astype(BF16), ones_blk, preferred_element_type=F32)
        return lax.rsqrt(ss * (1.0 / HEAD_DIM) + EPS)

    r = [inv_rms(chunk(c)) for c in range(N_NORMED // MXU_DIM)]
    pn = [chunk(c) * r[c] * g[:, MXU_DIM * c:MXU_DIM * (c + 1)] for c in range(N_NORMED // MXU_DIM)]
    qa_ref[:, 0:256] = pn[0].astype(BF16)
    qa_ref[:, 256:384] = pn[1][:, :128].astype(BF16)
    ka_ref[...] = pn[1][:, 128:].astype(BF16)
    qb_ref[...] = pn[2].astype(BF16)
    kb_ref[...] = pn[3].astype(BF16)
    cos = cos_ref[...]
    sin = sin_ref[...]
    cos2 = jnp.concatenate([cos, cos], axis=1)
    sin2 = jnp.concatenate([sin, sin], axis=1)
    lane = lax.broadcasted_iota(jnp.int32, (1, MXU_DIM), 1)
    first_half = (lane % (HEAD_DIM // 2)) < HEAD_DIM // 4

    def rotary_partner(z):
        return jnp.where(first_half, pltpu.roll(z, MXU_DIM - HEAD_DIM // 4, axis=1), pltpu.roll(z, HEAD_DIM // 4, axis=1))

    c4 = pn[4] * cos2 + rotary_partner(pn[4]) * sin2
    c5 = pn[5] * cos2 + rotary_partner(pn[5]) * sin2
    qc_ref[:, 0:256] = c4.astype(BF16)
    qc_ref[:, 256:384] = c5[:, :128].astype(BF16)
    kc_ref[...] = c5[:, 128:].astype(BF16)
    va_ref[...] = p[:, 1536:1664].astype(BF16)
    vb_ref[...] = p[:, 1664:1920].astype(BF16)
    vc_ref[:, 0:128] = p[:, 1920:2048].astype(BF16)
    vc_ref[:, 128:256] = jnp.ones((x.shape[0], 128), BF16)


def _in_proj(x2d, w_ext, ln, gain, cos_t, sin_t, ones_blk):
    n = x2d.shape[0]
    tm = TM_PROJ
    tiles_per_seq = SEQ // tm
    row = lambda i: (i, 0)
    fixed = lambda i: (0, 0)
    pos = lambda i: (i % tiles_per_seq, 0)
    widths = (A_WIDTH, KV_WIDTH, KV_WIDTH, B_WIDTH, B_WIDTH, B_WIDTH, C_WIDTH, KV_WIDTH, 2 * KV_WIDTH)
    return pl.pallas_call(
        _in_proj_kernel,
        grid=(n // tm,),
        in_specs=[
            pl.BlockSpec((tm, D_MODEL), row),
            pl.BlockSpec((D_MODEL, N_PROJ), fixed),
            pl.BlockSpec((1, D_MODEL), fixed),
            pl.BlockSpec((1, N_NORMED), fixed),
            pl.BlockSpec((tm, LANES), pos),
            pl.BlockSpec((tm, LANES), pos),
            pl.BlockSpec((MXU_DIM, MXU_DIM), fixed),
        ],
        out_specs=[pl.BlockSpec((tm, w), row) for w in widths],
        out_shape=[jax.ShapeDtypeStruct((n, w), BF16) for w in widths],
        compiler_params=_cparams(("parallel",)),
        name="in_proj",
    )(x2d, w_ext, ln, gain, cos_t, sin_t, ones_blk)


def _half_masks():
    lane = lax.broadcasted_iota(jnp.int32, (1, LANES), 1)
    lo = lane < HEAD_DIM
    return lo, lo.astype(BF16), (~lo).astype(BF16)


def _win_attn_kernel(sink_ref, q_ref, k_ref, v_ref, bias_ref, o_ref):
    j = pl.program_id(1)
    start = pl.multiple_of(jnp.clip(j * TQ_A - A_WINDOW, 0, SEQ - KW_A), 128)
    lo, m_lo, m_hi = _half_masks()
    low_rows = lax.broadcasted_iota(jnp.int32, (2 * TQ_A, 1), 0) < TQ_A
    for bb in range(LOCAL_BATCH):
        kw = k_ref[bb, pl.ds(start, KW_A), :]
        vw = jnp.concatenate([v_ref[bb, pl.ds(start, KW_A), :], jnp.ones((KW_A, LANES), BF16)], axis=1)
        for p in range(A_HEADS // 2):
            qblk = q_ref[bb, :, LANES * p:LANES * (p + 1)]
            q2 = jnp.concatenate([qblk * m_lo, qblk * m_hi], axis=0)
            s = lax.dot_general(q2, kw, (((1,), (1,)), ((), ())), preferred_element_type=F32)
            s = s + bias_ref[0, p]
            sk = jnp.where(low_rows, sink_ref[2 * p], sink_ref[2 * p + 1])
            m = jnp.maximum(jnp.max(s, axis=-1, keepdims=True), sk)
            e = jnp.exp2(s - m).astype(BF16)
            ol = jnp.dot(e, vw, preferred_element_type=F32)
            o2 = ol[:, :LANES] / (ol[:, LANES:] + jnp.exp2(sk - m))
            o_ref[bb, :, LANES * p:LANES * (p + 1)] = jnp.where(lo, o2[:TQ_A], o2[TQ_A:]).astype(BF16)


def _win_attn(qa, ka, va, bias, sink2):
    b = qa.shape[0]
    nq = SEQ // TQ_A
    variant = lambda bi, j: (jnp.where(j == 0, 0, jnp.where(j == nq - 1, 2, 1)), 0, 0, 0)
    bb = LOCAL_BATCH
    return pl.pallas_call(
        _win_attn_kernel,
        grid=(b // bb, nq),
        in_specs=[
            pl.BlockSpec(memory_space=pltpu.SMEM),
            pl.BlockSpec((bb, TQ_A, A_WIDTH), lambda bi, j: (bi, j, 0)),
            pl.BlockSpec((bb, SEQ, KV_WIDTH), lambda bi, j: (bi, 0, 0)),
            pl.BlockSpec((bb, SEQ, KV_WIDTH), lambda bi, j: (bi, 0, 0)),
            pl.BlockSpec((1, A_HEADS // 2, 2 * TQ_A, KW_A), variant),
        ],
        out_specs=pl.BlockSpec((bb, TQ_A, A_WIDTH), lambda bi, j: (bi, j, 0)),
        out_shape=jax.ShapeDtypeStruct((b, SEQ, A_WIDTH), BF16),
        compiler_params=_cparams(("parallel", "arbitrary")),
        name="win_attn",
    )(sink2, qa, ka, va, bias)


def _win_bias_table():
    slopes = np.array([2.0 ** (-8.0 * (n + 1) / A_HEADS) for n in range(A_HEADS)], np.float32)[list(PAIR_ORDER)]
    i = np.arange(TQ_A)[:, None]
    jj = np.arange(KW_A)[None, :]
    tabs = []
    for off in (0, A_WINDOW, KW_A - TQ_A):
        dist = np.abs(off + i - jj).astype(np.float32)
        tab = np.where(dist[None] <= A_WINDOW, -slopes[:, None, None] * dist[None] * LOG2E, NEG)
        tabs.append(tab)
    return jnp.asarray(np.stack(tabs).astype(np.float32).reshape(3, A_HEADS // 2, 2 * TQ_A, KW_A))


def _nbr_attn_kernel(q_ref, k_ref, v_ref, bias_ref, o_ref):
    j = pl.program_id(1)
    rows_per_tile = TQ_B // GRID_W
    krow0 = jnp.clip(j * rows_per_tile - B_ROWS // 2, 0, GRID_ROWS - KW_B // GRID_W)
    start = pl.multiple_of(krow0 * GRID_W, 256)
    lo, m_lo, m_hi = _half_masks()
    for bb in range(LOCAL_BATCH):
        for p in range(B_HEADS // 2):
            qblk = q_ref[bb, :, LANES * p:LANES * (p + 1)]
            kw = k_ref[bb, pl.ds(start, KW_B), LANES * p:LANES * (p + 1)]
            vw = jnp.concatenate([v_ref[bb, pl.ds(start, KW_B), LANES * p:LANES * (p + 1)],
                                  jnp.ones((KW_B, LANES), BF16)], axis=1)
            q2 = jnp.concatenate([qblk * m_lo, qblk * m_hi], axis=0)
            s = lax.dot_general(q2, kw, (((1,), (1,)), ((), ())), preferred_element_type=F32)
            s = s + bias_ref[0, p]
            m = jnp.max(s, axis=-1, keepdims=True)
            e = jnp.exp2(s - m).astype(BF16)
            ol = jnp.dot(e, vw, preferred_element_type=F32)
            o2 = ol[:, :LANES] / ol[:, LANES:]
            o_ref[bb, :, LANES * p:LANES * (p + 1)] = jnp.where(lo, o2[:TQ_B], o2[TQ_B:]).astype(BF16)


def _nbr_attn(qb, kb, vb, bias):
    b = qb.shape[0]
    nq = SEQ // TQ_B
    variant = lambda bi, j: (jnp.where(j == 0, 0, jnp.where(j == nq - 1, 2, 1)), 0, 0, 0)
    bb = LOCAL_BATCH
    return pl.pallas_call(
        _nbr_attn_kernel,
        grid=(b // bb, nq),
        in_specs=[
            pl.BlockSpec((bb, TQ_B, B_WIDTH), lambda bi, j: (bi, j, 0)),
            pl.BlockSpec((bb, SEQ, B_WIDTH), lambda bi, j: (bi, 0, 0)),
            pl.BlockSpec((bb, SEQ, B_WIDTH), lambda bi, j: (bi, 0, 0)),
            pl.BlockSpec((1, B_HEADS // 2, 2 * TQ_B, KW_B), variant),
        ],
        out_specs=pl.BlockSpec((bb, TQ_B, B_WIDTH), lambda bi, j: (bi, j, 0)),
        out_shape=jax.ShapeDtypeStruct((b, SEQ, B_WIDTH), BF16),
        compiler_params=_cparams(("parallel", "arbitrary")),
        name="nbr_attn",
    )(qb, kb, vb, bias)


def _nbr_bias_table(rpb):
    rows_per_tile = TQ_B // GRID_W
    krows = KW_B // GRID_W
    r = rpb.astype(F32) * LOG2E
    edge = GRID_W - B_COLS
    ext = jnp.concatenate([jnp.repeat(r[..., :1], edge, axis=-1), r, jnp.repeat(r[..., -1:], edge, axis=-1)], axis=-1)
    col = jnp.stack([ext[..., GRID_W - 1 - q:2 * GRID_W - 1 - q] for q in range(GRID_W)], axis=2)
    tabs = []
    for r_first in (0, rows_per_tile, GRID_ROWS - rows_per_tile):
        krow0 = int(np.clip(r_first - B_ROWS // 2, 0, GRID_ROWS - krows))
        slabs = []
        for ql in range(rows_per_tile):
            per_k = [col[:, int(np.clip(krow0 + kl - (r_first + ql) + B_ROWS - 1, 0, 2 * B_ROWS - 2))]
                     for kl in range(krows)]
            slabs.append(jnp.concatenate(per_k, axis=-1))
        tab = jnp.concatenate(slabs, axis=1)
        qi = np.arange(TQ_B)
        kj = np.arange(KW_B)
        qr = (r_first + qi // GRID_W)[:, None]
        qcol = (qi % GRID_W)[:, None]
        kr = (krow0 + kj // GRID_W)[None, :]
        kcol = (kj % GRID_W)[None, :]
        r0 = np.clip(qr - B_ROWS // 2, 0, GRID_ROWS - B_ROWS)
        c0 = np.clip(qcol - B_COLS // 2, 0, GRID_W - B_COLS)
        valid = (kr >= r0) & (kr < r0 + B_ROWS) & (kcol >= c0) & (kcol < c0 + B_COLS)
        tabs.append(jnp.where(jnp.asarray(valid)[None], tab, NEG))
    return jnp.stack(tabs).reshape(3, B_HEADS // 2, 2 * TQ_B, KW_B)


def _dense_attn_kernel(q_ref, k_ref, v_ref, o_ref, s_ref):
    lo, m_lo, m_hi = _half_masks()
    nchunk = SEQ // KC_C
    npairs = C_HEADS // 2

    def scores(p):
        qblk = q_ref[0, :, LANES * p:LANES * (p + 1)]
        q2 = jnp.concatenate([qblk * m_lo, qblk * m_hi], axis=0)
        m_run = None
        for c in range(nchunk):
            kc = k_ref[0, KC_C * c:KC_C * (c + 1), :]
            s = lax.dot_general(q2, kc, (((1,), (1,)), ((), ())), preferred_element_type=F32)
            s_ref[p % 2, :, KC_C * c:KC_C * (c + 1)] = s
            for t in range(KC_C // LANES):
                blk = s[:, LANES * t:LANES * (t + 1)]
                m_run = blk if m_run is None else jnp.maximum(m_run, blk)
        return jnp.max(m_run, axis=-1, keepdims=True)

    def weighted_values(p, m):
        acc = None
        for c in range(nchunk):
            e = jnp.exp2(s_ref[p % 2, :, KC_C * c:KC_C * (c + 1)] - m).astype(BF16)
            part = jnp.dot(e, v_ref[0, KC_C * c:KC_C * (c + 1), :], preferred_element_type=F32)
            acc = part if acc is None else acc + part
        on = acc[:, :LANES] / acc[:, LANES:]
        o_ref[0, :, LANES * p:LANES * (p + 1)] = jnp.where(lo, on[:TQ_C], on[TQ_C:]).astype(BF16)

    m_next = scores(0)
    for p in range(npairs):
        m_cur = m_next
        if p + 1 < npairs:
            m_next = scores(p + 1)
        weighted_values(p, m_cur)


def _dense_attn(qc, kc, vc):
    b = qc.shape[0]
    return pl.pallas_call(
        _dense_attn_kernel,
        grid=(b, SEQ // TQ_C),
        in_specs=[
            pl.BlockSpec((1, TQ_C, C_WIDTH), lambda bi, j: (bi, j, 0)),
            pl.BlockSpec((1, SEQ, KV_WIDTH), lambda bi, j: (bi, 0, 0)),
            pl.BlockSpec((1, SEQ, 2 * KV_WIDTH), lambda bi, j: (bi, 0, 0)),
        ],
        out_specs=pl.BlockSpec((1, TQ_C, C_WIDTH), lambda bi, j: (bi, j, 0)),
        out_shape=jax.ShapeDtypeStruct((b, SEQ, C_WIDTH), BF16),
        scratch_shapes=[pltpu.VMEM((2, 2 * TQ_C, SEQ), F32)],
        compiler_params=_cparams(("parallel", "arbitrary")),
        name="dense_attn",
    )(qc, kc, vc)


def _out_proj_kernel(oa_ref, ob_ref, oc_ref, x_ref, ga_ref, gb_ref, gc_ref, wa_ref, wb_ref, wc_ref,
                     ln_ref, wr_ref, br_ref, xn_ref, h_ref, cls_ref):
    def nrm(o_ref, g_ref):
        o = o_ref[...].astype(F32)
        ms = jnp.mean(o * o, axis=-1, keepdims=True)
        return (o * lax.rsqrt(ms + EPS) * g_ref[...]).astype(BF16)

    acc = jnp.dot(nrm(oa_ref, ga_ref), wa_ref[...], preferred_element_type=F32)
    acc = acc + jnp.dot(nrm(ob_ref, gb_ref), wb_ref[...], preferred_element_type=F32)
    acc = acc + jnp.dot(nrm(oc_ref, gc_ref), wc_ref[...], preferred_element_type=F32)
    xn = x_ref[...] + acc
    xn_ref[...] = xn
    ms = jnp.mean(xn * xn, axis=-1, keepdims=True)
    h2 = xn * lax.rsqrt(ms + EPS) * ln_ref[...]
    hi = h2.astype(BF16)
    lo = (h2 - hi.astype(F32)).astype(BF16)
    tm = xn.shape[0]
    half = D_MODEL // 2
    bits_lo = lax.bitcast_convert_type(hi[:, :half].astype(F32), jnp.uint32)
    bits_hi = lax.bitcast_convert_type(hi[:, half:].astype(F32), jnp.uint32)
    words = (bits_hi & jnp.uint32(0xFFFF0000)) | (bits_lo >> 16)
    for s in range(GATE_ROW):
        h_ref[pl.ds(s, tm, stride=SUBLANES), :] = words[:, LANES * s:LANES * (s + 1)]
    for s in range(GATE_ROW + 1, SUBLANES):
        h_ref[pl.ds(s, tm, stride=SUBLANES), :] = jnp.zeros((tm, LANES), jnp.uint32)
    wr = wr_ref[...]
    t = jnp.dot(hi, wr, preferred_element_type=F32)
    u = jnp.dot(lo, wr[:, :LANES], preferred_element_type=F32)
    logits = t[:, :LANES] + t[:, LANES:] + u + br_ref[...]

    lane = lax.broadcasted_iota(jnp.int32, logits.shape, 1).astype(F32)
    big = jnp.float32(3.0e38)
    is_g = lane < N_GROUPS
    gl = jnp.where(is_g, logits, -big)
    mg = jnp.max(gl, axis=-1, keepdims=True)
    grp = jnp.min(jnp.where(gl == mg, lane, big), axis=-1, keepdims=True)
    pg = 1.0 / jnp.sum(jnp.where(is_g, jnp.exp(gl - mg), 0.0), axis=-1, keepdims=True)
    e_lo = N_GROUPS + EXPERTS_PER_GROUP * grp
    sel = (lane >= e_lo) & (lane < e_lo + EXPERTS_PER_GROUP)
    el = jnp.where(sel, logits, -big)
    v1 = jnp.max(el, axis=-1, keepdims=True)
    i1 = jnp.min(jnp.where(el == v1, lane, big), axis=-1, keepdims=True)
    el2 = jnp.where(lane == i1, -big, el)
    v2 = jnp.max(el2, axis=-1, keepdims=True)
    i2 = jnp.min(jnp.where(el2 == v2, lane, big), axis=-1, keepdims=True)
    e21 = jnp.exp(v2 - v1)
    w1 = pg / (1.0 + e21)
    w2 = pg * e21 / (1.0 + e21)
    a = jnp.minimum(i1, i2) - e_lo
    b = jnp.maximum(i1, i2) - e_lo
    cls = grp * PAIRS_PER_GROUP + (a * (7.0 - a) * 0.5 + (b - a - 1.0))
    gates = jnp.where(lane == i1, w1, jnp.where(lane == i2, w2, jnp.where(lane == CLASS_LANE, cls, 0.0)))
    h_ref[pl.ds(GATE_ROW, tm, stride=SUBLANES), :] = lax.bitcast_convert_type(gates, jnp.uint32)
    cls_ref[...] = jnp.broadcast_to(cls, (tm, LANES)).astype(jnp.int32)


def _out_proj(oa, ob, oc, x2d, ga, gb, gc, wa, wb, wc, ln2, wr, br):
    n = x2d.shape[0]
    tm = TM_PROJ
    row = lambda i: (i, 0)
    fixed = lambda i: (0, 0)
    return pl.pallas_call(
        _out_proj_kernel,
        grid=(n // tm,),
        in_specs=[
            pl.BlockSpec((tm, A_WIDTH), row),
            pl.BlockSpec((tm, B_WIDTH), row),
            pl.BlockSpec((tm, C_WIDTH), row),
            pl.BlockSpec((tm, D_MODEL), row),
            pl.BlockSpec((1, A_WIDTH), fixed),
            pl.BlockSpec((1, B_WIDTH), fixed),
            pl.BlockSpec((1, C_WIDTH), fixed),
            pl.BlockSpec((A_WIDTH, D_MODEL), fixed),
            pl.BlockSpec((B_WIDTH, D_MODEL), fixed),
            pl.BlockSpec((C_WIDTH, D_MODEL), fixed),
            pl.BlockSpec((1, D_MODEL), fixed),
            pl.BlockSpec((D_MODEL, 2 * LANES), fixed),
            pl.BlockSpec((1, LANES), fixed),
        ],
        out_specs=[
            pl.BlockSpec((tm, D_MODEL), row),
            pl.BlockSpec((tm * SUBLANES, LANES), row),
            pl.BlockSpec((tm, LANES), row),
        ],
        out_shape=[
            jax.ShapeDtypeStruct((n, D_MODEL), F32),
            jax.ShapeDtypeStruct((n * SUBLANES, LANES), jnp.uint32),
            jax.ShapeDtypeStruct((n, LANES), jnp.int32),
        ],
        compiler_params=_cparams(("parallel",)),
        name="out_proj_router",
    )(oa, ob, oc, x2d, ga, gb, gc, wa, wb, wc, ln2, wr, br)


def _tile_copy(src_hbm, buf, sem, slot, src_row8, dst_tok):
    src = src_hbm.at[pl.ds(pl.multiple_of(src_row8, SUBLANES), SUBLANES), :]
    return pltpu.make_async_copy(src, buf[slot].at[pl.ds(SUBLANES * dst_tok, SUBLANES), :], sem.at[slot])


def _issue_gather(idx_ref, src_hbm, buf, sem, slot, toks):
    for r in range(toks):
        _tile_copy(src_hbm, buf, sem, slot, idx_ref[0, 0, r], r).start(priority=r % DMA_THREADS)


def _wait_gather(src_hbm, buf, sem, slot, toks):
    pltpu.make_async_copy(src_hbm.at[pl.ds(0, SUBLANES * toks), :], buf[slot], sem.at[slot]).wait()


def _tile_row(ref, s, toks):
    return ref[pl.ds(s, toks, stride=SUBLANES), :]


def _dispatch_kernel(ztile_ref, nvalid_ref, pos_ref, h_ref, o_hbm, zbuf, sem):
    tm = TM_DISP
    tile_rows = TM_MOE * SUBLANES
    i = pl.program_id(0)

    @pl.when(i == 0)
    def _():
        zbuf[...] = jnp.zeros(zbuf.shape, zbuf.dtype)

        def zero_tile(row0):
            dst = o_hbm.at[pl.ds(pl.multiple_of(row0, SUBLANES), tile_rows), :]
            return pltpu.make_async_copy(zbuf, dst, sem.at[1])

        for c in range(N_CLASSES):
            @pl.when(ztile_ref[c] >= 0)
            def _(c=c):
                zero_tile(ztile_ref[c]).start()

        n_tiles = o_hbm.shape[0] // tile_rows

        @pl.loop(nvalid_ref[0], n_tiles)
        def _(t):
            zero_tile(t * tile_rows).start()

        for c in range(N_CLASSES):
            @pl.when(ztile_ref[c] >= 0)
            def _(c=c):
                zero_tile(ztile_ref[c]).wait()

        @pl.loop(nvalid_ref[0], n_tiles)
        def _(t):
            zero_tile(t * tile_rows).wait()

    def row_copy(r):
        dst = o_hbm.at[pl.ds(pl.multiple_of(pos_ref[0, 0, r], SUBLANES), SUBLANES), :]
        return pltpu.make_async_copy(h_ref.at[pl.ds(SUBLANES * r, SUBLANES), :], dst, sem.at[0])

    for r in range(tm):
        row_copy(r).start(priority=r % DMA_THREADS)
    pltpu.make_async_copy(h_ref, o_hbm.at[pl.ds(0, tm * SUBLANES), :], sem.at[0]).wait()


def _dispatch(hrow, ztile, nvalid, pos_tiles, nt):
    n = hrow.shape[0] // SUBLANES
    tm = TM_DISP
    grid_spec = pltpu.PrefetchScalarGridSpec(
        num_scalar_prefetch=2,
        grid=(n // tm,),
        in_specs=[
            pl.BlockSpec((1, 1, tm), lambda i, zt, nv: (i, 0, 0), memory_space=pltpu.SMEM),
            pl.BlockSpec((tm * SUBLANES, LANES), lambda i, zt, nv: (i, 0)),
        ],
        out_specs=pl.BlockSpec(memory_space=pl.ANY),
        scratch_shapes=[pltpu.VMEM((TM_MOE * SUBLANES, LANES), jnp.uint32), pltpu.SemaphoreType.DMA((2,))],
    )
    return pl.pallas_call(
        _dispatch_kernel,
        grid_spec=grid_spec,
        out_shape=jax.ShapeDtypeStruct((nt * TM_MOE * SUBLANES, LANES), jnp.uint32),
        compiler_params=_cparams(("arbitrary",)),
        name="moe_dispatch",
    )(ztile, nvalid, pos_tiles, hrow)


def _moe_kernel(ea_ref, eb_ref, nvalid_ref, h_ref, wga_ref, wua_ref, wda_ref, wgb_ref, wub_ref, wdb_ref, y_ref):
    tm = TM_MOE
    i = pl.program_id(0)
    nvalid = nvalid_ref[0]

    @pl.when(i < nvalid)
    def _():
        parts_lo, parts_hi = [], []
        for s in range(D_MODEL // 2 // LANES):
            w = _tile_row(h_ref, s, tm)
            parts_lo.append(lax.bitcast_convert_type(w << 16, F32).astype(BF16))
            parts_hi.append(lax.bitcast_convert_type(w & jnp.uint32(0xFFFF0000), F32).astype(BF16))
        x = jnp.concatenate(parts_lo + parts_hi, axis=1)
        gates = lax.bitcast_convert_type(_tile_row(h_ref, GATE_ROW, tm), F32)
        lane = lax.broadcasted_iota(jnp.int32, gates.shape, 1)

        def expert(wg_ref, wu_ref, wd_ref, e):
            g = jnp.dot(x, wg_ref[0], preferred_element_type=F32)
            u = jnp.dot(x, wu_ref[0], preferred_element_type=F32)
            act = (g / (1.0 + jnp.exp(-g)) * u).astype(BF16)
            y = jnp.dot(act, wd_ref[0], preferred_element_type=F32)
            ge = jnp.sum(jnp.where(lane == e + N_GROUPS, gates, 0.0), axis=-1, keepdims=True)
            return ge * y

        y = expert(wga_ref, wua_ref, wda_ref, ea_ref[i]) + expert(wgb_ref, wub_ref, wdb_ref, eb_ref[i])
        for s in range(SUBLANES):
            y_ref[pl.ds(s, tm, stride=SUBLANES), :] = y[:, LANES * s:LANES * (s + 1)]

    @pl.when(i >= nvalid)
    def _():
        y_ref[...] = jnp.zeros(y_ref.shape, F32)


def _moe_experts(h_sorted, ea, eb, nvalid, wg, wu, wd):
    tm = TM_MOE
    nt = ea.shape[0]
    w_in_spec = lambda sel: pl.BlockSpec((1, D_MODEL, D_EXPERT), sel)
    w_out_spec = lambda sel: pl.BlockSpec((1, D_EXPERT, D_MODEL), sel)
    sel_a = lambda i, ea, eb, nv: (ea[i], 0, 0)
    sel_b = lambda i, ea, eb, nv: (eb[i], 0, 0)
    rows = lambda i, ea, eb, nv: (jnp.minimum(i, nv[0] - 1), 0)
    grid_spec = pltpu.PrefetchScalarGridSpec(
        num_scalar_prefetch=3,
        grid=(nt,),
        in_specs=[
            pl.BlockSpec((tm * SUBLANES, LANES), rows),
            w_in_spec(sel_a), w_in_spec(sel_a), w_out_spec(sel_a),
            w_in_spec(sel_b), w_in_spec(sel_b), w_out_spec(sel_b),
        ],
        out_specs=pl.BlockSpec((tm * SUBLANES, LANES), lambda i, ea, eb, nv: (i, 0)),
    )
    return pl.pallas_call(
        _moe_kernel,
        grid_spec=grid_spec,
        out_shape=jax.ShapeDtypeStruct((nt * tm * SUBLANES, LANES), F32),
        compiler_params=_cparams(("arbitrary",)),
        name="moe_experts",
    )(ea, eb, nvalid, h_sorted, wg, wu, wd, wg, wu, wd)


def _combine_kernel(pos0_ref, pos1_ref, posn_ref, xn_ref, y_hbm, o_ref, buf0, buf1, buf2, sem):
    buf = (buf0, buf1, buf2)
    tm = TM_COMB
    i = pl.program_id(0)

    @pl.when(i == 0)
    def _():
        _issue_gather(pos0_ref, y_hbm, buf, sem, 0, tm)
        _issue_gather(pos1_ref, y_hbm, buf, sem, 1, tm)

    for slot in range(GATHER_SLOTS):
        @pl.when(i % GATHER_SLOTS == slot)
        def _(slot=slot):
            _wait_gather(y_hbm, buf, sem, slot, tm)
            _issue_gather(posn_ref, y_hbm, buf, sem, (slot + 2) % GATHER_SLOTS, tm)
            for s in range(SUBLANES):
                cols = slice(LANES * s, LANES * (s + 1))
                o_ref[:, cols] = xn_ref[:, cols] + _tile_row(buf[slot], s, tm)

            @pl.when(i == pl.num_programs(0) - 1)
            def _():
                _wait_gather(y_hbm, buf, sem, (slot + 1) % GATHER_SLOTS, tm)
                _wait_gather(y_hbm, buf, sem, (slot + 2) % GATHER_SLOTS, tm)


def _combine(pos_tiles, xn, y_sorted):
    n = xn.shape[0]
    tm = TM_COMB
    smem_blk = lambda f: pl.BlockSpec((1, 1, tm), f, memory_space=pltpu.SMEM)
    return pl.pallas_call(
        _combine_kernel,
        grid=(n // tm,),
        in_specs=[
            smem_blk(lambda i: (0, 0, 0)),
            smem_blk(lambda i: (1, 0, 0)),
            smem_blk(lambda i: (i + 2, 0, 0)),
            pl.BlockSpec((tm, D_MODEL), lambda i: (i, 0)),
            pl.BlockSpec(memory_space=pl.ANY),
        ],
        out_specs=pl.BlockSpec((tm, D_MODEL), lambda i: (i, 0)),
        out_shape=jax.ShapeDtypeStruct((n, D_MODEL), F32),
        scratch_shapes=[pltpu.VMEM((tm * SUBLANES, LANES), F32) for _ in range(GATHER_SLOTS)]
        + [pltpu.SemaphoreType.DMA((GATHER_SLOTS,))],
        compiler_params=_cparams(("arbitrary",)),
        name="moe_combine",
    )(pos_tiles, pos_tiles, pos_tiles, xn, y_sorted)


def _moe_plan(cls, n):
    tm = TM_MOE
    nt = n // tm + N_CLASSES
    onehot = (cls[:, None] == jnp.arange(N_CLASSES, dtype=jnp.int32)[None, :]).astype(jnp.int32)
    counts = jnp.sum(onehot, axis=0)
    rank = jnp.sum(jnp.cumsum(onehot, axis=0) * onehot, axis=1) - 1
    tiles = (counts + tm - 1) // tm
    tile_end = jnp.cumsum(tiles)
    tile_start = tile_end - tiles
    pos = jnp.sum(onehot * tile_start[None, :], axis=1) * tm + rank
    pad = GATHER_SLOTS - 1
    ztile = jnp.where(tiles > 0, (tile_end - 1) * (tm * SUBLANES), -1).astype(jnp.int32)
    nvalid = tile_end[-1]
    tile_id = jnp.minimum(jnp.arange(nt, dtype=jnp.int32), nvalid - 1)
    tcls = jnp.sum((tile_id[:, None] >= tile_end[None, :]).astype(jnp.int32), axis=1)
    grp, pair = tcls // PAIRS_PER_GROUP, tcls % PAIRS_PER_GROUP
    pair_a = jnp.asarray(PAIR_A, jnp.int32)
    pair_b = jnp.asarray(PAIR_B, jnp.int32)
    ea = grp * EXPERTS_PER_GROUP + jnp.sum((pair[:, None] == jnp.arange(PAIRS_PER_GROUP)[None]) * pair_a[None], axis=1)
    eb = grp * EXPERTS_PER_GROUP + jnp.sum((pair[:, None] == jnp.arange(PAIRS_PER_GROUP)[None]) * pair_b[None], axis=1)
    pos8 = SUBLANES * pos
    pos_disp = pos8.reshape(n // TM_DISP, 1, TM_DISP)
    pos_comb = jnp.concatenate([pos8, jnp.zeros((pad * TM_COMB,), jnp.int32)]).reshape(n // TM_COMB + pad, 1, TM_COMB)
    return (ea.astype(jnp.int32), eb.astype(jnp.int32), nvalid.reshape(1).astype(jnp.int32), ztile,
            pos_disp, pos_comb)


def _moe(hrow, cls, xn, wg, wu, wd):
    n = xn.shape[0]
    ea, eb, nvalid, ztile, pos_disp, pos_comb = _moe_plan(cls[:, 0], n)
    h_sorted = _dispatch(hrow, ztile, nvalid, pos_disp, ea.shape[0])
    y_sorted = _moe_experts(h_sorted, ea, eb, nvalid, wg, wu, wd)
    return _combine(pos_comb, xn, y_sorted)


def _head_cols(base, heads):
    return np.concatenate([base + HEAD_DIM * h + np.arange(HEAD_DIM) for h in heads])


def _proj_columns():
    o_qa, o_ka, o_va = 0, A_WIDTH, A_WIDTH + KV_WIDTH
    o_qb = o_va + KV_WIDTH
    o_kb, o_vb = o_qb + B_WIDTH, o_qb + 2 * B_WIDTH
    o_qc = o_vb + B_WIDTH
    o_kc, o_vc = o_qc + C_WIDTH, o_qc + C_WIDTH + KV_WIDTH
    nat2, nat4 = range(2), range(4)
    return np.concatenate([
        _head_cols(o_qa, PAIR_ORDER), _head_cols(o_ka, nat2),
        _head_cols(o_qb, nat4), _head_cols(o_kb, nat4),
        _head_cols(o_qc, PAIR_ORDER), _head_cols(o_kc, nat2),
        _head_cols(o_va, nat2), _head_cols(o_vb, nat4), _head_cols(o_vc, nat2),
    ])


def _rope_tables():
    nf = HEAD_DIM // 4
    inv = (np.float32(ROPE_THETA) ** (-np.arange(nf, dtype=np.float32) / np.float32(nf))).astype(np.float32)
    pos = np.arange(SEQ)
    ang_r = (pos // GRID_W).astype(np.float32)[:, None] * inv[None, :]
    ang_c = (pos % GRID_W).astype(np.float32)[:, None] * inv[None, :]
    cos = np.concatenate([np.cos(ang_r)] * 2 + [np.cos(ang_c)] * 2, axis=-1).astype(np.float32)
    sin = np.concatenate([-np.sin(ang_r), np.sin(ang_r), -np.sin(ang_c), np.sin(ang_c)], axis=-1).astype(np.float32)
    return jnp.asarray(np.concatenate([cos, cos], axis=-1)), jnp.asarray(np.concatenate([sin, sin], axis=-1))


def _layer_params(l, w_in, ln1, qk_gain, sink, rpb, out_gain, w_out, ln2, w_rg, b_rg, w_re, b_re):
    cols = _proj_columns()
    w_ext = w_in[l][:, cols].astype(BF16)
    qs = HEAD_DIM ** -0.5 * LOG2E
    g = qk_gain[l].astype(F32)
    gain = jnp.concatenate([
        jnp.tile(g[0, 0], A_HEADS) * qs, jnp.tile(g[0, 1], A_KV),
        jnp.tile(g[1, 0], B_HEADS) * qs, jnp.tile(g[1, 1], B_HEADS),
        jnp.tile(g[2, 0], C_HEADS) * qs, jnp.tile(g[2, 1], C_KV),
    ])[None, :]
    order = list(PAIR_ORDER)
    sink2 = sink[l].astype(F32)[jnp.asarray(order)] * LOG2E
    rows_a = _head_cols(0, PAIR_ORDER)
    rows_b = A_WIDTH + np.arange(B_WIDTH)
    rows_c = _head_cols(A_WIDTH + B_WIDTH, PAIR_ORDER)
    og = out_gain[l].astype(F32)
    wo = w_out[l]
    wr = jnp.concatenate([w_rg[l], w_re[l], jnp.zeros((D_MODEL, LANES - N_GROUPS - N_EXPERTS), F32)], axis=1)
    wr_hi = wr.astype(BF16)
    wr_lo = (wr - wr_hi.astype(F32)).astype(BF16)
    br = jnp.concatenate([b_rg[l].astype(F32), b_re[l].astype(F32),
                          jnp.zeros((LANES - N_GROUPS - N_EXPERTS,), F32)])[None, :]
    return dict(
        w_ext=w_ext, ln1=ln1[l][None, :], gain=gain, sink2=sink2, nbr_bias=_nbr_bias_table(rpb[l]),
        ga=og[rows_a][None, :], gb=og[rows_b][None, :], gc=og[rows_c][None, :],
        wa=wo[rows_a].astype(BF16), wb=wo[rows_b].astype(BF16), wc=wo[rows_c].astype(BF16),
        ln2=ln2[l][None, :], wr=jnp.concatenate([wr_hi, wr_lo], axis=1), br=br,
    )


def _block_ones():
    idx = np.arange(MXU_DIM) // HEAD_DIM
    return jnp.asarray((idx[:, None] == idx[None, :]).astype(np.float32)).astype(BF16)


def _trunk(x, params, shared, moe_w):
    b = x.shape[0]
    x2d = x.reshape(b * SEQ, D_MODEL)
    for l, lp in enumerate(params):
        qa, ka, va, qb, kb, vb, qc, kc, vc = _in_proj(
            x2d, lp["w_ext"], lp["ln1"], lp["gain"], shared["cos"], shared["sin"], shared["ones"])
        seq = lambda z: z.reshape(b, SEQ, z.shape[-1])
        oa = _win_attn(seq(qa), seq(ka), seq(va), shared["win_bias"], lp["sink2"])
        ob = _nbr_attn(seq(qb), seq(kb), seq(vb), lp["nbr_bias"])
        oc = _dense_attn(seq(qc), seq(kc), seq(vc))
        flat = lambda z: z.reshape(b * SEQ, z.shape[-1])
        xn, hrow, cls = _out_proj(flat(oa), flat(ob), flat(oc), x2d, lp["ga"], lp["gb"], lp["gc"],
                             lp["wa"], lp["wb"], lp["wc"], lp["ln2"], lp["wr"], lp["br"])
        wg, wu, wd = moe_w[l]
        x2d = _moe(hrow, cls, xn, wg, wu, wd)
    return x2d.reshape(b, SEQ, D_MODEL)


def kernel(x_prompt, x_sample, ln1, w_in, qk_gain, sink, rpb, out_gain, w_out, ln2, w_router_group,
           b_router_group, w_router_expert, b_router_expert, w_gate, w_up, w_down):
    depth = w_in.shape[0]
    params = [_layer_params(l, w_in, ln1, qk_gain, sink, rpb, out_gain, w_out, ln2, w_router_group,
                            b_router_group, w_router_expert, b_router_expert) for l in range(depth)]
    cos_t, sin_t = _rope_tables()
    shared = dict(cos=cos_t, sin=sin_t, ones=_block_ones(), win_bias=_win_bias_table())
    moe_w = [(w_gate[l].astype(BF16), w_up[l].astype(BF16), w_down[l].astype(BF16)) for l in range(depth)]
    y_prompt = _trunk(x_prompt, params, shared, moe_w)
    y_sample = _trunk(x_sample, params, shared, moe_w)
    return (y_prompt, y_sample)
```
